```python
import math
import jax, jax.numpy as jnp
from jax import lax
import numpy as np

D_MODEL = 1024
BATCH = 4
SEQ = 8192
DEPTH = 4
DEC_BATCH = 32
DEC_SEQ = 16
PAST_LEN = 4096

CHUNK = 64
N_MIXERS = 4
N_LAYERS_A = len(range(0, DEPTH, N_MIXERS))
N_LAYERS_B = len(range(1, DEPTH, N_MIXERS))
N_LAYERS_C = len(range(2, DEPTH, N_MIXERS))
N_LAYERS_D = len(range(3, DEPTH, N_MIXERS))
GMLP_CHUNK = 128
GMLP_HALF = 3 * D_MODEL
GMLP_GROUPS = 8
GMLP_GROUP_W = GMLP_HALF // GMLP_GROUPS
DIFF_HEADS = 8
DIFF_DK = D_MODEL // (2 * DIFF_HEADS)
DIFF_DV = 2 * DIFF_DK
Q_BLOCK = 128
RET_HEADS = 4
RET_DK = D_MODEL // RET_HEADS
RET_DV = 2 * D_MODEL // RET_HEADS
HG_EXPAND = 128
HG_HEADS = D_MODEL // HG_EXPAND
HG_DK = HG_EXPAND
HG_DV = D_MODEL // HG_HEADS
FFN_HIDDEN = ((8 * D_MODEL + 3 * 256 - 1) // (3 * 256)) * 256
EPS = 1e-6
NEG_INF = -1e30
F32 = jnp.float32

kernel_name = 'hybrid_streaming_encoder_step'


def rmsnorm(x, g):
    xf = x.astype(F32)
    y = xf * lax.rsqrt(jnp.mean(xf * xf, axis=-1, keepdims=True) + EPS)
    return (y * g.astype(F32)).astype(x.dtype)


def rms_unit(x):
    xf = x.astype(F32)
    return xf * lax.rsqrt(jnp.mean(xf * xf, axis=-1, keepdims=True) + EPS)


def layernorm(x, g, b):
    xf = x.astype(F32)
    mu = jnp.mean(xf, axis=-1, keepdims=True)
    xc = xf - mu
    var = jnp.mean(xc * xc, axis=-1, keepdims=True)
    return (xc * lax.rsqrt(var + EPS) * g.astype(F32) + b.astype(F32)).astype(x.dtype)


def ada_mod(c, w, b):
    m = jax.nn.silu(c) @ w + b
    return jnp.split(m[:, None, :], 6, axis=-1)


def modulate(x, g, shift, scale):
    return rmsnorm(x, g) * (1.0 + scale) + shift


def swiglu(h, w_in, w_out):
    gate, up = jnp.split(h @ w_in, 2, axis=-1)
    return (jax.nn.silu(gate) * up) @ w_out


def gmlp_mix(h, w_in, ln_g, ln_b, w_s, b_s, w_out):
    B_, S_, _ = h.shape
    z = jax.nn.gelu(h @ w_in, approximate=False)
    u, v = jnp.split(z, 2, axis=-1)
    v = layernorm(v, ln_g, ln_b)
    T = min(S_, GMLP_CHUNK)
    vb = v.reshape(B_, S_ // T, T, GMLP_GROUPS, GMLP_GROUP_W)
    w = jnp.tril(w_s[:, :T, :T])
    mixed = jnp.einsum('gts,bnsgc->bntgc', w, vb) + b_s[:, :T].T[None, None, :, :, None]
    out = u * mixed.reshape(B_, S_, GMLP_HALF)
    return out @ w_out, v


def diff_project(h, w_in):
    B_, S_, _ = h.shape
    q, k, v = jnp.split(h @ w_in, [D_MODEL, 2 * D_MODEL], axis=-1)
    return (q.reshape(B_, S_, 2 * DIFF_HEADS, DIFF_DK),
            k.reshape(B_, S_, 2 * DIFF_HEADS, DIFF_DK),
            v.reshape(B_, S_, DIFF_HEADS, DIFF_DV))


def diff_lambda_value(lam, lam_init):
    lf = lam.astype(F32)
    return jnp.exp(jnp.sum(lf[0] * lf[1])) - jnp.exp(jnp.sum(lf[2] * lf[3])) + lam_init


def diff_weights(s, lam):
    p = jax.nn.softmax(s, axis=-1)
    B_, _, T, K = p.shape
    p = p.reshape(B_, DIFF_HEADS, 2, T, K)
    return p[:, :, 0] - lam * p[:, :, 1]


def diff_out(o, subln_g, lam_init, w_out):
    o = rmsnorm(o, subln_g) * (1.0 - lam_init)
    return o.reshape(o.shape[0], o.shape[1], DIFF_HEADS * DIFF_DV) @ w_out


def diff_attn_prompt(h, w_in, lam_p, subln_g, w_out, lam_init):
    B_, S_, _ = h.shape
    q, k, v = diff_project(h, w_in)
    lam = diff_lambda_value(lam_p, lam_init)
    scale = DIFF_DK ** -0.5
    n_blk = S_ // Q_BLOCK
    key_chunk = jnp.arange(S_) // CHUNK
    q_blocks = q.reshape(B_, n_blk, Q_BLOCK, 2 * DIFF_HEADS, DIFF_DK).swapaxes(0, 1)

    def one_block(args):
        q_blk, blk = args
        q_chunk = (blk * Q_BLOCK + jnp.arange(Q_BLOCK)) // CHUNK
        s = jnp.einsum('bthd,bshd->bhts', q_blk, k).astype(F32) * scale
        s = jnp.where(key_chunk[None, :] <= q_chunk[:, None], s, NEG_INF)
        a = diff_weights(s, lam).astype(v.dtype)
        return jnp.einsum('bhts,bshv->bthv', a, v)

    o = lax.map(one_block, (q_blocks, jnp.arange(n_blk)))
    o = o.swapaxes(0, 1).reshape(B_, S_, DIFF_HEADS, DIFF_DV)
    return diff_out(o, subln_g, lam_init, w_out), k, v


def diff_attn_sample(h, cache_k, cache_v, w_in, lam_p, subln_g, w_out, lam_init):
    B_, L, _ = h.shape
    P = cache_k.shape[1]
    q, k, v = diff_project(h, w_in)
    lam = diff_lambda_value(lam_p, lam_init)
    scale = DIFF_DK ** -0.5
    s_past = jnp.einsum('bthd,bshd->bhts', q, cache_k)
    s_new = jnp.einsum('bthd,bshd->bhts', q, k)
    s = jnp.concatenate([s_past, s_new], axis=-1).astype(F32) * scale
    q_chunk = (P + jnp.arange(L)) // CHUNK
    key_chunk = jnp.arange(P + L) // CHUNK
    s = jnp.where(key_chunk[None, :] <= q_chunk[:, None], s, NEG_INF)
    a = diff_weights(s, lam).astype(v.dtype)
    o = (jnp.einsum('bhts,bshv->bthv', a[..., :P], cache_v)
         + jnp.einsum('bhts,bshv->bthv', a[..., P:], v))
    return diff_out(o, subln_g, lam_init, w_out), k, v


def ret_log_gamma():
    return jnp.log(1.0 - 2.0 ** (-5.0 - jnp.arange(RET_HEADS, dtype=F32)))


def rotate_every_two(x):
    x1 = x[..., ::2]
    x2 = x[..., 1::2]
    return jnp.stack([-x2, x1], axis=-1).reshape(x.shape)


def xpos_rotate(x, pos):
    inv = 1.0 / (10000.0 ** jnp.linspace(0.0, 1.0, RET_DK // 2, dtype=F32))
    ang = pos.astype(F32)[:, None] * jnp.repeat(inv, 2)[None, :]
    sin = jnp.sin(ang)[None, :, None, :]
    cos = jnp.cos(ang)[None, :, None, :]
    xf = x.astype(F32)
    return xf * cos + rotate_every_two(xf) * sin


def retention_project(h, w_in, pos):
    B_, S_, _ = h.shape
    q, k, v, g = jnp.split(h @ w_in, [D_MODEL, 2 * D_MODEL, 4 * D_MODEL], axis=-1)
    q = xpos_rotate(q.reshape(B_, S_, RET_HEADS, RET_DK), pos)
    k = xpos_rotate(k.reshape(B_, S_, RET_HEADS, RET_DK), pos) * (RET_DK ** -0.5)
    v = v.reshape(B_, S_, RET_HEADS, RET_DV).astype(F32)
    return q, k, v, g


def retention_block(q, k, v, state, log_gamma):
    L = q.shape[1]
    t = jnp.arange(L, dtype=F32)
    diff = t[:, None] - t[None, :]
    decay = jnp.where(diff >= 0, jnp.exp(log_gamma[:, None, None] * jnp.maximum(diff, 0.0)), 0.0)
    scores = jnp.einsum('bthd,bshd->bhts', q, k) * decay[None]
    o = jnp.einsum('bhts,bshv->bthv', scores, v)
    cross = jnp.exp(log_gamma[None, :] * (t[:, None] + 1.0))
    o = o + jnp.einsum('bthd,bhdv->bthv', q, state) * cross[None, :, :, None]
    k_dec = k * jnp.exp(log_gamma[None, :] * (L - 1.0 - t[:, None]))[None, :, :, None]
    new_state = (jnp.exp(log_gamma * L)[None, :, None, None] * state
                 + jnp.einsum('bshd,bshv->bhdv', k_dec, v))
    return o, new_state


def retention_output(o, g, w_out):
    B_, S_ = o.shape[:2]
    o = rms_unit(o).reshape(B_, S_, 2 * D_MODEL) * jax.nn.silu(g.astype(F32))
    return o.astype(g.dtype) @ w_out


def retention_prompt(h, w_in, w_out):
    B_, S_, _ = h.shape
    q, k, v, g = retention_project(h, w_in, jnp.arange(S_))
    lg = ret_log_gamma()
    n = S_ // CHUNK

    def to_blocks(a):
        return a.reshape(B_, n, CHUNK, *a.shape[2:]).swapaxes(0, 1)

    def step(state, blk):
        o, state = retention_block(blk[0], blk[1], blk[2], state, lg)
        return state, o

    state0 = jnp.zeros((B_, RET_HEADS, RET_DK, RET_DV), F32)
    state, o = lax.scan(step, state0, (to_blocks(q), to_blocks(k), to_blocks(v)))
    o = o.swapaxes(0, 1).reshape(B_, S_, RET_HEADS, RET_DV)
    return retention_output(o, g, w_out), state.astype(h.dtype)


def retention_sample(h, state, w_in, w_out):
    B_, L, _ = h.shape
    q, k, v, g = retention_project(h, w_in, PAST_LEN + jnp.arange(L))
    o, new_state = retention_block(q, k, v, state.astype(F32), ret_log_gamma())
    return retention_output(o, g, w_out), new_state.astype(state.dtype)


def hgrn_project(h, w_in, lb):
    B_, S_, _ = h.shape
    q, f, i, g = jnp.split(h @ w_in, 4, axis=-1)
    lbf = lb.astype(F32)
    logf = jnp.logaddexp(jnp.log(lbf), jnp.log1p(-lbf) + jax.nn.log_sigmoid(f.astype(F32)))
    k = -jnp.expm1(logf)
    shp = (B_, S_, HG_HEADS, HG_DK)
    return (jax.nn.silu(q.astype(F32)).reshape(shp), k.reshape(shp),
            i.astype(F32).reshape(B_, S_, HG_HEADS, HG_DV), logf.reshape(shp), g)


def gla_block(q, k, v, logf, state):
    L = q.shape[1]
    b = jnp.cumsum(logf, axis=1)
    q_dec = q * jnp.exp(b)
    k_inv = k * jnp.exp(-b)
    causal = jnp.arange(L)[:, None] >= jnp.arange(L)[None, :]
    scores = jnp.where(causal, jnp.einsum('bthd,bshd->bhts', q_dec, k_inv), 0.0)
    o = jnp.einsum('bhts,bshv->bthv', scores, v) + jnp.einsum('bthd,bhdv->bthv', q_dec, state)
    b_last = b[:, -1]
    k_end = k * jnp.exp(b_last[:, None] - b)
    new_state = jnp.exp(b_last)[..., None] * state + jnp.einsum('bshd,bshv->bhdv', k_end, v)
    return o, new_state


def hgrn_output(o, g, norm_g, w_out):
    B_, S_ = o.shape[:2]
    o = rmsnorm(o, norm_g.reshape(HG_HEADS, HG_DV))
    o = o.reshape(B_, S_, D_MODEL) * jax.nn.silu(g.astype(F32))
    return o.astype(g.dtype) @ w_out


def hgrn_prompt(h, w_in, norm_g, w_out, lb):
    B_, S_, _ = h.shape
    q, k, v, logf, g = hgrn_project(h, w_in, lb)
    n = S_ // CHUNK

    def to_blocks(a):
        return a.reshape(B_, n, CHUNK, *a.shape[2:]).swapaxes(0, 1)

    def step(state, blk):
        o, state = gla_block(blk[0], blk[1], blk[2], blk[3], state)
        return state, o

    state0 = jnp.zeros((B_, HG_HEADS, HG_DK, HG_DV), F32)
    state, o = lax.scan(step, state0, (to_blocks(q), to_blocks(k), to_blocks(v), to_blocks(logf)))
    o = o.swapaxes(0, 1).reshape(B_, S_, HG_HEADS, HG_DV)
    return hgrn_output(o, g, norm_g, w_out), state.astype(h.dtype)


def hgrn_sample(h, state, w_in, norm_g, w_out, lb):
    q, k, v, logf, g = hgrn_project(h, w_in, lb)
    o, new_state = gla_block(q, k, v, logf, state.astype(F32))
    return hgrn_output(o, g, norm_g, w_out), new_state.astype(state.dtype)


def setup_inputs(seed: int = 0) -> dict:
    key = jax.random.key(seed)
    keys = iter(jax.random.split(key, 32))

    def nrm(shape, scale=1.0):
        return jax.random.normal(next(keys), shape, F32) * scale

    D = D_MODEL
    F = FFN_HIDDEN
    return {
        'x_prompt': nrm((BATCH, SEQ, D)),
        'x_sample': nrm((DEC_BATCH, DEC_SEQ, D)),
        'cache_k_diff': nrm((N_LAYERS_B, DEC_BATCH, PAST_LEN, 2 * DIFF_HEADS, DIFF_DK)),
        'cache_v_diff': nrm((N_LAYERS_B, DEC_BATCH, PAST_LEN, DIFF_HEADS, DIFF_DV)),
        'state_retention': nrm((N_LAYERS_C, DEC_BATCH, RET_HEADS, RET_DK, RET_DV), 0.5),
        'state_hgrn': nrm((N_LAYERS_D, DEC_BATCH, HG_HEADS, HG_DK, HG_DV), 0.3),
        'c_prompt': nrm((BATCH, D)),
        'c_sample': nrm((DEC_BATCH, D)),
        'w_ada': nrm((DEPTH, D, 6 * D), 0.5 * D ** -0.5),
        'b_ada': nrm((DEPTH, 6 * D), 0.01),
        'norm_gains': 1.0 + nrm((DEPTH, 4, D), 0.05),
        'gmlp_w_in': nrm((N_LAYERS_A, D, 2 * GMLP_HALF), D ** -0.5),
        'gmlp_ln_g': 1.0 + nrm((N_LAYERS_A, GMLP_HALF), 0.05),
        'gmlp_ln_b': nrm((N_LAYERS_A, GMLP_HALF), 0.02),
        'gmlp_w_s': nrm((N_LAYERS_A, GMLP_GROUPS, GMLP_CHUNK, GMLP_CHUNK), GMLP_CHUNK ** -0.5),
        'gmlp_b_s': 1.0 + nrm((N_LAYERS_A, GMLP_GROUPS, GMLP_CHUNK), 0.05),
        'gmlp_w_out': nrm((N_LAYERS_A, GMLP_HALF, D), GMLP_HALF ** -0.5),
        'diff_w_in': nrm((N_LAYERS_B, D, 3 * D), D ** -0.5),
        'diff_lambda': nrm((N_LAYERS_B, 4, DIFF_DK), 0.1),
        'diff_subln': 1.0 + nrm((N_LAYERS_B, DIFF_DV), 0.05),
        'diff_w_out': nrm((N_LAYERS_B, D, D), D ** -0.5),
        'ret_w_in': nrm((N_LAYERS_C, D, 6 * D), D ** -0.5),
        'ret_w_out': nrm((N_LAYERS_C, 2 * D, D), (2 * D) ** -0.5),
        'hgrn_w_in': nrm((N_LAYERS_D, D, 4 * D), D ** -0.5),
        'hgrn_norm': 1.0 + nrm((N_LAYERS_D, D), 0.05),
        'hgrn_w_out': nrm((N_LAYERS_D, D, D), D ** -0.5),
        'hgrn_lower_bounds': nrm((DEPTH, D), 0.1),
        'ffn_w_in': nrm((DEPTH, D, 2 * F), D ** -0.5),
        'ffn_w_out': nrm((DEPTH, F, D), F ** -0.5),
    }


def reference(x_prompt, x_sample, cache_k_diff, cache_v_diff, state_retention, state_hgrn,
              c_prompt, c_sample, w_ada, b_ada, norm_gains,
              gmlp_w_in, gmlp_ln_g, gmlp_ln_b, gmlp_w_s, gmlp_b_s, gmlp_w_out,
              diff_w_in, diff_lambda, diff_subln, diff_w_out,
              ret_w_in, ret_w_out,
              hgrn_w_in, hgrn_norm, hgrn_w_out, hgrn_lower_bounds,
              ffn_w_in, ffn_w_out):
    lb_p = jax.nn.softmax(hgrn_lower_bounds.astype(F32), axis=0)
    lower = jnp.cumsum(lb_p, axis=0) - lb_p[0]

    yp, ys = x_prompt, x_sample
    gmlp_v_s = []
    kd_p, vd_p, kd_s, vd_s = [], [], [], []
    ret_p, ret_s, hg_p, hg_s = [], [], [], []
    for i in range(DEPTH):
        kind, j = i % N_MIXERS, i // N_MIXERS
        mp = ada_mod(c_prompt, w_ada[i], b_ada[i])
        ms = ada_mod(c_sample, w_ada[i], b_ada[i])
        hp = modulate(yp, norm_gains[i, 0], mp[0], mp[1])
        hs = modulate(ys, norm_gains[i, 0], ms[0], ms[1])
        if kind == 0:
            op, _ = gmlp_mix(hp, gmlp_w_in[j], gmlp_ln_g[j], gmlp_ln_b[j], gmlp_w_s[j], gmlp_b_s[j], gmlp_w_out[j])
            os_, v_rows = gmlp_mix(hs, gmlp_w_in[j], gmlp_ln_g[j], gmlp_ln_b[j], gmlp_w_s[j], gmlp_b_s[j], gmlp_w_out[j])
            gmlp_v_s.append(v_rows)
        elif kind == 1:
            lam_init = 0.8 - 0.6 * math.exp(-0.3 * i)
            op, k_new, v_new = diff_attn_prompt(hp, diff_w_in[j], diff_lambda[j], diff_subln[j], diff_w_out[j], lam_init)
            kd_p.append(k_new)
            vd_p.append(v_new)
            os_, k_new, v_new = diff_attn_sample(hs, cache_k_diff[j], cache_v_diff[j], diff_w_in[j], diff_lambda[j],
                                                 diff_subln[j], diff_w_out[j], lam_init)
            kd_s.append(k_new)
            vd_s.append(v_new)
        elif kind == 2:
            op, st = retention_prompt(hp, ret_w_in[j], ret_w_out[j])
            ret_p.append(st)
            os_, st = retention_sample(hs, state_retention[j], ret_w_in[j], ret_w_out[j])
            ret_s.append(st)
        else:
            op, st = hgrn_prompt(hp, hgrn_w_in[j], hgrn_norm[j], hgrn_w_out[j], lower[i])
            hg_p.append(st)
            os_, st = hgrn_sample(hs, state_hgrn[j], hgrn_w_in[j], hgrn_norm[j], hgrn_w_out[j], lower[i])
            hg_s.append(st)
        yp = yp + mp[2] * rmsnorm(op, norm_gains[i, 1])
        ys = ys + ms[2] * rmsnorm(os_, norm_gains[i, 1])
        hp = modulate(yp, norm_gains[i, 2], mp[3], mp[4])
        hs = modulate(ys, norm_gains[i, 2], ms[3], ms[4])
        yp = yp + mp[5] * rmsnorm(swiglu(hp, ffn_w_in[i], ffn_w_out[i]), norm_gains[i, 3])
        ys = ys + ms[5] * rmsnorm(swiglu(hs, ffn_w_in[i], ffn_w_out[i]), norm_gains[i, 3])

    return (yp, ys, jnp.stack(gmlp_v_s), jnp.stack(kd_p), jnp.stack(vd_p), jnp.stack(kd_s), jnp.stack(vd_s),
            jnp.stack(ret_p), jnp.stack(ret_s), jnp.stack(hg_p), jnp.stack(hg_s))
```

```python
import functools
import math

import numpy as np
import jax
import jax.numpy as jnp
from jax import lax
from jax.experimental import pallas as pl
from jax.experimental.pallas import tpu as pltpu

F32 = jnp.float32
BF16 = jnp.bfloat16
EPS = 1e-6
NEG_INF = -1e30

CHUNK = 64
GMLP_CHUNK = 128
GMLP_GROUPS = 8
DIFF_HEADS = 8
RET_HEADS = 4
HG_WIDTH = 128
LANES = 128
MXU_WIDTH = 256
VMEM_CAP = 60 << 20

_NT = (((1,), (1,)), ((), ()))
_TN = (((0,), (0,)), ((), ()))


def _dot(a, b):
    return jnp.dot(a, b, preferred_element_type=F32)


def _dot_nt(a, b):
    return lax.dot_general(a, b, _NT, preferred_element_type=F32)


def _dot_tn(a, b):
    return lax.dot_general(a, b, _TN, preferred_element_type=F32)


def _silu(x):
    return x * jax.nn.sigmoid(x)


def _gelu(x):
    return 0.5 * x * (1.0 + lax.erf(x * (2.0 ** -0.5)))


def _rms_rows(x):
    return x * lax.rsqrt(jnp.mean(x * x, axis=-1, keepdims=True) + EPS)


def _mod_in(x, g, mod_ref, k_shift, k_scale):
    return _rms_rows(x) * g * (1.0 + mod_ref[k_scale]) + mod_ref[k_shift]


def _resid_out(x, o, g, gate):
    return x + gate * (_rms_rows(o) * g)


def _params(n_grid, vmem_bytes):
    return pltpu.CompilerParams(
        dimension_semantics=("arbitrary",) * n_grid,
        vmem_limit_bytes=int(min(max(vmem_bytes, 32 << 20), VMEM_CAP)))


def _const_spec(shape):
    nd = len(shape)
    return pl.BlockSpec(shape, lambda *_: (0,) * nd, pipeline_mode=pl.Buffered(1))


def _row_specs(x, mod, tm):
    _, _, d = x.shape
    r = mod.shape[2]
    x_spec = pl.BlockSpec((None, tm, d), lambda b, i, *_: (b, i, 0))
    if r == 1:
        mod_spec = pl.BlockSpec((None, 6, 1, d), lambda b, i, *_: (b, 0, 0, 0))
    else:
        mod_spec = pl.BlockSpec((None, 6, tm, d), lambda b, i, *_: (b, 0, i, 0))
    return x_spec, mod_spec


def _nbytes(*arrays):
    return sum(int(np.prod(a.shape)) * jnp.dtype(a.dtype).itemsize for a in arrays)


def _ada_kernel(c_ref, w_ref, b_ref, o_ref):
    a = _silu(c_ref[...]).astype(BF16)
    o_ref[...] = _dot(a, w_ref[...].astype(BF16)) + b_ref[...]


def _ada(c_all, w_ada, b_ada):
    depth, d, n = w_ada.shape
    rows = c_all.shape[0]
    tn = n // 4
    return pl.pallas_call(
        _ada_kernel,
        grid=(depth, n // tn),
        in_specs=[pl.BlockSpec((rows, d), lambda l, j: (0, 0)),
                  pl.BlockSpec((None, d, tn), lambda l, j: (l, 0, j)),
                  pl.BlockSpec((None, 1, tn), lambda l, j: (l, 0, j))],
        out_specs=pl.BlockSpec((None, rows, tn), lambda l, j: (l, 0, j)),
        out_shape=jax.ShapeDtypeStruct((depth, rows, n), F32),
        compiler_params=_params(2, 3 * d * tn * 4 + (8 << 20)),
        name="ada_mod",
    )(c_all, w_ada, b_ada.reshape(depth, 1, n))


def _ffn_kernel(x_ref, mod_ref, g_ref, win_ref, wout_ref, y_ref, *, hidden, chunks):
    x = x_ref[...]
    h = _mod_in(x, g_ref[2:3, :], mod_ref, 3, 4).astype(BF16)
    acc = None
    for c0, cw in chunks:
        gate = _dot(h, win_ref[:, c0:c0 + cw])
        up = _dot(h, win_ref[:, hidden + c0:hidden + c0 + cw])
        act = (_silu(gate) * up).astype(BF16)
        part = _dot(act, wout_ref[c0:c0 + cw, :])
        acc = part if acc is None else acc + part
    y_ref[...] = _resid_out(x, acc, g_ref[3:4, :], mod_ref[5])


def _split_chunks(total, width):
    out, c0 = [], 0
    while c0 < total:
        out.append((c0, min(width, total - c0)))
        c0 += width
    return tuple(out)


def _ffn(x, mod, gains, w_in, w_out, tm):
    b, s, d = x.shape
    hidden = w_out.shape[0]
    tm = min(tm, s)
    x_spec, mod_spec = _row_specs(x, mod, tm)
    vmem = _nbytes(w_in, w_out) + 6 * tm * d * 4 + 4 * tm * 1024 * 4 + (8 << 20)
    return pl.pallas_call(
        functools.partial(_ffn_kernel, hidden=hidden, chunks=_split_chunks(hidden, 4 * MXU_WIDTH)),
        grid=(b, s // tm),
        in_specs=[x_spec, mod_spec, _const_spec(gains.shape), _const_spec(w_in.shape),
                  _const_spec(w_out.shape)],
        out_specs=x_spec,
        out_shape=jax.ShapeDtypeStruct(x.shape, F32),
        compiler_params=_params(2, vmem),
        name="ffn",
    )(x, mod, gains, w_in, w_out)


def _gmlp_kernel(x_ref, mod_ref, g_ref, win_ref, lng_ref, lnb_ref, wbd_ref, brow_ref, wout_ref,
                 *out_and_scratch, half, groups, pair, emit_v):
    if emit_v:
        y_ref, vn_ref, v_scr = out_and_scratch
    else:
        y_ref, v_scr = out_and_scratch
    gw = half // groups
    cw = pair * gw
    nblk = half // cw
    x = x_ref[...]
    h = _mod_in(x, g_ref[0:1, :], mod_ref, 0, 1).astype(BF16)
    s1 = None
    s2 = None
    for j in range(nblk):
        v = _gelu(_dot(h, win_ref[:, half + j * cw:half + (j + 1) * cw]))
        v_scr[:, j * cw:(j + 1) * cw] = v
        a1 = jnp.sum(v, axis=-1, keepdims=True)
        a2 = jnp.sum(v * v, axis=-1, keepdims=True)
        s1 = a1 if s1 is None else s1 + a1
        s2 = a2 if s2 is None else s2 + a2
    mu = s1 * (1.0 / half)
    rstd = lax.rsqrt(s2 * (1.0 / half) - mu * mu + EPS)
    acc = None
    for j in range(nblk):
        cols = slice(j * cw, (j + 1) * cw)
        vn = (v_scr[:, cols] - mu) * rstd * lng_ref[:, cols] + lnb_ref[:, cols]
        if emit_v:
            vn_ref[:, cols] = vn
        vnb = vn.astype(BF16)
        u = _gelu(_dot(h, win_ref[:, cols]))
        mixed = []
        for gg in range(pair):
            g = j * pair + gg
            mixed.append(_dot(wbd_ref[g], vnb[:, gg * gw:(gg + 1) * gw]) + brow_ref[:, g:g + 1])
        out = (u * jnp.concatenate(mixed, axis=1)).astype(BF16)
        part = _dot(out, wout_ref[cols, :])
        acc = part if acc is None else acc + part
    y_ref[...] = _resid_out(x, acc, g_ref[1:2, :], mod_ref[2])


def _gmlp(x, mod, gains, w_in, ln_g, ln_b, w_s, b_s, w_out, tm, t_chunk, emit_v):
    b, s, d = x.shape
    half = w_out.shape[0]
    groups = w_s.shape[0]
    n_rep = tm // t_chunk
    wt = jnp.tril(w_s[:, :t_chunk, :t_chunk])
    eye = jnp.eye(n_rep, dtype=w_s.dtype)
    w_bd = (eye[None, :, None, :, None] * wt[:, None, :, None, :]).reshape(groups, tm, tm).astype(BF16)
    b_rows = jnp.tile(b_s[:, :t_chunk].T, (n_rep, 1))
    x_spec, mod_spec = _row_specs(x, mod, tm)
    out_shape = [jax.ShapeDtypeStruct(x.shape, F32)]
    out_specs = [x_spec]
    if emit_v:
        out_shape.append(jax.ShapeDtypeStruct((b, s, half), F32))
        out_specs.append(pl.BlockSpec((None, tm, half), lambda bb, i: (bb, i, 0)))
    vmem = (_nbytes(w_in, w_out, w_bd) + tm * half * 4 * (5 if emit_v else 1)
            + 6 * tm * d * 4 + (12 << 20))
    res = pl.pallas_call(
        functools.partial(_gmlp_kernel, half=half, groups=groups, pair=2, emit_v=emit_v),
        grid=(b, s // tm),
        in_specs=[x_spec, mod_spec, _const_spec(gains.shape), _const_spec(w_in.shape),
                  _const_spec((1, half)), _const_spec((1, half)), _const_spec(w_bd.shape),
                  _const_spec(b_rows.shape), _const_spec(w_out.shape)],
        out_specs=out_specs,
        out_shape=out_shape,
        scratch_shapes=[pltpu.VMEM((tm, half), F32)],
        compiler_params=_params(2, vmem),
        name="gmlp_v" if emit_v else "gmlp",
    )(x, mod, gains, w_in, ln_g.reshape(1, half), ln_b.reshape(1, half), w_bd, b_rows, w_out)
    return res if emit_v else res[0]


def _outproj_kernel(o_ref, x_ref, mod_ref, g_ref, w_ref, y_ref):
    y_ref[...] = _resid_out(x_ref[...], _dot(o_ref[...], w_ref[...]), g_ref[1:2, :], mod_ref[2])


def _outproj(o, x, mod, gains, w_out, tm):
    b, s, d = x.shape
    tm = min(tm, s)
    k = o.shape[-1]
    x_spec, mod_spec = _row_specs(x, mod, tm)
    return pl.pallas_call(
        _outproj_kernel,
        grid=(b, s // tm),
        in_specs=[pl.BlockSpec((None, tm, k), lambda bb, i: (bb, i, 0)), x_spec, mod_spec,
                  _const_spec(gains.shape), _const_spec(w_out.shape)],
        out_specs=x_spec,
        out_shape=jax.ShapeDtypeStruct(x.shape, F32),
        compiler_params=_params(2, _nbytes(w_out) + 8 * tm * d * 4 + 2 * tm * k * 2 + (8 << 20)),
        name="outproj",
    )(o, x, mod, gains, w_out)


def _qkv_kernel(x_ref, mod_ref, g_ref, w_ref, q_ref, k_ref, v_ref, k16_ref, v16_ref, *, scale):
    d = x_ref.shape[-1]
    h = _mod_in(x_ref[...], g_ref[0:1, :], mod_ref, 0, 1).astype(BF16)
    q_ref[...] = (_dot(h, w_ref[:, :d]) * scale).astype(BF16)
    k = _dot(h, w_ref[:, d:2 * d])
    k_ref[...] = k
    k16_ref[...] = k.astype(BF16)
    v = _dot(h, w_ref[:, 2 * d:])
    v_ref[...] = v
    v16_ref[...] = v.astype(BF16)


def _qkv(x, mod, gains, w_in, tm):
    b, s, d = x.shape
    tm = min(tm, s)
    x_spec, mod_spec = _row_specs(x, mod, tm)
    dk = d // (2 * DIFF_HEADS)
    sds = jax.ShapeDtypeStruct
    return pl.pallas_call(
        functools.partial(_qkv_kernel, scale=dk ** -0.5),
        grid=(b, s // tm),
        in_specs=[x_spec, mod_spec, _const_spec(gains.shape), _const_spec(w_in.shape)],
        out_specs=[x_spec] * 5,
        out_shape=[sds(x.shape, BF16), sds(x.shape, F32), sds(x.shape, F32),
                   sds(x.shape, BF16), sds(x.shape, BF16)],
        compiler_params=_params(2, _nbytes(w_in) + 24 * tm * d * 4 + (8 << 20)),
        name="diff_qkv",
    )(x, mod, gains, w_in)


def _diff_lambda(lam_ref, lam_init):
    lp = lam_ref[...]
    e1 = jnp.exp(jnp.sum(lp[0:1, :] * lp[1:2, :], axis=-1, keepdims=True))
    e2 = jnp.exp(jnp.sum(lp[2:3, :] * lp[3:4, :], axis=-1, keepdims=True))
    return e1 - e2 + lam_init


def _stack_q_halves(q, qs_scr, rows):
    lane = lax.broadcasted_iota(jnp.int32, q.shape, 1)
    first = (lane & (LANES - 1)) < (LANES // 2)
    zero = jnp.zeros_like(q)
    qs_scr[0:rows, :] = jnp.where(first, q, zero)
    qs_scr[rows:2 * rows, :] = jnp.where(first, zero, q)


def _flash_init(m_scr, l_scr, acc_scr):
    m_scr[...] = jnp.full(m_scr.shape, NEG_INF, F32)
    l_scr[...] = jnp.zeros(l_scr.shape, F32)
    acc_scr[...] = jnp.zeros(acc_scr.shape, F32)


def _flash_step(qs_scr, k_blk, v_blk, m_scr, l_scr, acc_scr, mask):
    for h in range(DIFF_HEADS):
        sl = slice(LANES * h, LANES * (h + 1))
        s = _dot_nt(qs_scr[:, sl], k_blk(sl))
        if mask is not None:
            s = jnp.where(mask, s, NEG_INF)
        m_old = m_scr[h]
        m_new = jnp.maximum(m_old, jnp.max(s, axis=1, keepdims=True))
        alpha = jnp.exp(m_old - m_new)
        p = jnp.exp(s - m_new)
        l_scr[h] = alpha * l_scr[h] + jnp.sum(p, axis=1, keepdims=True)
        acc_scr[:, sl] = alpha * acc_scr[:, sl] + _dot(p.astype(BF16), v_blk(sl))
        m_scr[h] = m_new


def _flash_finish(l_scr, acc_scr, lam, subln, lam_init, rows):
    outs = []
    for h in range(DIFF_HEADS):
        sl = slice(LANES * h, LANES * (h + 1))
        l = l_scr[h]
        o = acc_scr[0:rows, sl] / l[0:rows] - lam * (acc_scr[rows:2 * rows, sl] / l[rows:2 * rows])
        outs.append((_rms_rows(o) * subln * (1.0 - lam_init)).astype(BF16))
    return jnp.concatenate(outs, axis=1)


def _chunk_mask(rows, cols, row_pos0, col_pos0):
    r = lax.broadcasted_iota(jnp.int32, (2 * rows, cols), 0)
    c = lax.broadcasted_iota(jnp.int32, (2 * rows, cols), 1)
    r = jnp.where(r >= rows, r - rows, r)
    return ((c + col_pos0) // CHUNK) <= ((r + row_pos0) // CHUNK)


def _flash_prompt_kernel(qt_ref, kt_ref, q_ref, k_ref, v_ref, x_ref, mod_ref, g_ref, lam_ref,
                         subln_ref, wout_ref, y_ref, qs_scr, m_scr, l_scr, acc_scr, *, tq, lam_init):
    p = pl.program_id(1)
    qi = qt_ref[p]
    ki = kt_ref[p]

    @pl.when(ki == 0)
    def _():
        _flash_init(m_scr, l_scr, acc_scr)
        _stack_q_halves(q_ref[...], qs_scr, tq)

    k_blk = lambda sl: k_ref[:, sl]
    v_blk = lambda sl: v_ref[:, sl]

    @pl.when(ki < qi)
    def _():
        _flash_step(qs_scr, k_blk, v_blk, m_scr, l_scr, acc_scr, None)

    @pl.when(ki == qi)
    def _():
        _flash_step(qs_scr, k_blk, v_blk, m_scr, l_scr, acc_scr, _chunk_mask(tq, tq, 0, 0))
        lam = _diff_lambda(lam_ref, lam_init)
        oc = _flash_finish(l_scr, acc_scr, lam, subln_ref[...], lam_init, tq)
        y_ref[...] = _resid_out(x_ref[...], _dot(oc, wout_ref[...]), g_ref[1:2, :], mod_ref[2])


def _flash_prompt(q, k16, v16, x, mod, gains, lam_p, subln, w_out, lam_init, tq):
    b, s, d = x.shape
    tq = min(tq, s)
    nq = s // tq
    pairs = [(qi, ki) for qi in range(nq) for ki in range(qi + 1)]
    qt = jnp.asarray([pq for pq, _ in pairs], jnp.int32)
    kt = jnp.asarray([pk for _, pk in pairs], jnp.int32)
    q_spec = pl.BlockSpec((None, tq, d), lambda bb, p, qt_, kt_: (bb, qt_[p], 0))
    kv_spec = pl.BlockSpec((None, tq, d), lambda bb, p, qt_, kt_: (bb, kt_[p], 0))
    mod_spec = pl.BlockSpec((None, 6, 1, d), lambda bb, p, qt_, kt_: (bb, 0, 0, 0))
    vmem = (_nbytes(w_out) + 10 * tq * d * 2 + 4 * tq * d * 4 + 2 * tq * d * 4
            + 2 * DIFF_HEADS * 2 * tq * LANES * 4 + 8 * tq * tq * 4 + (8 << 20))
    grid_spec = pltpu.PrefetchScalarGridSpec(
        num_scalar_prefetch=2,
        grid=(b, len(pairs)),
        in_specs=[q_spec, kv_spec, kv_spec, q_spec, mod_spec, _const_spec(gains.shape),
                  _const_spec(lam_p.shape), _const_spec((1, LANES)), _const_spec(w_out.shape)],
        out_specs=q_spec,
        scratch_shapes=[pltpu.VMEM((2 * tq, d), BF16),
                        pltpu.VMEM((DIFF_HEADS, 2 * tq, 1), F32),
                        pltpu.VMEM((DIFF_HEADS, 2 * tq, 1), F32),
                        pltpu.VMEM((2 * tq, d), F32)])
    return pl.pallas_call(
        functools.partial(_flash_prompt_kernel, tq=tq, lam_init=lam_init),
        grid_spec=grid_spec,
        out_shape=jax.ShapeDtypeStruct(x.shape, F32),
        compiler_params=_params(2, vmem),
        name="diff_flash_prompt",
    )(qt, kt, q, k16, v16, x, mod, gains, lam_p, subln.reshape(1, LANES), w_out)


def _flash_sample_kernel(q_ref, ck_ref, cv_ref, kn_ref, vn_ref, lam_ref, subln_ref, o_ref,
                         qs_scr, kpad_scr, vpad_scr, m_scr, l_scr, acc_scr,
                         *, rows, past, lam_init, new_mask_needed):
    kb = pl.program_id(1)

    @pl.when(kb == 0)
    def _():
        _flash_init(m_scr, l_scr, acc_scr)
        _stack_q_halves(q_ref[...], qs_scr, rows)
        kpad_scr[...] = jnp.zeros(kpad_scr.shape, BF16)
        vpad_scr[...] = jnp.zeros(vpad_scr.shape, BF16)
        kpad_scr[0:rows, :] = kn_ref[...]
        vpad_scr[0:rows, :] = vn_ref[...]
        c = lax.broadcasted_iota(jnp.int32, (2 * rows, LANES), 1)
        mask = c < rows
        if new_mask_needed:
            mask = mask & _chunk_mask(rows, LANES, past, past)
        _flash_step(qs_scr, lambda sl: kpad_scr[:, sl], lambda sl: vpad_scr[:, sl],
                    m_scr, l_scr, acc_scr, mask)

    _flash_step(qs_scr, lambda sl: ck_ref[:, sl].astype(BF16), lambda sl: cv_ref[:, sl].astype(BF16),
                m_scr, l_scr, acc_scr, None)

    @pl.when(kb == pl.num_programs(1) - 1)
    def _():
        lam = _diff_lambda(lam_ref, lam_init)
        o_ref[...] = _flash_finish(l_scr, acc_scr, lam, subln_ref[...], lam_init, rows)


def _flash_sample(q, cache_k, cache_v, k_new, v_new, lam_p, subln, lam_init, rows, tk):
    bs, past, d = cache_k.shape
    tk = min(tk, past)
    pos = past + np.arange(rows)
    new_mask_needed = not bool(np.all((pos[None, :] // CHUNK) <= (pos[:, None] // CHUNK)))
    row_spec = pl.BlockSpec((rows, d), lambda b, kb: (b, 0))
    cache_spec = pl.BlockSpec((None, tk, d), lambda b, kb: (b, kb, 0))
    vmem = 4 * tk * d * 4 + 4 * tk * d * 2 + (12 << 20)
    return pl.pallas_call(
        functools.partial(_flash_sample_kernel, rows=rows, past=past, lam_init=lam_init,
                          new_mask_needed=new_mask_needed),
        grid=(bs, past // tk),
        in_specs=[row_spec, cache_spec, cache_spec, row_spec, row_spec,
                  _const_spec(lam_p.shape), _const_spec((1, LANES))],
        out_specs=row_spec,
        out_shape=jax.ShapeDtypeStruct(q.shape, BF16),
        scratch_shapes=[pltpu.VMEM((2 * rows, d), BF16),
                        pltpu.VMEM((LANES, d), BF16), pltpu.VMEM((LANES, d), BF16),
                        pltpu.VMEM((DIFF_HEADS, 2 * rows, 1), F32),
                        pltpu.VMEM((DIFF_HEADS, 2 * rows, 1), F32),
                        pltpu.VMEM((2 * rows, d), F32)],
        compiler_params=_params(2, vmem),
        name="diff_flash_sample",
    )(q, cache_k, cache_v, k_new, v_new, lam_p, subln.reshape(1, LANES))


def _ret_log_gamma(h):
    return float(np.log(np.float32(1.0) - np.float32(2.0) ** np.float32(-5.0 - h)))


def _rotate_pairs(x):
    n = x.shape[-1]
    lane = lax.broadcasted_iota(jnp.int32, x.shape, 1)
    return jnp.where((lane & 1) == 0, -pltpu.roll(x, n - 1, 1), pltpu.roll(x, 1, 1))


def _ret_project(h, win_ref, cos, sin, d, dk):
    cos4 = jnp.concatenate([cos] * (d // dk), axis=1)
    sin4 = jnp.concatenate([sin] * (d // dk), axis=1)
    q = _dot(h, win_ref[:, 0:d])
    q = q * cos4 + _rotate_pairs(q) * sin4
    k = _dot(h, win_ref[:, d:2 * d])
    k = (k * cos4 + _rotate_pairs(k) * sin4) * (dk ** -0.5)
    v = _dot(h, win_ref[:, 2 * d:4 * d])
    return q, k, v


def _ret_gate(h, win_ref, d):
    return _silu(_dot(h, win_ref[:, 4 * d:6 * d]))


def _ret_decay(lg, rows, same_seq=None):
    t = lax.broadcasted_iota(jnp.int32, (rows, rows), 0)
    s = lax.broadcasted_iota(jnp.int32, (rows, rows), 1)
    ok = t >= s
    if same_seq is not None:
        ok = ok & ((t // same_seq) == (s // same_seq))
    diff = jnp.maximum(t - s, 0).astype(F32)
    return jnp.where(ok, jnp.exp(lg * diff), 0.0)


def _ret_prompt_kernel(x_ref, mod_ref, g_ref, win_ref, cos_ref, sin_ref, wout_ref, y_ref, st_ref,
                       state_scr, *, heads):
    i = pl.program_id(1)
    rows, d = x_ref.shape
    dk = d // heads
    dv = 2 * dk

    @pl.when(i == 0)
    def _():
        state_scr[...] = jnp.zeros(state_scr.shape, F32)

    x = x_ref[...]
    h = _mod_in(x, g_ref[0:1, :], mod_ref, 0, 1).astype(BF16)
    q, k, v = _ret_project(h, win_ref, cos_ref[...], sin_ref[...], d, dk)
    sg = _ret_gate(h, win_ref, d)
    t = lax.broadcasted_iota(jnp.int32, (rows, 1), 0).astype(F32)
    gated = []
    for hh in range(heads):
        lg = _ret_log_gamma(hh)
        qh = q[:, hh * dk:(hh + 1) * dk].astype(BF16)
        kh = k[:, hh * dk:(hh + 1) * dk]
        vh = v[:, hh * dv:(hh + 1) * dv].astype(BF16)
        state = state_scr[hh]
        scores = _dot_nt(qh, kh.astype(BF16)) * _ret_decay(lg, rows)
        o = _dot(scores.astype(BF16), vh) + _dot(qh, state.astype(BF16)) * jnp.exp(lg * (t + 1.0))
        k_dec = (kh * jnp.exp(lg * (rows - 1.0 - t))).astype(BF16)
        state_scr[hh] = math.exp(lg * rows) * state + _dot_tn(k_dec, vh)
        gated.append((_rms_rows(o) * sg[:, hh * dv:(hh + 1) * dv]).astype(BF16))
    oc = jnp.concatenate(gated, axis=1)
    y_ref[...] = _resid_out(x, _dot(oc, wout_ref[...]), g_ref[1:2, :], mod_ref[2])

    @pl.when(i == pl.num_programs(1) - 1)
    def _():
        st_ref[...] = state_scr[...]


def _xpos_tables(pos, dk):
    inv = 1.0 / (10000.0 ** jnp.linspace(0.0, 1.0, dk // 2, dtype=F32))
    ang = pos.astype(F32)[:, None] * jnp.repeat(inv, 2)[None, :]
    return jnp.cos(ang), jnp.sin(ang)


def _ret_prompt(x, mod, gains, w_in, w_out, tm):
    b, s, d = x.shape
    tm = min(tm, s)
    heads = RET_HEADS
    dk = d // heads
    dv = 2 * dk
    cos, sin = _xpos_tables(jnp.arange(s), dk)
    x_spec, mod_spec = _row_specs(x, mod, tm)
    tab_spec = pl.BlockSpec((tm, dk), lambda bb, i: (i, 0))
    st_spec = pl.BlockSpec((None, heads, dk, dv), lambda bb, i: (bb, 0, 0, 0))
    vmem = _nbytes(w_in, w_out) + 3 * heads * dk * dv * 4 + 40 * tm * d * 4 + (8 << 20)
    return pl.pallas_call(
        functools.partial(_ret_prompt_kernel, heads=heads),
        grid=(b, s // tm),
        in_specs=[x_spec, mod_spec, _const_spec(gains.shape), _const_spec(w_in.shape),
                  tab_spec, tab_spec, _const_spec(w_out.shape)],
        out_specs=[x_spec, st_spec],
        out_shape=[jax.ShapeDtypeStruct(x.shape, F32),
                   jax.ShapeDtypeStruct((b, heads, dk, dv), F32)],
        scratch_shapes=[pltpu.VMEM((heads, dk, dv), F32)],
        compiler_params=_params(2, vmem),
        name="ret_prompt",
    )(x, mod, gains, w_in, cos, sin, w_out)


def _ret_sample_kernel(x_ref, mod_ref, g_ref, win_ref, cos_ref, sin_ref, st_in_ref, wout_ref,
                       y_ref, st_out_ref, q_scr, k_scr, v_scr, o_scr, *, heads, rows):
    b = pl.program_id(0)
    total, d = x_ref.shape
    dk = d // heads
    dv = 2 * dk

    @pl.when(b == 0)
    def _():
        h = _mod_in(x_ref[...], g_ref[0:1, :], mod_ref, 0, 1).astype(BF16)
        q, k, v = _ret_project(h, win_ref, cos_ref[...], sin_ref[...], d, dk)
        q_scr[...] = q.astype(BF16)
        k_scr[...] = k
        v_scr[...] = v.astype(BF16)
        for hh in range(heads):
            scores = (_dot_nt(q[:, hh * dk:(hh + 1) * dk].astype(BF16),
                              k[:, hh * dk:(hh + 1) * dk].astype(BF16))
                      * _ret_decay(_ret_log_gamma(hh), total, same_seq=rows))
            o_scr[:, hh * dv:(hh + 1) * dv] = _dot(scores.astype(BF16),
                                                   v[:, hh * dv:(hh + 1) * dv].astype(BF16))

    r0 = pl.multiple_of(b * rows, rows)
    t = lax.broadcasted_iota(jnp.int32, (rows, 1), 0).astype(F32)
    ta = lax.broadcasted_iota(jnp.int32, (total, 1), 0)
    mine = (ta >= r0) & (ta < r0 + rows)
    t_all = (ta - r0).astype(F32)
    for hh in range(heads):
        lg = _ret_log_gamma(hh)
        state = st_in_ref[hh]
        qh = q_scr[pl.ds(r0, rows), hh * dk:(hh + 1) * dk]
        cross = _dot(qh, state.astype(BF16)) * jnp.exp(lg * (t + 1.0))
        o_scr[pl.ds(r0, rows), hh * dv:(hh + 1) * dv] += cross
        k_dec = jnp.where(mine, k_scr[:, hh * dk:(hh + 1) * dk] * jnp.exp(lg * (rows - 1.0 - t_all)), 0.0)
        st_out_ref[hh] = (math.exp(lg * rows) * state
                          + _dot_tn(k_dec.astype(BF16), v_scr[:, hh * dv:(hh + 1) * dv]))

    @pl.when(b == pl.num_programs(0) - 1)
    def _():
        h = _mod_in(x_ref[...], g_ref[0:1, :], mod_ref, 0, 1).astype(BF16)
        gated = []
        for hh in range(heads):
            sl = slice(hh * dv, (hh + 1) * dv)
            sg = _silu(_dot(h, win_ref[:, 4 * d + hh * dv:4 * d + (hh + 1) * dv]))
            gated.append((_rms_rows(o_scr[:, sl]) * sg).astype(BF16))
        oc = jnp.concatenate(gated, axis=1)
        y_ref[...] = _resid_out(x_ref[...], _dot(oc, wout_ref[...]), g_ref[1:2, :], mod_ref[2])


def _ret_sample(x, mod, gains, w_in, w_out, state, rows, past):
    total, d = x.shape
    bs, heads, dk, dv = state.shape
    cos, sin = _xpos_tables(past + jnp.arange(rows), dk)
    cos = jnp.tile(cos, (bs, 1))
    sin = jnp.tile(sin, (bs, 1))
    st_spec = pl.BlockSpec((None, heads, dk, dv), lambda b: (b, 0, 0, 0))
    vmem = _nbytes(w_in, w_out, x, x, mod) + 4 * heads * dk * dv * 4 + 60 * total * d * 4 + (8 << 20)
    return pl.pallas_call(
        functools.partial(_ret_sample_kernel, heads=heads, rows=rows),
        grid=(bs,),
        in_specs=[_const_spec(x.shape), _const_spec(mod.shape), _const_spec(gains.shape),
                  _const_spec(w_in.shape), _const_spec(cos.shape), _const_spec(sin.shape),
                  st_spec, _const_spec(w_out.shape)],
        out_specs=[pl.BlockSpec(x.shape, lambda b: (0, 0)), st_spec],
        out_shape=[jax.ShapeDtypeStruct(x.shape, F32), jax.ShapeDtypeStruct(state.shape, F32)],
        scratch_shapes=[pltpu.VMEM((total, d), BF16), pltpu.VMEM((total, d), F32),
                        pltpu.VMEM((total, 2 * d), BF16), pltpu.VMEM((total, 2 * d), F32)],
        compiler_params=_params(1, vmem),
        name="ret_sample",
    )(x, mod, gains, w_in, cos, sin, state, w_out)


def _hgrn_lower(lb_ref, layer):
    lb = lb_ref[...]
    e = jnp.exp(lb - jnp.max(lb, axis=0, keepdims=True))
    p = e / jnp.sum(e, axis=0, keepdims=True)
    if layer == 0:
        return jnp.zeros_like(p[0:1, :])
    return jnp.sum(p[1:layer + 1, :], axis=0, keepdims=True)


def _block_cumsum(x, block):
    rows = x.shape[0]
    t = lax.broadcasted_iota(jnp.int32, (rows, rows), 0)
    s = lax.broadcasted_iota(jnp.int32, (rows, rows), 1)
    tri = jnp.where((s <= t) & ((t // block) == (s // block)), 1.0, 0.0).astype(BF16)
    hi = x.astype(BF16)
    lo = (x - hi.astype(F32)).astype(BF16)
    return _dot(tri, hi) + _dot(tri, lo)


def _block_last(x, block):
    rows = x.shape[0]
    parts = [jnp.broadcast_to(x[c * block + block - 1:c * block + block, :], (block, x.shape[1]))
             for c in range(rows // block)]
    return parts[0] if len(parts) == 1 else jnp.concatenate(parts, axis=0)


def _hgrn_project(h, win_ref, lower, d, block):
    q = _silu(_dot(h, win_ref[:, 0:d]))
    f = _dot(h, win_ref[:, d:2 * d])
    v = _dot(h, win_ref[:, 2 * d:3 * d])
    sg = _silu(_dot(h, win_ref[:, 3 * d:4 * d]))
    sig = jax.nn.sigmoid(f)
    forget = lower + (1.0 - lower) * sig
    k = (1.0 - lower) * (1.0 - sig)
    b = _block_cumsum(jnp.log(forget), block)
    b_last = _block_last(b, block)
    return q * jnp.exp(b), k * jnp.exp(-b), k * jnp.exp(b_last - b), v, sg, jnp.exp(b_last)


def _hgrn_prompt_kernel(x_ref, mod_ref, g_ref, win_ref, lb_ref, ng_ref, wout_ref, y_ref, st_ref,
                        state_scr, o_scr, *, layer):
    i = pl.program_id(1)
    rows, d = x_ref.shape
    heads = d // HG_WIDTH
    w = HG_WIDTH

    @pl.when(i == 0)
    def _():
        state_scr[...] = jnp.zeros(state_scr.shape, F32)

    x = x_ref[...]
    h = _mod_in(x, g_ref[0:1, :], mod_ref, 0, 1).astype(BF16)
    q_dec, k_inv, k_end, v, sg, e_last = _hgrn_project(h, win_ref, _hgrn_lower(lb_ref, layer), d, CHUNK)
    t = lax.broadcasted_iota(jnp.int32, (CHUNK, CHUNK), 0)
    s = lax.broadcasted_iota(jnp.int32, (CHUNK, CHUNK), 1)
    causal = t >= s
    for c in range(rows // CHUNK):
        rc = slice(c * CHUNK, (c + 1) * CHUNK)
        for hh in range(heads):
            ch = slice(hh * w, (hh + 1) * w)
            qd = q_dec[rc, ch].astype(BF16)
            vv = v[rc, ch].astype(BF16)
            state_t = state_scr[hh]
            scores = jnp.where(causal, _dot_nt(qd, k_inv[rc, ch].astype(BF16)), 0.0)
            o_scr[rc, ch] = _dot(scores.astype(BF16), vv) + _dot_nt(qd, state_t.astype(BF16))
            state_scr[hh] = (e_last[c * CHUNK:c * CHUNK + 1, ch] * state_t
                             + _dot_tn(vv, k_end[rc, ch].astype(BF16)))
    outs = []
    for hh in range(heads):
        ch = slice(hh * w, (hh + 1) * w)
        outs.append((_rms_rows(o_scr[:, ch]) * ng_ref[:, ch] * sg[:, ch]).astype(BF16))
    oc = jnp.concatenate(outs, axis=1)
    y_ref[...] = _resid_out(x, _dot(oc, wout_ref[...]), g_ref[1:2, :], mod_ref[2])

    @pl.when(i == pl.num_programs(1) - 1)
    def _():
        for hh in range(heads):
            st_ref[hh] = state_scr[hh].T


def _hgrn_prompt(x, mod, gains, w_in, norm_g, w_out, lower_bounds, layer, tm):
    b, s, d = x.shape
    tm = min(tm, s)
    heads = d // HG_WIDTH
    x_spec, mod_spec = _row_specs(x, mod, tm)
    st_spec = pl.BlockSpec((None, heads, HG_WIDTH, HG_WIDTH), lambda bb, i: (bb, 0, 0, 0))
    vmem = _nbytes(w_in, w_out) + 40 * tm * d * 4 + (8 << 20)
    return pl.pallas_call(
        functools.partial(_hgrn_prompt_kernel, layer=layer),
        grid=(b, s // tm),
        in_specs=[x_spec, mod_spec, _const_spec(gains.shape), _const_spec(w_in.shape),
                  _const_spec(lower_bounds.shape), _const_spec((1, d)), _const_spec(w_out.shape)],
        out_specs=[x_spec, st_spec],
        out_shape=[jax.ShapeDtypeStruct(x.shape, F32),
                   jax.ShapeDtypeStruct((b, heads, HG_WIDTH, HG_WIDTH), F32)],
        scratch_shapes=[pltpu.VMEM((heads, HG_WIDTH, HG_WIDTH), F32), pltpu.VMEM((tm, d), F32)],
        compiler_params=_params(2, vmem),
        name="hgrn_prompt",
    )(x, mod, gains, w_in, lower_bounds, norm_g.reshape(1, d), w_out)


def _hgrn_sample_kernel(x_ref, mod_ref, g_ref, win_ref, lb_ref, ng_ref, st_in_ref, wout_ref,
                        y_ref, st_out_ref, qd_scr, ke_scr, v_scr, sg_scr, el_scr, o_scr,
                        *, layer, rows):
    b = pl.program_id(0)
    total, d = x_ref.shape
    heads = d // HG_WIDTH
    w = HG_WIDTH

    @pl.when(b == 0)
    def _():
        h = _mod_in(x_ref[...], g_ref[0:1, :], mod_ref, 0, 1).astype(BF16)
        q_dec, k_inv, k_end, v, sg, e_last = _hgrn_project(h, win_ref, _hgrn_lower(lb_ref, layer), d, rows)
        qd_scr[...] = q_dec
        ke_scr[...] = k_end
        v_scr[...] = v
        sg_scr[...] = sg
        el_scr[...] = e_last
        t = lax.broadcasted_iota(jnp.int32, (total, total), 0)
        s = lax.broadcasted_iota(jnp.int32, (total, total), 1)
        ok = (t >= s) & ((t // rows) == (s // rows))
        for hh in range(heads):
            ch = slice(hh * w, (hh + 1) * w)
            scores = jnp.where(ok, _dot_nt(q_dec[:, ch].astype(BF16), k_inv[:, ch].astype(BF16)), 0.0)
            o_scr[:, ch] = _dot(scores.astype(BF16), v[:, ch].astype(BF16))

    r0 = pl.multiple_of(b * rows, rows)
    ta = lax.broadcasted_iota(jnp.int32, (total, 1), 0)
    mine = (ta >= r0) & (ta < r0 + rows)
    for hh in range(heads):
        ch = slice(hh * w, (hh + 1) * w)
        state = st_in_ref[hh]
        qd = qd_scr[pl.ds(r0, rows), ch].astype(BF16)
        o_scr[pl.ds(r0, rows), ch] += _dot(qd, state.astype(BF16))
        k_end = jnp.where(mine, ke_scr[:, ch], 0.0).astype(BF16)
        new_t = (el_scr[pl.ds(r0, 1), ch] * state.T
                 + _dot_tn(v_scr[:, ch].astype(BF16), k_end))
        st_out_ref[hh] = new_t.T

    @pl.when(b == pl.num_programs(0) - 1)
    def _():
        outs = []
        for hh in range(heads):
            ch = slice(hh * w, (hh + 1) * w)
            outs.append((_rms_rows(o_scr[:, ch]) * ng_ref[:, ch] * sg_scr[:, ch]).astype(BF16))
        oc = jnp.concatenate(outs, axis=1)
        y_ref[...] = _resid_out(x_ref[...], _dot(oc, wout_ref[...]), g_ref[1:2, :], mod_ref[2])


def _hgrn_sample(x, mod, gains, w_in, norm_g, w_out, lower_bounds, layer, state, rows):
    total, d = x.shape
    bs, heads, dk, dv = state.shape
    st_spec = pl.BlockSpec((None, heads, dk, dv), lambda b: (b, 0, 0, 0))
    scr = pltpu.VMEM((total, d), F32)
    vmem = _nbytes(w_in, w_out, x, x, mod) + 60 * total * d * 4 + (8 << 20)
    return pl.pallas_call(
        functools.partial(_hgrn_sample_kernel, layer=layer, rows=rows),
        grid=(bs,),
        in_specs=[_const_spec(x.shape), _const_spec(mod.shape), _const_spec(gains.shape),
                  _const_spec(w_in.shape), _const_spec(lower_bounds.shape), _const_spec((1, d)),
                  st_spec, _const_spec(w_out.shape)],
        out_specs=[pl.BlockSpec(x.shape, lambda b: (0, 0)), st_spec],
        out_shape=[jax.ShapeDtypeStruct(x.shape, F32), jax.ShapeDtypeStruct(state.shape, F32)],
        scratch_shapes=[scr, scr, scr, scr, scr, scr],
        compiler_params=_params(1, vmem),
        name="hgrn_sample",
    )(x, mod, gains, w_in, lower_bounds, norm_g.reshape(1, d), state, w_out)


def kernel(x_prompt, x_sample, cache_k_diff, cache_v_diff, state_retention, state_hgrn, c_prompt, c_sample, w_ada, b_ada, norm_gains, gmlp_w_in, gmlp_ln_g, gmlp_ln_b, gmlp_w_s, gmlp_b_s, gmlp_w_out, diff_w_in, diff_lambda, diff_subln, diff_w_out, ret_w_in, ret_w_out, hgrn_w_in, hgrn_norm, hgrn_w_out, hgrn_lower_bounds, ffn_w_in, ffn_w_out):
    bp, s, d = x_prompt.shape
    bs, ls, _ = x_sample.shape
    ms = bs * ls
    depth = w_ada.shape[0]
    past = cache_k_diff.shape[2]
    n_mix = 4

    m_all = _ada(jnp.concatenate([c_prompt, c_sample], axis=0), w_ada, b_ada)
    yp = x_prompt
    ys = x_sample.reshape(1, ms, d)
    outs = {name: [] for name in ("gv", "kp", "vp", "ks", "vs", "rp", "rs", "hp", "hs")}
    for i in range(depth):
        kind, j = i % n_mix, i // n_mix
        m = m_all[i].reshape(bp + bs, 6, d)
        mod_p = m[:bp].reshape(bp, 6, 1, d)
        mod_s = jnp.repeat(m[bp:], ls, axis=0).transpose(1, 0, 2).reshape(1, 6, ms, d)
        gains = norm_gains[i]
        if kind == 0:
            w_in, w_out = gmlp_w_in[j].astype(BF16), gmlp_w_out[j].astype(BF16)
            args = (gmlp_ln_g[j], gmlp_ln_b[j], gmlp_w_s[j], gmlp_b_s[j], w_out)
            yp = _gmlp(yp, mod_p, gains, w_in, *args, tm=2 * GMLP_CHUNK, t_chunk=GMLP_CHUNK, emit_v=False)
            ys, v_rows = _gmlp(ys, mod_s, gains, w_in, *args, tm=ms, t_chunk=ls, emit_v=True)
            outs["gv"].append(v_rows.reshape(bs, ls, -1))
        elif kind == 1:
            lam_init = 0.8 - 0.6 * math.exp(-0.3 * i)
            w_in, w_out = diff_w_in[j].astype(BF16), diff_w_out[j].astype(BF16)
            hk, hv = 2 * DIFF_HEADS, DIFF_HEADS
            q, k, v, k16, v16 = _qkv(yp, mod_p, gains, w_in, tm=512)
            outs["kp"].append(k.reshape(bp, s, hk, d // hk))
            outs["vp"].append(v.reshape(bp, s, hv, d // hv))
            yp = _flash_prompt(q, k16, v16, yp, mod_p, gains, diff_lambda[j], diff_subln[j], w_out,
                               lam_init, tq=512)
            q, k, v, k16, v16 = _qkv(ys, mod_s, gains, w_in, tm=512)
            outs["ks"].append(k.reshape(bs, ls, hk, d // hk))
            outs["vs"].append(v.reshape(bs, ls, hv, d // hv))
            oc = _flash_sample(q[0], cache_k_diff[j].reshape(bs, past, d), cache_v_diff[j].reshape(bs, past, d),
                               k16[0], v16[0], diff_lambda[j], diff_subln[j], lam_init, rows=ls, tk=1024)
            ys = _outproj(oc[None], ys, mod_s, gains, w_out, tm=512)
        elif kind == 2:
            w_in, w_out = ret_w_in[j].astype(BF16), ret_w_out[j].astype(BF16)
            yp, st = _ret_prompt(yp, mod_p, gains, w_in, w_out, tm=256)
            outs["rp"].append(st)
            y2, st = _ret_sample(ys[0], mod_s[0, :3], gains, w_in, w_out, state_retention[j], rows=ls, past=past)
            ys = y2[None]
            outs["rs"].append(st)
        else:
            w_in, w_out = hgrn_w_in[j].astype(BF16), hgrn_w_out[j].astype(BF16)
            yp, st = _hgrn_prompt(yp, mod_p, gains, w_in, hgrn_norm[j], w_out, hgrn_lower_bounds, i, tm=256)
            outs["hp"].append(st)
            y2, st = _hgrn_sample(ys[0], mod_s[0, :3], gains, w_in, hgrn_norm[j], w_out, hgrn_lower_bounds, i,
                                  state_hgrn[j], rows=ls)
            ys = y2[None]
            outs["hs"].append(st)
        f_in, f_out = ffn_w_in[i].astype(BF16), ffn_w_out[i].astype(BF16)
        yp = _ffn(yp, mod_p, gains, f_in, f_out, tm=512)
        ys = _ffn(ys, mod_s, gains, f_in, f_out, tm=512)

    return (yp, ys.reshape(bs, ls, d), jnp.stack(outs["gv"]), jnp.stack(outs["kp"]), jnp.stack(outs["vp"]),
            jnp.stack(outs["ks"]), jnp.stack(outs["vs"]), jnp.stack(outs["rp"]), jnp.stack(outs["rs"]),
            jnp.stack(outs["hp"]), jnp.stack(outs["hs"]))
```

```python
import functools
import math

import numpy as np
import jax
import jax.numpy as jnp
from jax import lax
from jax.experimental import pallas as pl
from jax.experimental.pallas import tpu as pltpu

F32 = jnp.float32
BF16 = jnp.bfloat16
EPS = 1e-6
NEG_INF = -1e30

CHUNK = 64
GMLP_CHUNK = 128
GMLP_GROUPS = 8
DIFF_HEADS = 8
RET_HEADS = 4
HG_WIDTH = 128
LANES = 128
MXU_WIDTH = 256
VMEM_CAP = 60 << 20

_NT = (((1,), (1,)), ((), ()))
_TN = (((0,), (0,)), ((), ()))


def _dot(a, b):
    return jnp.dot(a, b, preferred_element_type=F32)


def _dot_nt(a, b):
    return lax.dot_general(a, b, _NT, preferred_element_type=F32)


def _dot_tn(a, b):
    return lax.dot_general(a, b, _TN, preferred_element_type=F32)


def _silu(x):
    return x * jax.nn.sigmoid(x)


def _gelu(x):
    return 0.5 * x * (1.0 + lax.erf(x * (2.0 ** -0.5)))


def _rms_rows(x):
    return x * lax.rsqrt(jnp.mean(x * x, axis=-1, keepdims=True) + EPS)


def _mod_in(x, g, mod_ref, k_shift, k_scale):
    return _rms_rows(x) * g * (1.0 + mod_ref[k_scale]) + mod_ref[k_shift]


def _resid_out(x, o, g, gate):
    return x + gate * (_rms_rows(o) * g)


def _params(n_grid, vmem_bytes):
    return pltpu.CompilerParams(
        dimension_semantics=("arbitrary",) * n_grid,
        vmem_limit_bytes=int(min(max(vmem_bytes, 32 << 20), VMEM_CAP)))


def _const_spec(shape):
    nd = len(shape)
    return pl.BlockSpec(shape, lambda *_: (0,) * nd, pipeline_mode=pl.Buffered(1))


def _row_specs(x, mod, tm):
    _, _, d = x.shape
    r = mod.shape[2]
    x_spec = pl.BlockSpec((None, tm, d), lambda b, i, *_: (b, i, 0))
    if r == 1:
        mod_spec = pl.BlockSpec((None, 6, 1, d), lambda b, i, *_: (b, 0, 0, 0))
    else:
        mod_spec = pl.BlockSpec((None, 6, tm, d), lambda b, i, *_: (b, 0, i, 0))
    return x_spec, mod_spec


def _nbytes(*arrays):
    return sum(int(np.prod(a.shape)) * jnp.dtype(a.dtype).itemsize for a in arrays)


def _ada_kernel(c_ref, w_ref, b_ref, o_ref):
    a = _silu(c_ref[...]).astype(BF16)
    o_ref[...] = _dot(a, w_ref[...].astype(BF16)) + b_ref[...]


def _ada(c_all, w_ada, b_ada):
    depth, d, n = w_ada.shape
    rows = c_all.shape[0]
    tn = n // 4
    return pl.pallas_call(
        _ada_kernel,
        grid=(depth, n // tn),
        in_specs=[pl.BlockSpec((rows, d), lambda l, j: (0, 0)),
                  pl.BlockSpec((None, d, tn), lambda l, j: (l, 0, j)),
                  pl.BlockSpec((None, 1, tn), lambda l, j: (l, 0, j))],
        out_specs=pl.BlockSpec((None, rows, tn), lambda l, j: (l, 0, j)),
        out_shape=jax.ShapeDtypeStruct((depth, rows, n), F32),
        compiler_params=_params(2, 3 * d * tn * 4 + (8 << 20)),
        name="ada_mod",
    )(c_all, w_ada, b_ada.reshape(depth, 1, n))


def _ffn_kernel(x_ref, mod_ref, g_ref, win_ref, wout_ref, y_ref, *, hidden, chunks):
    x = x_ref[...]
    h = _mod_in(x, g_ref[2:3, :], mod_ref, 3, 4).astype(BF16)
    acc = None
    for c0, cw in chunks:
        gate = _dot(h, win_ref[:, c0:c0 + cw])
        up = _dot(h, win_ref[:, hidden + c0:hidden + c0 + cw])
        act = (_silu(gate) * up).astype(BF16)
        part = _dot(act, wout_ref[c0:c0 + cw, :])
        acc = part if acc is None else acc + part
    y_ref[...] = _resid_out(x, acc, g_ref[3:4, :], mod_ref[5])


def _split_chunks(total, width):
    out, c0 = [], 0
    while c0 < total:
        out.append((c0, min(width, total - c0)))
        c0 += width
    return tuple(out)


def _ffn(x, mod, gains, w_in, w_out, tm):
    b, s, d = x.shape
    hidden = w_out.shape[0]
    tm = min(tm, s)
    x_spec, mod_spec = _row_specs(x, mod, tm)
    vmem = _nbytes(w_in, w_out) + 6 * tm * d * 4 + 4 * tm * 1024 * 4 + (8 << 20)
    return pl.pallas_call(
        functools.partial(_ffn_kernel, hidden=hidden, chunks=_split_chunks(hidden, 4 * MXU_WIDTH)),
        grid=(b, s // tm),
        in_specs=[x_spec, mod_spec, _const_spec(gains.shape), _const_spec(w_in.shape),
                  _const_spec(w_out.shape)],
        out_specs=x_spec,
        out_shape=jax.ShapeDtypeStruct(x.shape, F32),
        compiler_params=_params(2, vmem),
        name="ffn",
    )(x, mod, gains, w_in, w_out)


def _gmlp_kernel(x_ref, mod_ref, g_ref, win_ref, lng_ref, lnb_ref, wbd_ref, brow_ref, wout_ref,
                 *out_and_scratch, half, groups, pair, emit_v):
    if emit_v:
        y_ref, vn_ref, v_scr = out_and_scratch
    else:
        y_ref, v_scr = out_and_scratch
    gw = half // groups
    cw = pair * gw
    nblk = half // cw
    x = x_ref[...]
    h = _mod_in(x, g_ref[0:1, :], mod_ref, 0, 1).astype(BF16)
    s1 = None
    s2 = None
    for j in range(nblk):
        v = _gelu(_dot(h, win_ref[:, half + j * cw:half + (j + 1) * cw]))
        v_scr[:, j * cw:(j + 1) * cw] = v
        a1 = jnp.sum(v, axis=-1, keepdims=True)
        a2 = jnp.sum(v * v, axis=-1, keepdims=True)
        s1 = a1 if s1 is None else s1 + a1
        s2 = a2 if s2 is None else s2 + a2
    mu = s1 * (1.0 / half)
    rstd = lax.rsqrt(s2 * (1.0 / half) - mu * mu + EPS)
    acc = None
    for j in range(nblk):
        cols = slice(j * cw, (j + 1) * cw)
        vn = (v_scr[:, cols] - mu) * rstd * lng_ref[:, cols] + lnb_ref[:, cols]
        if emit_v:
            vn_ref[:, cols] = vn
        vnb = vn.astype(BF16)
        u = _gelu(_dot(h, win_ref[:, cols]))
        mixed = []
        for gg in range(pair):
            g = j * pair + gg
            mixed.append(_dot(wbd_ref[g], vnb[:, gg * gw:(gg + 1) * gw]) + brow_ref[:, g:g + 1])
        out = (u * jnp.concatenate(mixed, axis=1)).astype(BF16)
        part = _dot(out, wout_ref[cols, :])
        acc = part if acc is None else acc + part
    y_ref[...] = _resid_out(x, acc, g_ref[1:2, :], mod_ref[2])


def _gmlp(x, mod, gains, w_in, ln_g, ln_b, w_s, b_s, w_out, tm, t_chunk, emit_v):
    b, s, d = x.shape
    half = w_out.shape[0]
    groups = w_s.shape[0]
    n_rep = tm // t_chunk
    wt = jnp.tril(w_s[:, :t_chunk, :t_chunk])
    eye = jnp.eye(n_rep, dtype=w_s.dtype)
    w_bd = (eye[None, :, None, :, None] * wt[:, None, :, None, :]).reshape(groups, tm, tm).astype(BF16)
    b_rows = jnp.tile(b_s[:, :t_chunk].T, (n_rep, 1))
    x_spec, mod_spec = _row_specs(x, mod, tm)
    out_shape = [jax.ShapeDtypeStruct(x.shape, F32)]
    out_specs = [x_spec]
    if emit_v:
        out_shape.append(jax.ShapeDtypeStruct((b, s, half), F32))
        out_specs.append(pl.BlockSpec((None, tm, half), lambda bb, i: (bb, i, 0)))
    vmem = (_nbytes(w_in, w_out, w_bd) + tm * half * 4 * (5 if emit_v else 1)
            + 6 * tm * d * 4 + (12 << 20))
    res = pl.pallas_call(
        functools.partial(_gmlp_kernel, half=half, groups=groups, pair=2, emit_v=emit_v),
        grid=(b, s // tm),
        in_specs=[x_spec, mod_spec, _const_spec(gains.shape), _const_spec(w_in.shape),
                  _const_spec((1, half)), _const_spec((1, half)), _const_spec(w_bd.shape),
                  _const_spec(b_rows.shape), _const_spec(w_out.shape)],
        out_specs=out_specs,
        out_shape=out_shape,
        scratch_shapes=[pltpu.VMEM((tm, half), F32)],
        compiler_params=_params(2, vmem),
        name="gmlp_v" if emit_v else "gmlp",
    )(x, mod, gains, w_in, ln_g.reshape(1, half), ln_b.reshape(1, half), w_bd, b_rows, w_out)
    return res if emit_v else res[0]


def _outproj_kernel(o_ref, x_ref, mod_ref, g_ref, w_ref, y_ref):
    y_ref[...] = _resid_out(x_ref[...], _dot(o_ref[...], w_ref[...]), g_ref[1:2, :], mod_ref[2])


def _outproj(o, x, mod, gains, w_out, tm):
    b, s, d = x.shape
    tm = min(tm, s)
    k = o.shape[-1]
    x_spec, mod_spec = _row_specs(x, mod, tm)
    return pl.pallas_call(
        _outproj_kernel,
        grid=(b, s // tm),
        in_specs=[pl.BlockSpec((None, tm, k), lambda bb, i: (bb, i, 0)), x_spec, mod_spec,
                  _const_spec(gains.shape), _const_spec(w_out.shape)],
        out_specs=x_spec,
        out_shape=jax.ShapeDtypeStruct(x.shape, F32),
        compiler_params=_params(2, _nbytes(w_out) + 8 * tm * d * 4 + 2 * tm * k * 2 + (8 << 20)),
        name="outproj",
    )(o, x, mod, gains, w_out)


def _qkv_kernel(x_ref, mod_ref, g_ref, w_ref, q_ref, k_ref, v_ref, k16_ref, v16_ref, *, scale, transposed):
    d = x_ref.shape[-1]
    h = _mod_in(x_ref[...], g_ref[0:1, :], mod_ref, 0, 1).astype(BF16)
    q = _dot(h, w_ref[:, :d]) * scale
    q_ref[...] = (q.T if transposed else q).astype(BF16)
    k = _dot(h, w_ref[:, d:2 * d])
    k_ref[...] = k
    k16_ref[...] = k.astype(BF16)
    v = _dot(h, w_ref[:, 2 * d:])
    v_ref[...] = v
    v16_ref[...] = (v.T if transposed else v).astype(BF16)


def _qkv(x, mod, gains, w_in, tm, scale, transposed):
    b, s, d = x.shape
    tm = min(tm, s)
    x_spec, mod_spec = _row_specs(x, mod, tm)
    sds = jax.ShapeDtypeStruct
    if transposed:
        t_spec = pl.BlockSpec((None, d, tm), lambda bb, i: (bb, 0, i))
        t_shape = sds((b, d, s), BF16)
    else:
        t_spec, t_shape = x_spec, sds(x.shape, BF16)
    return pl.pallas_call(
        functools.partial(_qkv_kernel, scale=scale, transposed=transposed),
        grid=(b, s // tm),
        in_specs=[x_spec, mod_spec, _const_spec(gains.shape), _const_spec(w_in.shape)],
        out_specs=[t_spec, x_spec, x_spec, x_spec, t_spec],
        out_shape=[t_shape, sds(x.shape, F32), sds(x.shape, F32), sds(x.shape, BF16), t_shape],
        compiler_params=_params(2, _nbytes(w_in) + 24 * tm * d * 4 + (8 << 20)),
        name="diff_qkv_t" if transposed else "diff_qkv",
    )(x, mod, gains, w_in)


def _diff_lambda(lam_ref, lam_init):
    lp = lam_ref[...]
    e1 = jnp.exp(jnp.sum(lp[0:1, :] * lp[1:2, :], axis=-1, keepdims=True))
    e2 = jnp.exp(jnp.sum(lp[2:3, :] * lp[3:4, :], axis=-1, keepdims=True))
    return e1 - e2 + lam_init


def _stack_q_halves(q, qs_scr, rows):
    lane = lax.broadcasted_iota(jnp.int32, q.shape, 1)
    first = (lane & (LANES - 1)) < (LANES // 2)
    zero = jnp.zeros_like(q)
    qs_scr[0:rows, :] = jnp.where(first, q, zero)
    qs_scr[rows:2 * rows, :] = jnp.where(first, zero, q)


def _flash_init(m_scr, l_scr, acc_scr):
    m_scr[...] = jnp.full(m_scr.shape, NEG_INF, F32)
    l_scr[...] = jnp.zeros(l_scr.shape, F32)
    acc_scr[...] = jnp.zeros(acc_scr.shape, F32)


def _flash_step(qs_scr, scores, v_blk, m_scr, l_scr, acc_scr, mask):
    for h in range(DIFF_HEADS):
        sl = slice(LANES * h, LANES * (h + 1))
        s = scores(h, qs_scr[:, sl])
        if mask is not None:
            s = jnp.where(mask, s, NEG_INF)
        m_old = m_scr[h]
        m_new = jnp.maximum(m_old, jnp.max(s, axis=1, keepdims=True))
        alpha = jnp.exp(m_old - m_new)
        p = jnp.exp(s - m_new)
        l_scr[h] = alpha * l_scr[h] + jnp.sum(p, axis=1, keepdims=True)
        acc_scr[:, sl] = alpha * acc_scr[:, sl] + _dot(p.astype(BF16), v_blk(h))
        m_scr[h] = m_new


def _flash_finish(l_scr, acc_scr, lam, subln, lam_init, rows):
    outs = []
    for h in range(DIFF_HEADS):
        sl = slice(LANES * h, LANES * (h + 1))
        l = l_scr[h]
        o = acc_scr[0:rows, sl] / l[0:rows] - lam * (acc_scr[rows:2 * rows, sl] / l[rows:2 * rows])
        outs.append((_rms_rows(o) * subln * (1.0 - lam_init)).astype(BF16))
    return jnp.concatenate(outs, axis=1)


def _chunk_mask(rows, cols, row_pos0, col_pos0):
    r = lax.broadcasted_iota(jnp.int32, (2 * rows, cols), 0)
    c = lax.broadcasted_iota(jnp.int32, (2 * rows, cols), 1)
    r = jnp.where(r >= rows, r - rows, r)
    return ((c + col_pos0) // CHUNK) <= ((r + row_pos0) // CHUNK)


def _flash_t_step(qz_scr, k_ref, vt_ref, m_scr, l_scr, acc_scr, mask):
    for h in range(DIFF_HEADS):
        rows = slice(LANES * h, LANES * (h + 1))
        k_pair = k_ref[:, rows]
        vt = vt_ref[rows, :]
        for c in range(2):
            r = 2 * h + c
            s = _dot(k_pair, qz_scr[c, rows, :])
            if mask is not None:
                s = jnp.where(mask, s, NEG_INF)
            m_old = m_scr[r:r + 1, :]
            m_new = jnp.maximum(m_old, jnp.max(s, axis=0, keepdims=True))
            alpha = jnp.exp2(m_old - m_new)
            p = jnp.exp2(s - m_new)
            l_scr[r:r + 1, :] = alpha * l_scr[r:r + 1, :] + jnp.sum(p, axis=0, keepdims=True)
            acc_scr[c, rows, :] = alpha * acc_scr[c, rows, :] + _dot(vt, p.astype(BF16))
            m_scr[r:r + 1, :] = m_new


def _flash_prompt_kernel(qt_ref, kt_ref, q_ref, k_ref, vt_ref, x_ref, mod_ref, g_ref, lam_ref,
                         subln_ref, woutt_ref, y_ref, qz_scr, m_scr, l_scr, acc_scr, *, lam_init):
    p = pl.program_id(1)
    qi = qt_ref[p]
    ki = kt_ref[p]
    tk = k_ref.shape[0]
    tq = q_ref.shape[1]

    @pl.when(ki == 0)
    def _():
        m_scr[...] = jnp.full(m_scr.shape, NEG_INF, F32)
        l_scr[...] = jnp.zeros(l_scr.shape, F32)
        acc_scr[...] = jnp.zeros(acc_scr.shape, F32)
        q = q_ref[...]
        feat = lax.broadcasted_iota(jnp.int32, q.shape, 0)
        first = (feat & (LANES - 1)) < (LANES // 2)
        zero = jnp.zeros_like(q)
        qz_scr[0] = jnp.where(first, q, zero)
        qz_scr[1] = jnp.where(first, zero, q)

    @pl.when(ki < qi)
    def _():
        _flash_t_step(qz_scr, k_ref, vt_ref, m_scr, l_scr, acc_scr, None)

    @pl.when(ki == qi)
    def _():
        key = lax.broadcasted_iota(jnp.int32, (tk, tq), 0)
        qry = lax.broadcasted_iota(jnp.int32, (tk, tq), 1)
        _flash_t_step(qz_scr, k_ref, vt_ref, m_scr, l_scr, acc_scr, (key // CHUNK) <= (qry // CHUNK))
        lam = _diff_lambda(lam_ref, lam_init)
        outs = []
        for h in range(DIFF_HEADS):
            rows = slice(LANES * h, LANES * (h + 1))
            o = (acc_scr[0, rows, :] / l_scr[2 * h:2 * h + 1, :]
                 - lam * (acc_scr[1, rows, :] / l_scr[2 * h + 1:2 * h + 2, :]))
            o = o * lax.rsqrt(jnp.mean(o * o, axis=0, keepdims=True) + EPS)
            outs.append((o * (subln_ref[...] * (1.0 - lam_init))).astype(BF16))
        out_t = _dot(woutt_ref[...], jnp.concatenate(outs, axis=0))
        y_ref[...] = _resid_out(x_ref[...], out_t.T, g_ref[1:2, :], mod_ref[2])


def _flash_prompt(q_t, k16, v_t, x, mod, gains, lam_p, subln, w_out, lam_init, tq):
    b, s, d = x.shape
    tq = min(tq, s)
    nq = s // tq
    pairs = [(qi, ki) for qi in range(nq) for ki in range(qi + 1)]
    qt = jnp.asarray([pq for pq, _ in pairs], jnp.int32)
    kt = jnp.asarray([pk for _, pk in pairs], jnp.int32)
    x_spec = pl.BlockSpec((None, tq, d), lambda bb, p, qt_, kt_: (bb, qt_[p], 0))
    q_spec = pl.BlockSpec((None, d, tq), lambda bb, p, qt_, kt_: (bb, 0, qt_[p]))
    k_spec = pl.BlockSpec((None, tq, d), lambda bb, p, qt_, kt_: (bb, kt_[p], 0))
    v_spec = pl.BlockSpec((None, d, tq), lambda bb, p, qt_, kt_: (bb, 0, kt_[p]))
    mod_spec = pl.BlockSpec((None, 6, 1, d), lambda bb, p, qt_, kt_: (bb, 0, 0, 0))
    w_out_t = w_out.T
    vmem = (_nbytes(w_out) + 6 * tq * d * 2 + 4 * tq * d * 4 + 2 * tq * d * 2 + 2 * tq * d * 4
            + 8 * tq * tq * 4 + (8 << 20))
    grid_spec = pltpu.PrefetchScalarGridSpec(
        num_scalar_prefetch=2,
        grid=(b, len(pairs)),
        in_specs=[q_spec, k_spec, v_spec, x_spec, mod_spec, _const_spec(gains.shape),
                  _const_spec(lam_p.shape), _const_spec((LANES, 1)), _const_spec(w_out_t.shape)],
        out_specs=x_spec,
        scratch_shapes=[pltpu.VMEM((2, d, tq), BF16),
                        pltpu.VMEM((2 * DIFF_HEADS, tq), F32),
                        pltpu.VMEM((2 * DIFF_HEADS, tq), F32),
                        pltpu.VMEM((2, d, tq), F32)])
    return pl.pallas_call(
        functools.partial(_flash_prompt_kernel, lam_init=lam_init),
        grid_spec=grid_spec,
        out_shape=jax.ShapeDtypeStruct(x.shape, F32),
        compiler_params=_params(2, vmem),
        name="diff_flash_prompt",
    )(qt, kt, q_t, k16, v_t, x, mod, gains, lam_p, subln.reshape(LANES, 1), w_out_t)


def _flash_sample_kernel(q_ref, ck_ref, cv_ref, kn_ref, vn_ref, lam_ref, subln_ref, o_ref,
                         qs_scr, kpad_scr, vpad_scr, m_scr, l_scr, acc_scr,
                         *, rows, past, lam_init, new_mask_needed):
    kb = pl.program_id(1)

    @pl.when(kb == 0)
    def _():
        _flash_init(m_scr, l_scr, acc_scr)
        _stack_q_halves(q_ref[...], qs_scr, rows)
        kpad_scr[...] = jnp.zeros(kpad_scr.shape, BF16)
        vpad_scr[...] = jnp.zeros(vpad_scr.shape, BF16)
        kpad_scr[0:rows, :] = kn_ref[...]
        vpad_scr[0:rows, :] = vn_ref[...]
        c = lax.broadcasted_iota(jnp.int32, (2 * rows, LANES), 1)
        mask = c < rows
        if new_mask_needed:
            mask = mask & _chunk_mask(rows, LANES, past, past)
        _flash_step(qs_scr,
                    lambda h, qp: _dot_nt(qp, kpad_scr[:, LANES * h:LANES * (h + 1)]),
                    lambda h: vpad_scr[:, LANES * h:LANES * (h + 1)],
                    m_scr, l_scr, acc_scr, mask)

    tk = ck_ref.shape[1]
    _flash_step(qs_scr,
                lambda h, qp: _dot(qp, ck_ref[LANES * h:LANES * (h + 1), :].astype(BF16)),
                lambda h: cv_ref[pl.ds(h, tk, stride=DIFF_HEADS), :].astype(BF16),
                m_scr, l_scr, acc_scr, None)

    @pl.when(kb == pl.num_programs(1) - 1)
    def _():
        lam = _diff_lambda(lam_ref, lam_init)
        o_ref[...] = _flash_finish(l_scr, acc_scr, lam, subln_ref[...], lam_init, rows)


def _flash_sample(q, cache_k_t, cache_v, k_new, v_new, lam_p, subln, lam_init, rows, tk):
    bs, d, past = cache_k_t.shape
    tk = min(tk, past)
    pos = past + np.arange(rows)
    new_mask_needed = not bool(np.all((pos[None, :] // CHUNK) <= (pos[:, None] // CHUNK)))
    row_spec = pl.BlockSpec((rows, d), lambda b, kb: (b, 0))
    k_spec = pl.BlockSpec((None, d, tk), lambda b, kb: (b, 0, kb))
    v_spec = pl.BlockSpec((None, tk * DIFF_HEADS, LANES), lambda b, kb: (b, kb, 0))
    vmem = 4 * tk * d * 4 + 4 * tk * d * 2 + (12 << 20)
    return pl.pallas_call(
        functools.partial(_flash_sample_kernel, rows=rows, past=past, lam_init=lam_init,
                          new_mask_needed=new_mask_needed),
        grid=(bs, past // tk),
        in_specs=[row_spec, k_spec, v_spec, row_spec, row_spec,
                  _const_spec(lam_p.shape), _const_spec((1, LANES))],
        out_specs=row_spec,
        out_shape=jax.ShapeDtypeStruct(q.shape, BF16),
        scratch_shapes=[pltpu.VMEM((2 * rows, d), BF16),
                        pltpu.VMEM((LANES, d), BF16), pltpu.VMEM((LANES, d), BF16),
                        pltpu.VMEM((DIFF_HEADS, 2 * rows, 1), F32),
                        pltpu.VMEM((DIFF_HEADS, 2 * rows, 1), F32),
                        pltpu.VMEM((2 * rows, d), F32)],
        compiler_params=_params(2, vmem),
        name="diff_flash_sample",
    )(q, cache_k_t, cache_v, k_new, v_new, lam_p, subln.reshape(1, LANES))


def _ret_log_gamma(h):
    return float(np.log(np.float32(1.0) - np.float32(2.0) ** np.float32(-5.0 - h)))


def _rotate_pairs(x):
    n = x.shape[-1]
    lane = lax.broadcasted_iota(jnp.int32, x.shape, 1)
    return jnp.where((lane & 1) == 0, -pltpu.roll(x, n - 1, 1), pltpu.roll(x, 1, 1))


def _ret_project(h, win_ref, cos, sin, d, dk):
    cos4 = jnp.concatenate([cos] * (d // dk), axis=1)
    sin4 = jnp.concatenate([sin] * (d // dk), axis=1)
    q = _dot(h, win_ref[:, 0:d])
    q = q * cos4 + _rotate_pairs(q) * sin4
    k = _dot(h, win_ref[:, d:2 * d])
    k = (k * cos4 + _rotate_pairs(k) * sin4) * (dk ** -0.5)
    v = _dot(h, win_ref[:, 2 * d:4 * d])
    return q, k, v


def _ret_gate(h, win_ref, d):
    return _silu(_dot(h, win_ref[:, 4 * d:6 * d]))


def _ret_decay(lg, rows, same_seq=None):
    t = lax.broadcasted_iota(jnp.int32, (rows, rows), 0)
    s = lax.broadcasted_iota(jnp.int32, (rows, rows), 1)
    ok = t >= s
    if same_seq is not None:
        ok = ok & ((t // same_seq) == (s // same_seq))
    diff = jnp.maximum(t - s, 0).astype(F32)
    return jnp.where(ok, jnp.exp(lg * diff), 0.0)


def _ret_prompt_kernel(x_ref, mod_ref, g_ref, win_ref, cos_ref, sin_ref, wout_ref, y_ref, st_ref,
                       state_scr, *, heads):
    i = pl.program_id(1)
    rows, d = x_ref.shape
    dk = d // heads
    dv = 2 * dk

    @pl.when(i == 0)
    def _():
        state_scr[...] = jnp.zeros(state_scr.shape, F32)

    x = x_ref[...]
    h = _mod_in(x, g_ref[0:1, :], mod_ref, 0, 1).astype(BF16)
    q, k, v = _ret_project(h, win_ref, cos_ref[...], sin_ref[...], d, dk)
    sg = _ret_gate(h, win_ref, d)
    t = lax.broadcasted_iota(jnp.int32, (rows, 1), 0).astype(F32)
    gated = []
    for hh in range(heads):
        lg = _ret_log_gamma(hh)
        qh = q[:, hh * dk:(hh + 1) * dk].astype(BF16)
        kh = k[:, hh * dk:(hh + 1) * dk]
        vh = v[:, hh * dv:(hh + 1) * dv].astype(BF16)
        state = state_scr[hh]
        scores = _dot_nt(qh, kh.astype(BF16)) * _ret_decay(lg, rows)
        o = _dot(scores.astype(BF16), vh) + _dot(qh, state.astype(BF16)) * jnp.exp(lg * (t + 1.0))
        k_dec = (kh * jnp.exp(lg * (rows - 1.0 - t))).astype(BF16)
        state_scr[hh] = math.exp(lg * rows) * state + _dot_tn(k_dec, vh)
        gated.append((_rms_rows(o) * sg[:, hh * dv:(hh + 1) * dv]).astype(BF16))
    oc = jnp.concatenate(gated, axis=1)
    y_ref[...] = _resid_out(x, _dot(oc, wout_ref[...]), g_ref[1:2, :], mod_ref[2])

    @pl.when(i == pl.num_programs(1) - 1)
    def _():
        st_ref[...] = state_scr[...]


def _xpos_tables(pos, dk):
    inv = 1.0 / (10000.0 ** jnp.linspace(0.0, 1.0, dk // 2, dtype=F32))
    ang = pos.astype(F32)[:, None] * jnp.repeat(inv, 2)[None, :]
    return jnp.cos(ang), jnp.sin(ang)


def _ret_prompt(x, mod, gains, w_in, w_out, tm):
    b, s, d = x.shape
    tm = min(tm, s)
    heads = RET_HEADS
    dk = d // heads
    dv = 2 * dk
    cos, sin = _xpos_tables(jnp.arange(s), dk)
    x_spec, mod_spec = _row_specs(x, mod, tm)
    tab_spec = pl.BlockSpec((tm, dk), lambda bb, i: (i, 0))
    st_spec = pl.BlockSpec((None, heads, dk, dv), lambda bb, i: (bb, 0, 0, 0))
    vmem = _nbytes(w_in, w_out) + 3 * heads * dk * dv * 4 + 40 * tm * d * 4 + (8 << 20)
    return pl.pallas_call(
        functools.partial(_ret_prompt_kernel, heads=heads),
        grid=(b, s // tm),
        in_specs=[x_spec, mod_spec, _const_spec(gains.shape), _const_spec(w_in.shape),
                  tab_spec, tab_spec, _const_spec(w_out.shape)],
        out_specs=[x_spec, st_spec],
        out_shape=[jax.ShapeDtypeStruct(x.shape, F32),
                   jax.ShapeDtypeStruct((b, heads, dk, dv), F32)],
        scratch_shapes=[pltpu.VMEM((heads, dk, dv), F32)],
        compiler_params=_params(2, vmem),
        name="ret_prompt",
    )(x, mod, gains, w_in, cos, sin, w_out)


def _ret_sample_kernel(x_ref, mod_ref, g_ref, win_ref, cos_ref, sin_ref, st_in_ref, wout_ref,
                       y_ref, st_out_ref, q_scr, k_scr, v_scr, o_scr, *, heads, rows):
    b = pl.program_id(0)
    total, d = x_ref.shape
    dk = d // heads
    dv = 2 * dk

    @pl.when(b == 0)
    def _():
        h = _mod_in(x_ref[...], g_ref[0:1, :], mod_ref, 0, 1).astype(BF16)
        q, k, v = _ret_project(h, win_ref, cos_ref[...], sin_ref[...], d, dk)
        q_scr[...] = q.astype(BF16)
        k_scr[...] = k
        v_scr[...] = v.astype(BF16)
        for hh in range(heads):
            scores = (_dot_nt(q[:, hh * dk:(hh + 1) * dk].astype(BF16),
                              k[:, hh * dk:(hh + 1) * dk].astype(BF16))
                      * _ret_decay(_ret_log_gamma(hh), total, same_seq=rows))
            o_scr[:, hh * dv:(hh + 1) * dv] = _dot(scores.astype(BF16),
                                                   v[:, hh * dv:(hh + 1) * dv].astype(BF16))

    r0 = pl.multiple_of(b * rows, rows)
    t = lax.broadcasted_iota(jnp.int32, (rows, 1), 0).astype(F32)
    ta = lax.broadcasted_iota(jnp.int32, (total, 1), 0)
    mine = (ta >= r0) & (ta < r0 + rows)
    t_all = (ta - r0).astype(F32)
    for hh in range(heads):
        lg = _ret_log_gamma(hh)
        state = st_in_ref[hh]
        qh = q_scr[pl.ds(r0, rows), hh * dk:(hh + 1) * dk]
        cross = _dot(qh, state.astype(BF16)) * jnp.exp(lg * (t + 1.0))
        o_scr[pl.ds(r0, rows), hh * dv:(hh + 1) * dv] += cross
        k_dec = jnp.where(mine, k_scr[:, hh * dk:(hh + 1) * dk] * jnp.exp(lg * (rows - 1.0 - t_all)), 0.0)
        st_out_ref[hh] = (math.exp(lg * rows) * state
                          + _dot_tn(k_dec.astype(BF16), v_scr[:, hh * dv:(hh + 1) * dv]))

    @pl.when(b == pl.num_programs(0) - 1)
    def _():
        h = _mod_in(x_ref[...], g_ref[0:1, :], mod_ref, 0, 1).astype(BF16)
        gated = []
        for hh in range(heads):
            sl = slice(hh * dv, (hh + 1) * dv)
            sg = _silu(_dot(h, win_ref[:, 4 * d + hh * dv:4 * d + (hh + 1) * dv]))
            gated.append((_rms_rows(o_scr[:, sl]) * sg).astype(BF16))
        oc = jnp.concatenate(gated, axis=1)
        y_ref[...] = _resid_out(x_ref[...], _dot(oc, wout_ref[...]), g_ref[1:2, :], mod_ref[2])


def _ret_sample(x, mod, gains, w_in, w_out, state, rows, past):
    total, d = x.shape
    bs, heads, dk, dv = state.shape
    cos, sin = _xpos_tables(past + jnp.arange(rows), dk)
    cos = jnp.tile(cos, (bs, 1))
    sin = jnp.tile(sin, (bs, 1))
    st_spec = pl.BlockSpec((None, heads, dk, dv), lambda b: (b, 0, 0, 0))
    vmem = _nbytes(w_in, w_out, x, x, mod) + 4 * heads * dk * dv * 4 + 60 * total * d * 4 + (8 << 20)
    return pl.pallas_call(
        functools.partial(_ret_sample_kernel, heads=heads, rows=rows),
        grid=(bs,),
        in_specs=[_const_spec(x.shape), _const_spec(mod.shape), _const_spec(gains.shape),
                  _const_spec(w_in.shape), _const_spec(cos.shape), _const_spec(sin.shape),
                  st_spec, _const_spec(w_out.shape)],
        out_specs=[pl.BlockSpec(x.shape, lambda b: (0, 0)), st_spec],
        out_shape=[jax.ShapeDtypeStruct(x.shape, F32), jax.ShapeDtypeStruct(state.shape, F32)],
        scratch_shapes=[pltpu.VMEM((total, d), BF16), pltpu.VMEM((total, d), F32),
                        pltpu.VMEM((total, 2 * d), BF16), pltpu.VMEM((total, 2 * d), F32)],
        compiler_params=_params(1, vmem),
        name="ret_sample",
    )(x, mod, gains, w_in, cos, sin, state, w_out)


def _hgrn_lower(lb_ref, layer):
    lb = lb_ref[...]
    e = jnp.exp(lb - jnp.max(lb, axis=0, keepdims=True))
    p = e / jnp.sum(e, axis=0, keepdims=True)
    if layer == 0:
        return jnp.zeros_like(p[0:1, :])
    return jnp.sum(p[1:layer + 1, :], axis=0, keepdims=True)


def _block_cumsum(x, block):
    rows = x.shape[0]
    t = lax.broadcasted_iota(jnp.int32, (rows, rows), 0)
    s = lax.broadcasted_iota(jnp.int32, (rows, rows), 1)
    tri = jnp.where((s <= t) & ((t // block) == (s // block)), 1.0, 0.0).astype(BF16)
    hi = x.astype(BF16)
    lo = (x - hi.astype(F32)).astype(BF16)
    return _dot(tri, hi) + _dot(tri, lo)


def _block_last(x, block):
    rows = x.shape[0]
    parts = [jnp.broadcast_to(x[c * block + block - 1:c * block + block, :], (block, x.shape[1]))
             for c in range(rows // block)]
    return parts[0] if len(parts) == 1 else jnp.concatenate(parts, axis=0)


def _hgrn_project(h, win_ref, lower, d, block):
    q = _silu(_dot(h, win_ref[:, 0:d]))
    f = _dot(h, win_ref[:, d:2 * d])
    v = _dot(h, win_ref[:, 2 * d:3 * d])
    sg = _silu(_dot(h, win_ref[:, 3 * d:4 * d]))
    sig = jax.nn.sigmoid(f)
    forget = lower + (1.0 - lower) * sig
    k = (1.0 - lower) * (1.0 - sig)
    b = _block_cumsum(jnp.log(forget), block)
    b_last = _block_last(b, block)
    return q * jnp.exp(b), k * jnp.exp(-b), k * jnp.exp(b_last - b), v, sg, jnp.exp(b_last)


def _hgrn_prompt_kernel(x_ref, mod_ref, g_ref, win_ref, lb_ref, ng_ref, wout_ref, y_ref, st_ref,
                        state_scr, o_scr, *, layer):
    i = pl.program_id(1)
    rows, d = x_ref.shape
    heads = d // HG_WIDTH
    w = HG_WIDTH

    @pl.when(i == 0)
    def _():
        state_scr[...] = jnp.zeros(state_scr.shape, F32)

    x = x_ref[...]
    h = _mod_in(x, g_ref[0:1, :], mod_ref, 0, 1).astype(BF16)
    q_dec, k_inv, k_end, v, sg, e_last = _hgrn_project(h, win_ref, _hgrn_lower(lb_ref, layer), d, CHUNK)
    t = lax.broadcasted_iota(jnp.int32, (CHUNK, CHUNK), 0)
    s = lax.broadcasted_iota(jnp.int32, (CHUNK, CHUNK), 1)
    causal = t >= s
    for c in range(rows // CHUNK):
        rc = slice(c * CHUNK, (c + 1) * CHUNK)
        for hh in range(heads):
            ch = slice(hh * w, (hh + 1) * w)
            qd = q_dec[rc, ch].astype(BF16)
            vv = v[rc, ch].astype(BF16)
            state_t = state_scr[hh]
            scores = jnp.where(causal, _dot_nt(qd, k_inv[rc, ch].astype(BF16)), 0.0)
            o_scr[rc, ch] = _dot(scores.astype(BF16), vv) + _dot_nt(qd, state_t.astype(BF16))
            state_scr[hh] = (e_last[c * CHUNK:c * CHUNK + 1, ch] * state_t
                             + _dot_tn(vv, k_end[rc, ch].astype(BF16)))
    outs = []
    for hh in range(heads):
        ch = slice(hh * w, (hh + 1) * w)
        outs.append((_rms_rows(o_scr[:, ch]) * ng_ref[:, ch] * sg[:, ch]).astype(BF16))
    oc = jnp.concatenate(outs, axis=1)
    y_ref[...] = _resid_out(x, _dot(oc, wout_ref[...]), g_ref[1:2, :], mod_ref[2])

    @pl.when(i == pl.num_programs(1) - 1)
    def _():
        for hh in range(heads):
            st_ref[hh] = state_scr[hh].T


def _hgrn_prompt(x, mod, gains, w_in, norm_g, w_out, lower_bounds, layer, tm):
    b, s, d = x.shape
    tm = min(tm, s)
    heads = d // HG_WIDTH
    x_spec, mod_spec = _row_specs(x, mod, tm)
    st_spec = pl.BlockSpec((None, heads, HG_WIDTH, HG_WIDTH), lambda bb, i: (bb, 0, 0, 0))
    vmem = _nbytes(w_in, w_out) + 40 * tm * d * 4 + (8 << 20)
    return pl.pallas_call(
        functools.partial(_hgrn_prompt_kernel, layer=layer),
        grid=(b, s // tm),
        in_specs=[x_spec, mod_spec, _const_spec(gains.shape), _const_spec(w_in.shape),
                  _const_spec(lower_bounds.shape), _const_spec((1, d)), _const_spec(w_out.shape)],
        out_specs=[x_spec, st_spec],
        out_shape=[jax.ShapeDtypeStruct(x.shape, F32),
                   jax.ShapeDtypeStruct((b, heads, HG_WIDTH, HG_WIDTH), F32)],
        scratch_shapes=[pltpu.VMEM((heads, HG_WIDTH, HG_WIDTH), F32), pltpu.VMEM((tm, d), F32)],
        compiler_params=_params(2, vmem),
        name="hgrn_prompt",
    )(x, mod, gains, w_in, lower_bounds, norm_g.reshape(1, d), w_out)


def _hgrn_sample_kernel(x_ref, mod_ref, g_ref, win_ref, lb_ref, ng_ref, st_in_ref, wout_ref,
                        y_ref, st_out_ref, qd_scr, ke_scr, v_scr, sg_scr, el_scr, o_scr,
                        *, layer, rows):
    b = pl.program_id(0)
    total, d = x_ref.shape
    heads = d // HG_WIDTH
    w = HG_WIDTH

    @pl.when(b == 0)
    def _():
        h = _mod_in(x_ref[...], g_ref[0:1, :], mod_ref, 0, 1).astype(BF16)
        q_dec, k_inv, k_end, v, sg, e_last = _hgrn_project(h, win_ref, _hgrn_lower(lb_ref, layer), d, rows)
        qd_scr[...] = q_dec
        ke_scr[...] = k_end
        v_scr[...] = v
        sg_scr[...] = sg
        el_scr[...] = e_last
        t = lax.broadcasted_iota(jnp.int32, (total, total), 0)
        s = lax.broadcasted_iota(jnp.int32, (total, total), 1)
        ok = (t >= s) & ((t // rows) == (s // rows))
        for hh in range(heads):
            ch = slice(hh * w, (hh + 1) * w)
            scores = jnp.where(ok, _dot_nt(q_dec[:, ch].astype(BF16), k_inv[:, ch].astype(BF16)), 0.0)
            o_scr[:, ch] = _dot(scores.astype(BF16), v[:, ch].astype(BF16))

    r0 = pl.multiple_of(b * rows, rows)
    ta = lax.broadcasted_iota(jnp.int32, (total, 1), 0)
    mine = (ta >= r0) & (ta < r0 + rows)
    for hh in range(heads):
        ch = slice(hh * w, (hh + 1) * w)
        state = st_in_ref[hh]
        qd = qd_scr[pl.ds(r0, rows), ch].astype(BF16)
        o_scr[pl.ds(r0, rows), ch] += _dot(qd, state.astype(BF16))
        k_end = jnp.where(mine, ke_scr[:, ch], 0.0).astype(BF16)
        new_t = (el_scr[pl.ds(r0, 1), ch] * state.T
                 + _dot_tn(v_scr[:, ch].astype(BF16), k_end))
        st_out_ref[hh] = new_t.T

    @pl.when(b == pl.num_programs(0) - 1)
    def _():
        outs = []
        for hh in range(heads):
            ch = slice(hh * w, (hh + 1) * w)
            outs.append((_rms_rows(o_scr[:, ch]) * ng_ref[:, ch] * sg_scr[:, ch]).astype(BF16))
        oc = jnp.concatenate(outs, axis=1)
        y_ref[...] = _resid_out(x_ref[...], _dot(oc, wout_ref[...]), g_ref[1:2, :], mod_ref[2])


def _hgrn_sample(x, mod, gains, w_in, norm_g, w_out, lower_bounds, layer, state, rows):
    total, d = x.shape
    bs, heads, dk, dv = state.shape
    st_spec = pl.BlockSpec((None, heads, dk, dv), lambda b: (b, 0, 0, 0))
    scr = pltpu.VMEM((total, d), F32)
    vmem = _nbytes(w_in, w_out, x, x, mod) + 60 * total * d * 4 + (8 << 20)
    return pl.pallas_call(
        functools.partial(_hgrn_sample_kernel, layer=layer, rows=rows),
        grid=(bs,),
        in_specs=[_const_spec(x.shape), _const_spec(mod.shape), _const_spec(gains.shape),
                  _const_spec(w_in.shape), _const_spec(lower_bounds.shape), _const_spec((1, d)),
                  st_spec, _const_spec(w_out.shape)],
        out_specs=[pl.BlockSpec(x.shape, lambda b: (0, 0)), st_spec],
        out_shape=[jax.ShapeDtypeStruct(x.shape, F32), jax.ShapeDtypeStruct(state.shape, F32)],
        scratch_shapes=[scr, scr, scr, scr, scr, scr],
        compiler_params=_params(1, vmem),
        name="hgrn_sample",
    )(x, mod, gains, w_in, lower_bounds, norm_g.reshape(1, d), state, w_out)


def kernel(x_prompt, x_sample, cache_k_diff, cache_v_diff, state_retention, state_hgrn, c_prompt, c_sample, w_ada, b_ada, norm_gains, gmlp_w_in, gmlp_ln_g, gmlp_ln_b, gmlp_w_s, gmlp_b_s, gmlp_w_out, diff_w_in, diff_lambda, diff_subln, diff_w_out, ret_w_in, ret_w_out, hgrn_w_in, hgrn_norm, hgrn_w_out, hgrn_lower_bounds, ffn_w_in, ffn_w_out):
    bp, s, d = x_prompt.shape
    bs, ls, _ = x_sample.shape
    ms = bs * ls
    depth = w_ada.shape[0]
    past = cache_k_diff.shape[2]
    n_mix = 4

    m_all = _ada(jnp.concatenate([c_prompt, c_sample], axis=0), w_ada, b_ada)
    yp = x_prompt
    ys = x_sample.reshape(1, ms, d)
    outs = {name: [] for name in ("gv", "kp", "vp", "ks", "vs", "rp", "rs", "hp", "hs")}
    for i in range(depth):
        kind, j = i % n_mix, i // n_mix
        m = m_all[i].reshape(bp + bs, 6, d)
        mod_p = m[:bp].reshape(bp, 6, 1, d)
        mod_s = jnp.repeat(m[bp:], ls, axis=0).transpose(1, 0, 2).reshape(1, 6, ms, d)
        gains = norm_gains[i]
        if kind == 0:
            w_in, w_out = gmlp_w_in[j].astype(BF16), gmlp_w_out[j].astype(BF16)
            args = (gmlp_ln_g[j], gmlp_ln_b[j], gmlp_w_s[j], gmlp_b_s[j], w_out)
            yp = _gmlp(yp, mod_p, gains, w_in, *args, tm=2 * GMLP_CHUNK, t_chunk=GMLP_CHUNK, emit_v=False)
            ys, v_rows = _gmlp(ys, mod_s, gains, w_in, *args, tm=ms, t_chunk=ls, emit_v=True)
            outs["gv"].append(v_rows.reshape(bs, ls, -1))
        elif kind == 1:
            lam_init = 0.8 - 0.6 * math.exp(-0.3 * i)
            w_in, w_out = diff_w_in[j].astype(BF16), diff_w_out[j].astype(BF16)
            hk, hv = 2 * DIFF_HEADS, DIFF_HEADS
            scale = (d // hk) ** -0.5
            q_t, k, v, k16, v_t = _qkv(yp, mod_p, gains, w_in, tm=512, scale=scale * math.log2(math.e),
                                       transposed=True)
            outs["kp"].append(k.reshape(bp, s, hk, d // hk))
            outs["vp"].append(v.reshape(bp, s, hv, d // hv))
            yp = _flash_prompt(q_t, k16, v_t, yp, mod_p, gains, diff_lambda[j], diff_subln[j], w_out,
                               lam_init, tq=512)
            q, k, v, k16, v16 = _qkv(ys, mod_s, gains, w_in, tm=512, scale=scale, transposed=False)
            outs["ks"].append(k.reshape(bs, ls, hk, d // hk))
            outs["vs"].append(v.reshape(bs, ls, hv, d // hv))
            cache_k_t = cache_k_diff[j].transpose(0, 2, 3, 1).reshape(bs, d, past)
            cache_v = cache_v_diff[j].reshape(bs, past * hv, d // hv)
            oc = _flash_sample(q[0], cache_k_t, cache_v, k16[0], v16[0], diff_lambda[j], diff_subln[j],
                               lam_init, rows=ls, tk=1024)
            ys = _outproj(oc[None], ys, mod_s, gains, w_out, tm=512)
        elif kind == 2:
            w_in, w_out = ret_w_in[j].astype(BF16), ret_w_out[j].astype(BF16)
            yp, st = _ret_prompt(yp, mod_p, gains, w_in, w_out, tm=256)
            outs["rp"].append(st)
            y2, st = _ret_sample(ys[0], mod_s[0, :3], gains, w_in, w_out, state_retention[j], rows=ls, past=past)
            ys = y2[None]
            outs["rs"].append(st)
        else:
            w_in, w_out = hgrn_w_in[j].astype(BF16), hgrn_w_out[j].astype(BF16)
            yp, st = _hgrn_prompt(yp, mod_p, gains, w_in, hgrn_norm[j], w_out, hgrn_lower_bounds, i, tm=256)
            outs["hp"].append(st)
            y2, st = _hgrn_sample(ys[0], mod_s[0, :3], gains, w_in, hgrn_norm[j], w_out, hgrn_lower_bounds, i,
                                  state_hgrn[j], rows=ls)
            ys = y2[None]
            outs["hs"].append(st)
        f_in, f_out = ffn_w_in[i].astype(BF16), ffn_w_out[i].astype(BF16)
        yp = _ffn(yp, mod_p, gains, f_in, f_out, tm=512)
        ys = _ffn(ys, mod_s, gains, f_in, f_out, tm=512)

    return (yp, ys.reshape(bs, ls, d), jnp.stack(outs["gv"]), jnp.stack(outs["kp"]), jnp.stack(outs["vp"]),
            jnp.stack(outs["ks"]), jnp.stack(outs["vs"]), jnp.stack(outs["rp"]), jnp.stack(outs["rs"]),
            jnp.stack(outs["hp"]), jnp.stack(outs["hs"]))
```

```python
import functools
import math

import numpy as np
import jax
import jax.numpy as jnp
from jax import lax
from jax.experimental import pallas as pl
from jax.experimental.pallas import tpu as pltpu

F32 = jnp.float32
BF16 = jnp.bfloat16
EPS = 1e-6
NEG_INF = -1e30

CHUNK = 64
GMLP_CHUNK = 128
GMLP_GROUPS = 8
DIFF_HEADS = 8
RET_HEADS = 4
HG_WIDTH = 128
LANES = 128
MXU_WIDTH = 256
VMEM_CAP = 60 << 20
SOFTMAX_SAFE_LOG2 = 96.0

_NT = (((1,), (1,)), ((), ()))
_TN = (((0,), (0,)), ((), ()))


def _dot(a, b):
    return jnp.dot(a, b, preferred_element_type=F32)


def _dot_nt(a, b):
    return lax.dot_general(a, b, _NT, preferred_element_type=F32)


def _dot_tn(a, b):
    return lax.dot_general(a, b, _TN, preferred_element_type=F32)


def _silu(x):
    return x * jax.nn.sigmoid(x)


def _gelu(x):
    return 0.5 * x * (1.0 + lax.erf(x * (2.0 ** -0.5)))


def _rms_rows(x):
    return x * lax.rsqrt(jnp.mean(x * x, axis=-1, keepdims=True) + EPS)


def _mod_in(x, g, mod_ref, k_shift, k_scale):
    return _rms_rows(x) * g * (1.0 + mod_ref[k_scale]) + mod_ref[k_shift]


def _resid_out(x, o, g, gate):
    return x + gate * (_rms_rows(o) * g)


def _params(n_grid, vmem_bytes):
    return pltpu.CompilerParams(
        dimension_semantics=("arbitrary",) * n_grid,
        vmem_limit_bytes=int(min(max(vmem_bytes, 32 << 20), VMEM_CAP)))


def _const_spec(shape):
    nd = len(shape)
    return pl.BlockSpec(shape, lambda *_: (0,) * nd, pipeline_mode=pl.Buffered(1))


def _row_specs(x, mod, tm):
    _, _, d = x.shape
    r = mod.shape[2]
    x_spec = pl.BlockSpec((None, tm, d), lambda b, i, *_: (b, i, 0))
    if r == 1:
        mod_spec = pl.BlockSpec((None, 6, 1, d), lambda b, i, *_: (b, 0, 0, 0))
    else:
        mod_spec = pl.BlockSpec((None, 6, tm, d), lambda b, i, *_: (b, 0, i, 0))
    return x_spec, mod_spec


def _nbytes(*arrays):
    return sum(int(np.prod(a.shape)) * jnp.dtype(a.dtype).itemsize for a in arrays)


def _ada_kernel(c_ref, w_ref, b_ref, o_ref):
    a = _silu(c_ref[...]).astype(BF16)
    o_ref[...] = _dot(a, w_ref[...].astype(BF16)) + b_ref[...]


def _ada(c_all, w_ada, b_ada):
    depth, d, n = w_ada.shape
    rows = c_all.shape[0]
    tn = n // 4
    return pl.pallas_call(
        _ada_kernel,
        grid=(depth, n // tn),
        in_specs=[pl.BlockSpec((rows, d), lambda l, j: (0, 0)),
                  pl.BlockSpec((None, d, tn), lambda l, j: (l, 0, j)),
                  pl.BlockSpec((None, 1, tn), lambda l, j: (l, 0, j))],
        out_specs=pl.BlockSpec((None, rows, tn), lambda l, j: (l, 0, j)),
        out_shape=jax.ShapeDtypeStruct((depth, rows, n), F32),
        compiler_params=_params(2, 3 * d * tn * 4 + (8 << 20)),
        name="ada_mod",
    )(c_all, w_ada, b_ada.reshape(depth, 1, n))


def _ffn_kernel(x_ref, mod_ref, g_ref, win_ref, wout_ref, y_ref, *, hidden, chunks):
    x = x_ref[...]
    h = _mod_in(x, g_ref[2:3, :], mod_ref, 3, 4).astype(BF16)
    acc = None
    for c0, cw in chunks:
        gate = _dot(h, win_ref[:, c0:c0 + cw])
        up = _dot(h, win_ref[:, hidden + c0:hidden + c0 + cw])
        act = (_silu(gate) * up).astype(BF16)
        part = _dot(act, wout_ref[c0:c0 + cw, :])
        acc = part if acc is None else acc + part
    y_ref[...] = _resid_out(x, acc, g_ref[3:4, :], mod_ref[5])


def _split_chunks(total, width):
    out, c0 = [], 0
    while c0 < total:
        out.append((c0, min(width, total - c0)))
        c0 += width
    return tuple(out)


def _layer_spec(shape, layer):
    nd = len(shape) - 1
    return pl.BlockSpec((None,) + tuple(shape[1:]), lambda *_: (layer,) + (0,) * nd,
                        pipeline_mode=pl.Buffered(1))


def _ffn(x, mod, gains, w_in, w_out, layer, tm):
    b, s, d = x.shape
    hidden = w_out.shape[1]
    tm = min(tm, s)
    x_spec, mod_spec = _row_specs(x, mod, tm)
    vmem = _nbytes(w_in[0], w_out[0]) + 6 * tm * d * 4 + 4 * tm * 1024 * 4 + (8 << 20)
    return pl.pallas_call(
        functools.partial(_ffn_kernel, hidden=hidden, chunks=_split_chunks(hidden, 4 * MXU_WIDTH)),
        grid=(b, s // tm),
        in_specs=[x_spec, mod_spec, _const_spec(gains.shape), _layer_spec(w_in.shape, layer),
                  _layer_spec(w_out.shape, layer)],
        out_specs=x_spec,
        out_shape=jax.ShapeDtypeStruct(x.shape, F32),
        compiler_params=_params(2, vmem),
        name="ffn",
    )(x, mod, gains, w_in, w_out)


def _gmlp_kernel(x_ref, mod_ref, g_ref, win_ref, lng_ref, lnb_ref, wbd_ref, brow_ref, wout_ref,
                 *out_and_scratch, half, groups, pair, emit_v):
    if emit_v:
        y_ref, vn_ref, v_scr = out_and_scratch
    else:
        y_ref, v_scr = out_and_scratch
    gw = half // groups
    cw = pair * gw
    nblk = half // cw
    x = x_ref[...]
    h = _mod_in(x, g_ref[0:1, :], mod_ref, 0, 1).astype(BF16)
    s1 = None
    s2 = None
    for j in range(nblk):
        v = _gelu(_dot(h, win_ref[:, half + j * cw:half + (j + 1) * cw]))
        v_scr[:, j * cw:(j + 1) * cw] = v
        a1 = jnp.sum(v, axis=-1, keepdims=True)
        a2 = jnp.sum(v * v, axis=-1, keepdims=True)
        s1 = a1 if s1 is None else s1 + a1
        s2 = a2 if s2 is None else s2 + a2
    mu = s1 * (1.0 / half)
    rstd = lax.rsqrt(s2 * (1.0 / half) - mu * mu + EPS)
    acc = None
    for j in range(nblk):
        cols = slice(j * cw, (j + 1) * cw)
        vn = (v_scr[:, cols] - mu) * rstd * lng_ref[:, cols] + lnb_ref[:, cols]
        if emit_v:
            vn_ref[:, cols] = vn
        vnb = vn.astype(BF16)
        u = _gelu(_dot(h, win_ref[:, cols]))
        mixed = []
        for gg in range(pair):
            g = j * pair + gg
            mixed.append(_dot(wbd_ref[g], vnb[:, gg * gw:(gg + 1) * gw]) + brow_ref[:, g:g + 1])
        out = (u * jnp.concatenate(mixed, axis=1)).astype(BF16)
        part = _dot(out, wout_ref[cols, :])
        acc = part if acc is None else acc + part
    y_ref[...] = _resid_out(x, acc, g_ref[1:2, :], mod_ref[2])


def _gmlp(x, mod, gains, w_in, ln_g, ln_b, w_s, b_s, w_out, tm, t_chunk, emit_v):
    b, s, d = x.shape
    half = w_out.shape[0]
    groups = w_s.shape[0]
    n_rep = tm // t_chunk
    pos = np.arange(tm)
    expand = jnp.asarray(pos[:, None] % t_chunk == np.arange(t_chunk)[None, :], w_s.dtype)
    same = jnp.asarray(pos[:, None] // t_chunk == pos[None, :] // t_chunk)
    wt = jnp.tril(w_s[:, :t_chunk, :t_chunk])
    w_bd = jnp.einsum("rt,gts,cs->grc", expand, wt, expand, precision=lax.Precision.HIGHEST)
    w_bd = jnp.where(same[None], w_bd, 0.0).astype(BF16)
    b_rows = jnp.tile(b_s[:, :t_chunk].T, (n_rep, 1))
    x_spec, mod_spec = _row_specs(x, mod, tm)
    out_shape = [jax.ShapeDtypeStruct(x.shape, F32)]
    out_specs = [x_spec]
    if emit_v:
        out_shape.append(jax.ShapeDtypeStruct((b, s, half), F32))
        out_specs.append(pl.BlockSpec((None, tm, half), lambda bb, i: (bb, i, 0)))
    vmem = (_nbytes(w_in, w_out, w_bd) + tm * half * 4 * (5 if emit_v else 1)
            + 6 * tm * d * 4 + (12 << 20))
    res = pl.pallas_call(
        functools.partial(_gmlp_kernel, half=half, groups=groups, pair=2, emit_v=emit_v),
        grid=(b, s // tm),
        in_specs=[x_spec, mod_spec, _const_spec(gains.shape), _const_spec(w_in.shape),
                  _const_spec((1, half)), _const_spec((1, half)), _const_spec(w_bd.shape),
                  _const_spec(b_rows.shape), _const_spec(w_out.shape)],
        out_specs=out_specs,
        out_shape=out_shape,
        scratch_shapes=[pltpu.VMEM((tm, half), F32)],
        compiler_params=_params(2, vmem),
        name="gmlp_v" if emit_v else "gmlp",
    )(x, mod, gains, w_in, ln_g.reshape(1, half), ln_b.reshape(1, half), w_bd, b_rows, w_out)
    return res if emit_v else res[0]


def _outproj_kernel(o_ref, x_ref, mod_ref, g_ref, w_ref, y_ref):
    y_ref[...] = _resid_out(x_ref[...], _dot(o_ref[...], w_ref[...]), g_ref[1:2, :], mod_ref[2])


def _outproj(o, x, mod, gains, w_out, tm):
    b, s, d = x.shape
    tm = min(tm, s)
    k = o.shape[-1]
    x_spec, mod_spec = _row_specs(x, mod, tm)
    return pl.pallas_call(
        _outproj_kernel,
        grid=(b, s // tm),
        in_specs=[pl.BlockSpec((None, tm, k), lambda bb, i: (bb, i, 0)), x_spec, mod_spec,
                  _const_spec(gains.shape), _const_spec(w_out.shape)],
        out_specs=x_spec,
        out_shape=jax.ShapeDtypeStruct(x.shape, F32),
        compiler_params=_params(2, _nbytes(w_out) + 8 * tm * d * 4 + 2 * tm * k * 2 + (8 << 20)),
        name="outproj",
    )(o, x, mod, gains, w_out)


def _max_head_norm2(x_t, heads):
    d, rows = x_t.shape
    n2 = jnp.sum((x_t * x_t).reshape(heads, d // heads, rows), axis=1)
    return jnp.full((8, LANES), jnp.max(n2), F32)


def _qkv_kernel(x_ref, mod_ref, g_ref, w_ref, q_ref, k_ref, v_ref, k16_ref, v16_ref, *norm_refs,
                scale, transposed):
    d = x_ref.shape[-1]
    h = _mod_in(x_ref[...], g_ref[0:1, :], mod_ref, 0, 1).astype(BF16)
    q = _dot(h, w_ref[:, :d]) * scale
    k = _dot(h, w_ref[:, d:2 * d])
    k_ref[...] = k
    k16_ref[...] = k.astype(BF16)
    v = _dot(h, w_ref[:, 2 * d:])
    v_ref[...] = v
    if transposed:
        qn_ref, kn_ref = norm_refs
        q_t = q.T
        q_ref[...] = q_t.astype(BF16)
        v16_ref[...] = v.T.astype(BF16)
        qn_ref[...] = _max_head_norm2(q_t, 2 * DIFF_HEADS)
        kn_ref[...] = _max_head_norm2(k.T, 2 * DIFF_HEADS)
    else:
        q_ref[...] = q.astype(BF16)
        v16_ref[...] = v.astype(BF16)


def _qkv(x, mod, gains, w_in, tm, scale, transposed):
    b, s, d = x.shape
    tm = min(tm, s)
    x_spec, mod_spec = _row_specs(x, mod, tm)
    sds = jax.ShapeDtypeStruct
    out_specs = [x_spec, x_spec, x_spec, x_spec, x_spec]
    out_shape = [sds(x.shape, BF16), sds(x.shape, F32), sds(x.shape, F32), sds(x.shape, BF16),
                 sds(x.shape, BF16)]
    if transposed:
        t_spec = pl.BlockSpec((None, d, tm), lambda bb, i: (bb, 0, i))
        n_spec = pl.BlockSpec((None, None, 8, LANES), lambda bb, i: (bb, i, 0, 0))
        out_specs[0] = out_specs[4] = t_spec
        out_shape[0] = out_shape[4] = sds((b, d, s), BF16)
        out_specs += [n_spec, n_spec]
        out_shape += [sds((b, s // tm, 8, LANES), F32)] * 2
    return pl.pallas_call(
        functools.partial(_qkv_kernel, scale=scale, transposed=transposed),
        grid=(b, s // tm),
        in_specs=[x_spec, mod_spec, _const_spec(gains.shape), _const_spec(w_in.shape)],
        out_specs=out_specs,
        out_shape=out_shape,
        compiler_params=_params(2, _nbytes(w_in) + 24 * tm * d * 4 + (8 << 20)),
        name="diff_qkv_t" if transposed else "diff_qkv",
    )(x, mod, gains, w_in)


def _diff_lambda(lam_ref, lam_init):
    lp = lam_ref[...]
    e1 = jnp.exp(jnp.sum(lp[0:1, :] * lp[1:2, :], axis=-1, keepdims=True))
    e2 = jnp.exp(jnp.sum(lp[2:3, :] * lp[3:4, :], axis=-1, keepdims=True))
    return e1 - e2 + lam_init


def _stack_q_halves(q, qs_scr, rows):
    lane = lax.broadcasted_iota(jnp.int32, q.shape, 1)
    first = (lane & (LANES - 1)) < (LANES // 2)
    zero = jnp.zeros_like(q)
    qs_scr[0:rows, :] = jnp.where(first, q, zero)
    qs_scr[rows:2 * rows, :] = jnp.where(first, zero, q)


def _flash_init(m_scr, l_scr, acc_scr):
    m_scr[...] = jnp.full(m_scr.shape, NEG_INF, F32)
    l_scr[...] = jnp.zeros(l_scr.shape, F32)
    acc_scr[...] = jnp.zeros(acc_scr.shape, F32)


def _flash_step(qs_scr, scores, v_blk, m_scr, l_scr, acc_scr, mask):
    for h in range(DIFF_HEADS):
        sl = slice(LANES * h, LANES * (h + 1))
        s = scores(h, qs_scr[:, sl])
        if mask is not None:
            s = jnp.where(mask, s, NEG_INF)
        m_old = m_scr[h]
        m_new = jnp.maximum(m_old, jnp.max(s, axis=1, keepdims=True))
        alpha = jnp.exp(m_old - m_new)
        p = jnp.exp(s - m_new)
        l_scr[h] = alpha * l_scr[h] + jnp.sum(p, axis=1, keepdims=True)
        acc_scr[:, sl] = alpha * acc_scr[:, sl] + _dot(p.astype(BF16), v_blk(h))
        m_scr[h] = m_new


def _flash_finish(l_scr, acc_scr, lam, subln, lam_init, rows):
    outs = []
    for h in range(DIFF_HEADS):
        sl = slice(LANES * h, LANES * (h + 1))
        l = l_scr[h]
        o = acc_scr[0:rows, sl] / l[0:rows] - lam * (acc_scr[rows:2 * rows, sl] / l[rows:2 * rows])
        outs.append((_rms_rows(o) * subln * (1.0 - lam_init)).astype(BF16))
    return jnp.concatenate(outs, axis=1)


def _chunk_mask(rows, cols, row_pos0, col_pos0):
    r = lax.broadcasted_iota(jnp.int32, (2 * rows, cols), 0)
    c = lax.broadcasted_iota(jnp.int32, (2 * rows, cols), 1)
    r = jnp.where(r >= rows, r - rows, r)
    return ((c + col_pos0) // CHUNK) <= ((r + row_pos0) // CHUNK)


def _flash_t_step(qz_scr, k_ref, vt_ref, m_scr, l_scr, acc_scr, mask, bounded=False):
    heads = [(h, c) for h in range(DIFF_HEADS) for c in range(2)]
    rows = lambda h: slice(LANES * h, LANES * (h + 1))

    def scores(i):
        h, c = heads[i]
        return _dot(k_ref[:, rows(h)], qz_scr[c, rows(h), :])

    def softmax(i, s):
        r = 2 * heads[i][0] + heads[i][1]
        if mask is not None:
            s = jnp.where(mask, s, NEG_INF)
        m_old = m_scr[r:r + 1, :]
        block_ref = 0.0 if bounded else jnp.max(s, axis=0, keepdims=True)
        m_new = jnp.maximum(m_old, block_ref)
        alpha = jnp.exp2(m_old - m_new)
        p = jnp.exp2(s - m_new)
        l_scr[r:r + 1, :] = alpha * l_scr[r:r + 1, :] + jnp.sum(p, axis=0, keepdims=True)
        m_scr[r:r + 1, :] = m_new
        return alpha, p.astype(BF16)

    def accumulate(i, alpha, p):
        h, c = heads[i]
        acc_scr[c, rows(h), :] = alpha * acc_scr[c, rows(h), :] + _dot(vt_ref[rows(h), :], p)

    n = len(heads)
    ahead, behind = 2, 1
    pending_s = {i: scores(i) for i in range(ahead)}
    pending_p = {}
    for i in range(n):
        if i + ahead < n:
            pending_s[i + ahead] = scores(i + ahead)
        pending_p[i] = softmax(i, pending_s.pop(i))
        if i - behind >= 0:
            accumulate(i - behind, *pending_p.pop(i - behind))
    for i in sorted(pending_p):
        accumulate(i, *pending_p[i])


def _flash_prompt_kernel(qt_ref, kt_ref, safe_ref, q_ref, k_ref, vt_ref, x_ref, mod_ref, g_ref, lam_ref,
                         subln_ref, woutt_ref, y_ref, qz_scr, m_scr, l_scr, acc_scr, *, lam_init):
    p = pl.program_id(1)
    qi = qt_ref[p]
    ki = kt_ref[p]
    bounded = safe_ref[pl.program_id(0) * pl.num_programs(1) + p] != 0
    tk = k_ref.shape[0]
    tq = q_ref.shape[1]

    @pl.when(ki == 0)
    def _():
        m_scr[...] = jnp.full(m_scr.shape, NEG_INF, F32)
        l_scr[...] = jnp.zeros(l_scr.shape, F32)
        acc_scr[...] = jnp.zeros(acc_scr.shape, F32)
        q = q_ref[...]
        feat = lax.broadcasted_iota(jnp.int32, q.shape, 0)
        first = (feat & (LANES - 1)) < (LANES // 2)
        zero = jnp.zeros_like(q)
        qz_scr[0] = jnp.where(first, q, zero)
        qz_scr[1] = jnp.where(first, zero, q)

    @pl.when((ki < qi) & bounded)
    def _():
        _flash_t_step(qz_scr, k_ref, vt_ref, m_scr, l_scr, acc_scr, None, bounded=True)

    @pl.when((ki < qi) & jnp.logical_not(bounded))
    def _():
        _flash_t_step(qz_scr, k_ref, vt_ref, m_scr, l_scr, acc_scr, None)

    @pl.when(ki == qi)
    def _():
        key = lax.broadcasted_iota(jnp.int32, (tk, tq), 0)
        qry = lax.broadcasted_iota(jnp.int32, (tk, tq), 1)
        _flash_t_step(qz_scr, k_ref, vt_ref, m_scr, l_scr, acc_scr, (key // CHUNK) <= (qry // CHUNK))
        lam = _diff_lambda(lam_ref, lam_init)
        outs = []
        for h in range(DIFF_HEADS):
            rows = slice(LANES * h, LANES * (h + 1))
            o = (acc_scr[0, rows, :] / l_scr[2 * h:2 * h + 1, :]
                 - lam * (acc_scr[1, rows, :] / l_scr[2 * h + 1:2 * h + 2, :]))
            o = o * lax.rsqrt(jnp.mean(o * o, axis=0, keepdims=True) + EPS)
            outs.append((o * (subln_ref[...] * (1.0 - lam_init))).astype(BF16))
        out_t = _dot(woutt_ref[...], jnp.concatenate(outs, axis=0))
        y_ref[...] = _resid_out(x_ref[...], out_t.T, g_ref[1:2, :], mod_ref[2])


def _flash_prompt(q_t, k16, v_t, qn2, kn2, x, mod, gains, lam_p, subln, w_out, lam_init, tq):
    b, s, d = x.shape
    tq = min(tq, s)
    nq = s // tq
    assert qn2.shape == kn2.shape == (b, nq)
    pairs = [(qi, ki) for qi in range(nq) for ki in range(qi + 1)]
    qt = jnp.asarray([pq for pq, _ in pairs], jnp.int32)
    kt = jnp.asarray([pk for _, pk in pairs], jnp.int32)
    safe = (qn2[:, qt] * kn2[:, kt] <= SOFTMAX_SAFE_LOG2 ** 2).astype(jnp.int32).reshape(-1)
    x_spec = pl.BlockSpec((None, tq, d), lambda bb, p, qt_, kt_, safe_: (bb, qt_[p], 0))
    q_spec = pl.BlockSpec((None, d, tq), lambda bb, p, qt_, kt_, safe_: (bb, 0, qt_[p]))
    k_spec = pl.BlockSpec((None, tq, d), lambda bb, p, qt_, kt_, safe_: (bb, kt_[p], 0))
    v_spec = pl.BlockSpec((None, d, tq), lambda bb, p, qt_, kt_, safe_: (bb, 0, kt_[p]))
    mod_spec = pl.BlockSpec((None, 6, 1, d), lambda bb, p, qt_, kt_, safe_: (bb, 0, 0, 0))
    w_out_t = w_out.T
    vmem = (_nbytes(w_out) + 6 * tq * d * 2 + 4 * tq * d * 4 + 2 * tq * d * 2 + 2 * tq * d * 4
            + 8 * tq * tq * 4 + (8 << 20))
    grid_spec = pltpu.PrefetchScalarGridSpec(
        num_scalar_prefetch=3,
        grid=(b, len(pairs)),
        in_specs=[q_spec, k_spec, v_spec, x_spec, mod_spec, _const_spec(gains.shape),
                  _const_spec(lam_p.shape), _const_spec((LANES, 1)), _const_spec(w_out_t.shape)],
        out_specs=x_spec,
        scratch_shapes=[pltpu.VMEM((2, d, tq), BF16),
                        pltpu.VMEM((2 * DIFF_HEADS, tq), F32),
                        pltpu.VMEM((2 * DIFF_HEADS, tq), F32),
                        pltpu.VMEM((2, d, tq), F32)])
    return pl.pallas_call(
        functools.partial(_flash_prompt_kernel, lam_init=lam_init),
        grid_spec=grid_spec,
        out_shape=jax.ShapeDtypeStruct(x.shape, F32),
        compiler_params=_params(2, vmem),
        name="diff_flash_prompt",
    )(qt, kt, safe, q_t, k16, v_t, x, mod, gains, lam_p, subln.reshape(LANES, 1), w_out_t)


def _flash_sample_kernel(q_ref, ck_ref, cv_ref, kn_ref, vn_ref, lam_ref, subln_ref, o_ref,
                         qs_scr, kpad_scr, vpad_scr, m_scr, l_scr, acc_scr,
                         *, rows, past, lam_init, new_mask_needed):
    kb = pl.program_id(1)

    @pl.when(kb == 0)
    def _():
        _flash_init(m_scr, l_scr, acc_scr)
        _stack_q_halves(q_ref[...], qs_scr, rows)
        kpad_scr[...] = jnp.zeros(kpad_scr.shape, BF16)
        vpad_scr[...] = jnp.zeros(vpad_scr.shape, BF16)
        kpad_scr[0:rows, :] = kn_ref[...]
        vpad_scr[0:rows, :] = vn_ref[...]
        c = lax.broadcasted_iota(jnp.int32, (2 * rows, LANES), 1)
        mask = c < rows
        if new_mask_needed:
            mask = mask & _chunk_mask(rows, LANES, past, past)
        _flash_step(qs_scr,
                    lambda h, qp: _dot_nt(qp, kpad_scr[:, LANES * h:LANES * (h + 1)]),
                    lambda h: vpad_scr[:, LANES * h:LANES * (h + 1)],
                    m_scr, l_scr, acc_scr, mask)

    tk = ck_ref.shape[1]
    _flash_step(qs_scr,
                lambda h, qp: _dot(qp, ck_ref[LANES * h:LANES * (h + 1), :].astype(BF16)),
                lambda h: cv_ref[pl.ds(h, tk, stride=DIFF_HEADS), :].astype(BF16),
                m_scr, l_scr, acc_scr, None)

    @pl.when(kb == pl.num_programs(1) - 1)
    def _():
        lam = _diff_lambda(lam_ref, lam_init)
        o_ref[...] = _flash_finish(l_scr, acc_scr, lam, subln_ref[...], lam_init, rows)


def _flash_sample(q, cache_k_t, cache_v, k_new, v_new, lam_p, subln, lam_init, rows, tk):
    bs, d, past = cache_k_t.shape
    tk = min(tk, past)
    pos = past + np.arange(rows)
    new_mask_needed = not bool(np.all((pos[None, :] // CHUNK) <= (pos[:, None] // CHUNK)))
    row_spec = pl.BlockSpec((rows, d), lambda b, kb: (b, 0))
    k_spec = pl.BlockSpec((None, d, tk), lambda b, kb: (b, 0, kb))
    v_spec = pl.BlockSpec((None, tk * DIFF_HEADS, LANES), lambda b, kb: (b, kb, 0))
    vmem = 4 * tk * d * 4 + 4 * tk * d * 2 + (12 << 20)
    return pl.pallas_call(
        functools.partial(_flash_sample_kernel, rows=rows, past=past, lam_init=lam_init,
                          new_mask_needed=new_mask_needed),
        grid=(bs, past // tk),
        in_specs=[row_spec, k_spec, v_spec, row_spec, row_spec,
                  _const_spec(lam_p.shape), _const_spec((1, LANES))],
        out_specs=row_spec,
        out_shape=jax.ShapeDtypeStruct(q.shape, BF16),
        scratch_shapes=[pltpu.VMEM((2 * rows, d), BF16),
                        pltpu.VMEM((LANES, d), BF16), pltpu.VMEM((LANES, d), BF16),
                        pltpu.VMEM((DIFF_HEADS, 2 * rows, 1), F32),
                        pltpu.VMEM((DIFF_HEADS, 2 * rows, 1), F32),
                        pltpu.VMEM((2 * rows, d), F32)],
        compiler_params=_params(2, vmem),
        name="diff_flash_sample",
    )(q, cache_k_t, cache_v, k_new, v_new, lam_p, subln.reshape(1, LANES))


def _ret_log_gamma(h):
    return float(np.log(np.float32(1.0) - np.float32(2.0) ** np.float32(-5.0 - h)))


def _rotate_pairs(x):
    n = x.shape[-1]
    lane = lax.broadcasted_iota(jnp.int32, x.shape, 1)
    return jnp.where((lane & 1) == 0, -pltpu.roll(x, n - 1, 1), pltpu.roll(x, 1, 1))


def _ret_project(h, win_ref, cos, sin, d, dk):
    cos4 = jnp.concatenate([cos] * (d // dk), axis=1)
    sin4 = jnp.concatenate([sin] * (d // dk), axis=1)
    q = _dot(h, win_ref[:, 0:d])
    q = q * cos4 + _rotate_pairs(q) * sin4
    k = _dot(h, win_ref[:, d:2 * d])
    k = (k * cos4 + _rotate_pairs(k) * sin4) * (dk ** -0.5)
    v = _dot(h, win_ref[:, 2 * d:4 * d])
    return q, k, v


def _ret_gate(h, win_ref, d):
    return _silu(_dot(h, win_ref[:, 4 * d:6 * d]))


def _ret_decay(lg, rows, same_seq=None):
    t = lax.broadcasted_iota(jnp.int32, (rows, rows), 0)
    s = lax.broadcasted_iota(jnp.int32, (rows, rows), 1)
    ok = t >= s
    if same_seq is not None:
        ok = ok & ((t // same_seq) == (s // same_seq))
    diff = jnp.maximum(t - s, 0).astype(F32)
    return jnp.where(ok, jnp.exp(lg * diff), 0.0)


def _ret_prompt_kernel(x_ref, mod_ref, g_ref, win_ref, cos_ref, sin_ref, wout_ref, y_ref, st_ref,
                       state_scr, *, heads):
    i = pl.program_id(1)
    rows, d = x_ref.shape
    dk = d // heads
    dv = 2 * dk

    @pl.when(i == 0)
    def _():
        state_scr[...] = jnp.zeros(state_scr.shape, F32)

    x = x_ref[...]
    h = _mod_in(x, g_ref[0:1, :], mod_ref, 0, 1).astype(BF16)
    q, k, v = _ret_project(h, win_ref, cos_ref[...], sin_ref[...], d, dk)
    sg = _ret_gate(h, win_ref, d)
    t = lax.broadcasted_iota(jnp.int32, (rows, 1), 0).astype(F32)
    gated = []
    for hh in range(heads):
        lg = _ret_log_gamma(hh)
        qh = q[:, hh * dk:(hh + 1) * dk].astype(BF16)
        kh = k[:, hh * dk:(hh + 1) * dk]
        vh = v[:, hh * dv:(hh + 1) * dv].astype(BF16)
        state = state_scr[hh]
        scores = _dot_nt(qh, kh.astype(BF16)) * _ret_decay(lg, rows)
        o = _dot(scores.astype(BF16), vh) + _dot(qh, state.astype(BF16)) * jnp.exp(lg * (t + 1.0))
        k_dec = (kh * jnp.exp(lg * (rows - 1.0 - t))).astype(BF16)
        state_scr[hh] = math.exp(lg * rows) * state + _dot_tn(k_dec, vh)
        gated.append((_rms_rows(o) * sg[:, hh * dv:(hh + 1) * dv]).astype(BF16))
    oc = jnp.concatenate(gated, axis=1)
    y_ref[...] = _resid_out(x, _dot(oc, wout_ref[...]), g_ref[1:2, :], mod_ref[2])

    @pl.when(i == pl.num_programs(1) - 1)
    def _():
        st_ref[...] = state_scr[...]


def _xpos_tables(pos, dk):
    inv = 1.0 / (10000.0 ** jnp.linspace(0.0, 1.0, dk // 2, dtype=F32))
    ang = pos.astype(F32)[:, None] * jnp.repeat(inv, 2)[None, :]
    return jnp.cos(ang), jnp.sin(ang)


def _ret_prompt(x, mod, gains, w_in, w_out, tm):
    b, s, d = x.shape
    tm = min(tm, s)
    heads = RET_HEADS
    dk = d // heads
    dv = 2 * dk
    cos, sin = _xpos_tables(jnp.arange(s), dk)
    x_spec, mod_spec = _row_specs(x, mod, tm)
    tab_spec = pl.BlockSpec((tm, dk), lambda bb, i: (i, 0))
    st_spec = pl.BlockSpec((None, heads, dk, dv), lambda bb, i: (bb, 0, 0, 0))
    vmem = _nbytes(w_in, w_out) + 3 * heads * dk * dv * 4 + 40 * tm * d * 4 + (8 << 20)
    return pl.pallas_call(
        functools.partial(_ret_prompt_kernel, heads=heads),
        grid=(b, s // tm),
        in_specs=[x_spec, mod_spec, _const_spec(gains.shape), _const_spec(w_in.shape),
                  tab_spec, tab_spec, _const_spec(w_out.shape)],
        out_specs=[x_spec, st_spec],
        out_shape=[jax.ShapeDtypeStruct(x.shape, F32),
                   jax.ShapeDtypeStruct((b, heads, dk, dv), F32)],
        scratch_shapes=[pltpu.VMEM((heads, dk, dv), F32)],
        compiler_params=_params(2, vmem),
        name="ret_prompt",
    )(x, mod, gains, w_in, cos, sin, w_out)


def _ret_sample_kernel(x_ref, mod_ref, g_ref, win_ref, cos_ref, sin_ref, st_in_ref, wout_ref,
                       y_ref, st_out_ref, q_scr, k_scr, v_scr, o_scr, *, heads, rows):
    b = pl.program_id(0)
    total, d = x_ref.shape
    dk = d // heads
    dv = 2 * dk

    @pl.when(b == 0)
    def _():
        h = _mod_in(x_ref[...], g_ref[0:1, :], mod_ref, 0, 1).astype(BF16)
        q, k, v = _ret_project(h, win_ref, cos_ref[...], sin_ref[...], d, dk)
        q_scr[...] = q.astype(BF16)
        k_scr[...] = k
        v_scr[...] = v.astype(BF16)
        for hh in range(heads):
            scores = (_dot_nt(q[:, hh * dk:(hh + 1) * dk].astype(BF16),
                              k[:, hh * dk:(hh + 1) * dk].astype(BF16))
                      * _ret_decay(_ret_log_gamma(hh), total, same_seq=rows))
            o_scr[:, hh * dv:(hh + 1) * dv] = _dot(scores.astype(BF16),
                                                   v[:, hh * dv:(hh + 1) * dv].astype(BF16))

    r0 = pl.multiple_of(b * rows, rows)
    t = lax.broadcasted_iota(jnp.int32, (rows, 1), 0).astype(F32)
    ta = lax.broadcasted_iota(jnp.int32, (total, 1), 0)
    mine = (ta >= r0) & (ta < r0 + rows)
    t_all = (ta - r0).astype(F32)
    for hh in range(heads):
        lg = _ret_log_gamma(hh)
        state = st_in_ref[hh]
        qh = q_scr[pl.ds(r0, rows), hh * dk:(hh + 1) * dk]
        cross = _dot(qh, state.astype(BF16)) * jnp.exp(lg * (t + 1.0))
        o_scr[pl.ds(r0, rows), hh * dv:(hh + 1) * dv] += cross
        k_dec = jnp.where(mine, k_scr[:, hh * dk:(hh + 1) * dk] * jnp.exp(lg * (rows - 1.0 - t_all)), 0.0)
        st_out_ref[hh] = (math.exp(lg * rows) * state
                          + _dot_tn(k_dec.astype(BF16), v_scr[:, hh * dv:(hh + 1) * dv]))

    @pl.when(b == pl.num_programs(0) - 1)
    def _():
        h = _mod_in(x_ref[...], g_ref[0:1, :], mod_ref, 0, 1).astype(BF16)
        gated = []
        for hh in range(heads):
            sl = slice(hh * dv, (hh + 1) * dv)
            sg = _silu(_dot(h, win_ref[:, 4 * d + hh * dv:4 * d + (hh + 1) * dv]))
            gated.append((_rms_rows(o_scr[:, sl]) * sg).astype(BF16))
        oc = jnp.concatenate(gated, axis=1)
        y_ref[...] = _resid_out(x_ref[...], _dot(oc, wout_ref[...]), g_ref[1:2, :], mod_ref[2])


def _ret_sample(x, mod, gains, w_in, w_out, state, rows, past):
    total, d = x.shape
    bs, heads, dk, dv = state.shape
    cos, sin = _xpos_tables(past + jnp.arange(rows), dk)
    cos = jnp.tile(cos, (bs, 1))
    sin = jnp.tile(sin, (bs, 1))
    st_spec = pl.BlockSpec((None, heads, dk, dv), lambda b: (b, 0, 0, 0))
    vmem = _nbytes(w_in, w_out, x, x, mod) + 4 * heads * dk * dv * 4 + 60 * total * d * 4 + (8 << 20)
    return pl.pallas_call(
        functools.partial(_ret_sample_kernel, heads=heads, rows=rows),
        grid=(bs,),
        in_specs=[_const_spec(x.shape), _const_spec(mod.shape), _const_spec(gains.shape),
                  _const_spec(w_in.shape), _const_spec(cos.shape), _const_spec(sin.shape),
                  st_spec, _const_spec(w_out.shape)],
        out_specs=[pl.BlockSpec(x.shape, lambda b: (0, 0)), st_spec],
        out_shape=[jax.ShapeDtypeStruct(x.shape, F32), jax.ShapeDtypeStruct(state.shape, F32)],
        scratch_shapes=[pltpu.VMEM((total, d), BF16), pltpu.VMEM((total, d), F32),
                        pltpu.VMEM((total, 2 * d), BF16), pltpu.VMEM((total, 2 * d), F32)],
        compiler_params=_params(1, vmem),
        name="ret_sample",
    )(x, mod, gains, w_in, cos, sin, state, w_out)


def _hgrn_lower(lb_ref, layer):
    lb = lb_ref[...]
    e = jnp.exp(lb - jnp.max(lb, axis=0, keepdims=True))
    p = e / jnp.sum(e, axis=0, keepdims=True)
    if layer == 0:
        return jnp.zeros_like(p[0:1, :])
    return jnp.sum(p[1:layer + 1, :], axis=0, keepdims=True)


def _block_cumsum(x, block):
    rows = x.shape[0]
    t = lax.broadcasted_iota(jnp.int32, (rows, rows), 0)
    s = lax.broadcasted_iota(jnp.int32, (rows, rows), 1)
    tri = jnp.where((s <= t) & ((t // block) == (s // block)), 1.0, 0.0).astype(BF16)
    hi = x.astype(BF16)
    lo = (x - hi.astype(F32)).astype(BF16)
    return _dot(tri, hi) + _dot(tri, lo)


def _block_last(x, block):
    rows = x.shape[0]
    parts = [jnp.broadcast_to(x[c * block + block - 1:c * block + block, :], (block, x.shape[1]))
             for c in range(rows // block)]
    return parts[0] if len(parts) == 1 else jnp.concatenate(parts, axis=0)


def _hgrn_project(h, win_ref, lower, d, block):
    q = _silu(_dot(h, win_ref[:, 0:d]))
    f = _dot(h, win_ref[:, d:2 * d])
    v = _dot(h, win_ref[:, 2 * d:3 * d])
    sg = _silu(_dot(h, win_ref[:, 3 * d:4 * d]))
    sig = jax.nn.sigmoid(f)
    forget = lower + (1.0 - lower) * sig
    k = (1.0 - lower) * (1.0 - sig)
    b = _block_cumsum(jnp.log(forget), block)
    b_last = _block_last(b, block)
    return q * jnp.exp(b), k * jnp.exp(-b), k * jnp.exp(b_last - b), v, sg, jnp.exp(b_last)


def _hgrn_prompt_kernel(x_ref, mod_ref, g_ref, win_ref, lb_ref, ng_ref, wout_ref, y_ref, st_ref,
                        state_scr, o_scr, *, layer):
    i = pl.program_id(1)
    rows, d = x_ref.shape
    heads = d // HG_WIDTH
    w = HG_WIDTH

    @pl.when(i == 0)
    def _():
        state_scr[...] = jnp.zeros(state_scr.shape, F32)

    x = x_ref[...]
    h = _mod_in(x, g_ref[0:1, :], mod_ref, 0, 1).astype(BF16)
    q_dec, k_inv, k_end, v, sg, e_last = _hgrn_project(h, win_ref, _hgrn_lower(lb_ref, layer), d, CHUNK)
    t = lax.broadcasted_iota(jnp.int32, (CHUNK, CHUNK), 0)
    s = lax.broadcasted_iota(jnp.int32, (CHUNK, CHUNK), 1)
    causal = t >= s
    for c in range(rows // CHUNK):
        rc = slice(c * CHUNK, (c + 1) * CHUNK)
        for hh in range(heads):
            ch = slice(hh * w, (hh + 1) * w)
            qd = q_dec[rc, ch].astype(BF16)
            vv = v[rc, ch].astype(BF16)
            state_t = state_scr[hh]
            scores = jnp.where(causal, _dot_nt(qd, k_inv[rc, ch].astype(BF16)), 0.0)
            o_scr[rc, ch] = _dot(scores.astype(BF16), vv) + _dot_nt(qd, state_t.astype(BF16))
            state_scr[hh] = (e_last[c * CHUNK:c * CHUNK + 1, ch] * state_t
                             + _dot_tn(vv, k_end[rc, ch].astype(BF16)))
    outs = []
    for hh in range(heads):
        ch = slice(hh * w, (hh + 1) * w)
        outs.append((_rms_rows(o_scr[:, ch]) * ng_ref[:, ch] * sg[:, ch]).astype(BF16))
    oc = jnp.concatenate(outs, axis=1)
    y_ref[...] = _resid_out(x, _dot(oc, wout_ref[...]), g_ref[1:2, :], mod_ref[2])

    @pl.when(i == pl.num_programs(1) - 1)
    def _():
        for hh in range(heads):
            st_ref[hh] = state_scr[hh].T


def _hgrn_prompt(x, mod, gains, w_in, norm_g, w_out, lower_bounds, layer, tm):
    b, s, d = x.shape
    tm = min(tm, s)
    heads = d // HG_WIDTH
    x_spec, mod_spec = _row_specs(x, mod, tm)
    st_spec = pl.BlockSpec((None, heads, HG_WIDTH, HG_WIDTH), lambda bb, i: (bb, 0, 0, 0))
    vmem = _nbytes(w_in, w_out) + 40 * tm * d * 4 + (8 << 20)
    return pl.pallas_call(
        functools.partial(_hgrn_prompt_kernel, layer=layer),
        grid=(b, s // tm),
        in_specs=[x_spec, mod_spec, _const_spec(gains.shape), _const_spec(w_in.shape),
                  _const_spec(lower_bounds.shape), _const_spec((1, d)), _const_spec(w_out.shape)],
        out_specs=[x_spec, st_spec],
        out_shape=[jax.ShapeDtypeStruct(x.shape, F32),
                   jax.ShapeDtypeStruct((b, heads, HG_WIDTH, HG_WIDTH), F32)],
        scratch_shapes=[pltpu.VMEM((heads, HG_WIDTH, HG_WIDTH), F32), pltpu.VMEM((tm, d), F32)],
        compiler_params=_params(2, vmem),
        name="hgrn_prompt",
    )(x, mod, gains, w_in, lower_bounds, norm_g.reshape(1, d), w_out)


def _hgrn_sample_kernel(x_ref, mod_ref, g_ref, win_ref, lb_ref, ng_ref, st_in_ref, wout_ref,
                        y_ref, st_out_ref, qd_scr, ke_scr, v_scr, sg_scr, el_scr, o_scr,
                        *, layer, rows):
    b = pl.program_id(0)
    total, d = x_ref.shape
    heads = d // HG_WIDTH
    w = HG_WIDTH

    @pl.when(b == 0)
    def _():
        h = _mod_in(x_ref[...], g_ref[0:1, :], mod_ref, 0, 1).astype(BF16)
        q_dec, k_inv, k_end, v, sg, e_last = _hgrn_project(h, win_ref, _hgrn_lower(lb_ref, layer), d, rows)
        qd_scr[...] = q_dec
        ke_scr[...] = k_end
        v_scr[...] = v
        sg_scr[...] = sg
        el_scr[...] = e_last
        t = lax.broadcasted_iota(jnp.int32, (total, total), 0)
        s = lax.broadcasted_iota(jnp.int32, (total, total), 1)
        ok = (t >= s) & ((t // rows) == (s // rows))
        for hh in range(heads):
            ch = slice(hh * w, (hh + 1) * w)
            scores = jnp.where(ok, _dot_nt(q_dec[:, ch].astype(BF16), k_inv[:, ch].astype(BF16)), 0.0)
            o_scr[:, ch] = _dot(scores.astype(BF16), v[:, ch].astype(BF16))

    r0 = pl.multiple_of(b * rows, rows)
    ta = lax.broadcasted_iota(jnp.int32, (total, 1), 0)
    mine = (ta >= r0) & (ta < r0 + rows)
    for hh in range(heads):
        ch = slice(hh * w, (hh + 1) * w)
        state = st_in_ref[hh]
        qd = qd_scr[pl.ds(r0, rows), ch].astype(BF16)
        o_scr[pl.ds(r0, rows), ch] += _dot(qd, state.astype(BF16))
        k_end = jnp.where(mine, ke_scr[:, ch], 0.0).astype(BF16)
        new_t = (el_scr[pl.ds(r0, 1), ch] * state.T
                 + _dot_tn(v_scr[:, ch].astype(BF16), k_end))
        st_out_ref[hh] = new_t.T

    @pl.when(b == pl.num_programs(0) - 1)
    def _():
        outs = []
        for hh in range(heads):
            ch = slice(hh * w, (hh + 1) * w)
            outs.append((_rms_rows(o_scr[:, ch]) * ng_ref[:, ch] * sg_scr[:, ch]).astype(BF16))
        oc = jnp.concatenate(outs, axis=1)
        y_ref[...] = _resid_out(x_ref[...], _dot(oc, wout_ref[...]), g_ref[1:2, :], mod_ref[2])


def _hgrn_sample(x, mod, gains, w_in, norm_g, w_out, lower_bounds, layer, state, rows):
    total, d = x.shape
    bs, heads, dk, dv = state.shape
    st_spec = pl.BlockSpec((None, heads, dk, dv), lambda b: (b, 0, 0, 0))
    scr = pltpu.VMEM((total, d), F32)
    vmem = _nbytes(w_in, w_out, x, x, mod) + 60 * total * d * 4 + (8 << 20)
    return pl.pallas_call(
        functools.partial(_hgrn_sample_kernel, layer=layer, rows=rows),
        grid=(bs,),
        in_specs=[_const_spec(x.shape), _const_spec(mod.shape), _const_spec(gains.shape),
                  _const_spec(w_in.shape), _const_spec(lower_bounds.shape), _const_spec((1, d)),
                  st_spec, _const_spec(w_out.shape)],
        out_specs=[pl.BlockSpec(x.shape, lambda b: (0, 0)), st_spec],
        out_shape=[jax.ShapeDtypeStruct(x.shape, F32), jax.ShapeDtypeStruct(state.shape, F32)],
        scratch_shapes=[scr, scr, scr, scr, scr, scr],
        compiler_params=_params(1, vmem),
        name="hgrn_sample",
    )(x, mod, gains, w_in, lower_bounds, norm_g.reshape(1, d), state, w_out)


def kernel(x_prompt, x_sample, cache_k_diff, cache_v_diff, state_retention, state_hgrn, c_prompt, c_sample, w_ada, b_ada, norm_gains, gmlp_w_in, gmlp_ln_g, gmlp_ln_b, gmlp_w_s, gmlp_b_s, gmlp_w_out, diff_w_in, diff_lambda, diff_subln, diff_w_out, ret_w_in, ret_w_out, hgrn_w_in, hgrn_norm, hgrn_w_out, hgrn_lower_bounds, ffn_w_in, ffn_w_out):
    bp, s, d = x_prompt.shape
    bs, ls, _ = x_sample.shape
    ms = bs * ls
    depth = w_ada.shape[0]
    past = cache_k_diff.shape[2]
    n_mix = 4

    m_all = _ada(jnp.concatenate([c_prompt, c_sample], axis=0), w_ada, b_ada)
    ffn_in16, ffn_out16 = ffn_w_in.astype(BF16), ffn_w_out.astype(BF16)
    yp = x_prompt
    ys = x_sample.reshape(1, ms, d)
    outs = {name: [] for name in ("gv", "kp", "vp", "ks", "vs", "rp", "rs", "hp", "hs")}
    for i in range(depth):
        kind, j = i % n_mix, i // n_mix
        m = m_all[i].reshape(bp + bs, 6, d)
        mod_p = m[:bp].reshape(bp, 6, 1, d)
        mod_s = jnp.repeat(m[bp:], ls, axis=0).transpose(1, 0, 2).reshape(1, 6, ms, d)
        gains = norm_gains[i]
        if kind == 0:
            w_in, w_out = gmlp_w_in[j].astype(BF16), gmlp_w_out[j].astype(BF16)
            args = (gmlp_ln_g[j], gmlp_ln_b[j], gmlp_w_s[j], gmlp_b_s[j], w_out)
            yp = _gmlp(yp, mod_p, gains, w_in, *args, tm=2 * GMLP_CHUNK, t_chunk=GMLP_CHUNK, emit_v=False)
            ys, v_rows = _gmlp(ys, mod_s, gains, w_in, *args, tm=ms, t_chunk=ls, emit_v=True)
            outs["gv"].append(v_rows.reshape(bs, ls, -1))
        elif kind == 1:
            lam_init = 0.8 - 0.6 * math.exp(-0.3 * i)
            w_in, w_out = diff_w_in[j].astype(BF16), diff_w_out[j].astype(BF16)
            hk, hv = 2 * DIFF_HEADS, DIFF_HEADS
            scale = (d // hk) ** -0.5
            q_t, k, v, k16, v_t, qn2, kn2 = _qkv(yp, mod_p, gains, w_in, tm=512,
                                                 scale=scale * math.log2(math.e), transposed=True)
            outs["kp"].append(k.reshape(bp, s, hk, d // hk))
            outs["vp"].append(v.reshape(bp, s, hv, d // hv))
            yp = _flash_prompt(q_t, k16, v_t, qn2[:, :, 0, 0], kn2[:, :, 0, 0], yp, mod_p, gains,
                               diff_lambda[j], diff_subln[j], w_out, lam_init, tq=512)
            q, k, v, k16, v16 = _qkv(ys, mod_s, gains, w_in, tm=512, scale=scale, transposed=False)
            outs["ks"].append(k.reshape(bs, ls, hk, d // hk))
            outs["vs"].append(v.reshape(bs, ls, hv, d // hv))
            cache_k_t = cache_k_diff[j].transpose(0, 2, 3, 1).reshape(bs, d, past)
            cache_v = cache_v_diff[j].reshape(bs, past * hv, d // hv)
            oc = _flash_sample(q[0], cache_k_t, cache_v, k16[0], v16[0], diff_lambda[j], diff_subln[j],
                               lam_init, rows=ls, tk=1024)
            ys = _outproj(oc[None], ys, mod_s, gains, w_out, tm=512)
        elif kind == 2:
            w_in, w_out = ret_w_in[j].astype(BF16), ret_w_out[j].astype(BF16)
            yp, st = _ret_prompt(yp, mod_p, gains, w_in, w_out, tm=256)
            outs["rp"].append(st)
            y2, st = _ret_sample(ys[0], mod_s[0, :3], gains, w_in, w_out, state_retention[j], rows=ls, past=past)
            ys = y2[None]
            outs["rs"].append(st)
        else:
            w_in, w_out = hgrn_w_in[j].astype(BF16), hgrn_w_out[j].astype(BF16)
            yp, st = _hgrn_prompt(yp, mod_p, gains, w_in, hgrn_norm[j], w_out, hgrn_lower_bounds, i, tm=256)
            outs["hp"].append(st)
            y2, st = _hgrn_sample(ys[0], mod_s[0, :3], gains, w_in, hgrn_norm[j], w_out, hgrn_lower_bounds, i,
                                  state_hgrn[j], rows=ls)
            ys = y2[None]
            outs["hs"].append(st)
        yp = _ffn(yp, mod_p, gains, ffn_in16, ffn_out16, i, tm=512)
        ys = _ffn(ys, mod_s, gains, ffn_in16, ffn_out16, i, tm=512)

    return (yp, ys.reshape(bs, ls, d), jnp.stack(outs["gv"]), jnp.stack(outs["kp"]), jnp.stack(outs["vp"]),
            jnp.stack(outs["ks"]), jnp.stack(outs["vs"]), jnp.stack(outs["rp"]), jnp.stack(outs["rs"]),
            jnp.stack(outs["hp"]), jnp.stack(outs["hs"]))
```

```python
import functools
import math

import numpy as np
import jax
import jax.numpy as jnp
from jax import lax
from jax.experimental import pallas as pl
from jax.experimental.pallas import tpu as pltpu

F32 = jnp.float32
BF16 = jnp.bfloat16
EPS = 1e-6
NEG_INF = -1e30

CHUNK = 64
GMLP_CHUNK = 128
GMLP_GROUPS = 8
DIFF_HEADS = 8
RET_HEADS = 4
HG_WIDTH = 128
LANES = 128
MXU_WIDTH = 256
SUB_ROWS = 256
VMEM_CAP = 60 << 20
SOFTMAX_SAFE_LOG2 = 96.0

_NT = (((1,), (1,)), ((), ()))
_TN = (((0,), (0,)), ((), ()))


def _dot(a, b):
    return jnp.dot(a, b, preferred_element_type=F32)


def _dot_nt(a, b):
    return lax.dot_general(a, b, _NT, preferred_element_type=F32)


def _dot_tn(a, b):
    return lax.dot_general(a, b, _TN, preferred_element_type=F32)


def _silu(x):
    return x * jax.nn.sigmoid(x)


def _gelu(x):
    return 0.5 * x * (1.0 + lax.erf(x * (2.0 ** -0.5)))


def _rms_rows(x):
    return x * lax.rsqrt(jnp.mean(x * x, axis=-1, keepdims=True) + EPS)


def _mod_in(x, g, mod_ref, k_shift, k_scale):
    return _rms_rows(x) * g * (1.0 + mod_ref[k_scale]) + mod_ref[k_shift]


def _resid_out(x, o, g, gate):
    return x + gate * (_rms_rows(o) * g)


def _params(n_grid, vmem_bytes):
    return pltpu.CompilerParams(
        dimension_semantics=("arbitrary",) * n_grid,
        vmem_limit_bytes=int(min(max(vmem_bytes, 32 << 20), VMEM_CAP)))


def _const_spec(shape):
    nd = len(shape)
    return pl.BlockSpec(shape, lambda *_: (0,) * nd, pipeline_mode=pl.Buffered(1))


def _row_specs(x, mod, tm):
    _, _, d = x.shape
    r = mod.shape[2]
    x_spec = pl.BlockSpec((None, tm, d), lambda b, i, *_: (b, i, 0))
    if r == 1:
        mod_spec = pl.BlockSpec((None, 6, 1, d), lambda b, i, *_: (b, 0, 0, 0))
    else:
        mod_spec = pl.BlockSpec((None, 6, tm, d), lambda b, i, *_: (b, 0, i, 0))
    return x_spec, mod_spec


def _nbytes(*arrays):
    return sum(int(np.prod(a.shape)) * jnp.dtype(a.dtype).itemsize for a in arrays)


def _ada_kernel(c_ref, w_ref, b_ref, o_ref):
    a = _silu(c_ref[...]).astype(BF16)
    o_ref[...] = _dot(a, w_ref[...].astype(BF16)) + b_ref[...]


def _ada(c_all, w_ada, b_ada):
    depth, d, n = w_ada.shape
    rows = c_all.shape[0]
    tn = n // 4
    return pl.pallas_call(
        _ada_kernel,
        grid=(depth, n // tn),
        in_specs=[pl.BlockSpec((rows, d), lambda l, j: (0, 0)),
                  pl.BlockSpec((None, d, tn), lambda l, j: (l, 0, j)),
                  pl.BlockSpec((None, 1, tn), lambda l, j: (l, 0, j))],
        out_specs=pl.BlockSpec((None, rows, tn), lambda l, j: (l, 0, j)),
        out_shape=jax.ShapeDtypeStruct((depth, rows, n), F32),
        compiler_params=_params(2, 3 * d * tn * 4 + (8 << 20)),
        name="ada_mod",
    )(c_all, w_ada, b_ada.reshape(depth, 1, n))


def _ffn_kernel(x_ref, mod_ref, g_ref, win_ref, wout_ref, y_ref, *, hidden, chunks):
    x = x_ref[...]
    h = _mod_in(x, g_ref[2:3, :], mod_ref, 3, 4).astype(BF16)
    acc = None
    for c0, cw in chunks:
        gate = _dot(h, win_ref[:, c0:c0 + cw])
        up = _dot(h, win_ref[:, hidden + c0:hidden + c0 + cw])
        act = (_silu(gate) * up).astype(BF16)
        part = _dot(act, wout_ref[c0:c0 + cw, :])
        acc = part if acc is None else acc + part
    y_ref[...] = _resid_out(x, acc, g_ref[3:4, :], mod_ref[5])


def _split_chunks(total, width):
    out, c0 = [], 0
    while c0 < total:
        out.append((c0, min(width, total - c0)))
        c0 += width
    return tuple(out)


def _layer_spec(shape, layer):
    nd = len(shape) - 1
    return pl.BlockSpec((None,) + tuple(shape[1:]), lambda *_: (layer,) + (0,) * nd,
                        pipeline_mode=pl.Buffered(1))


def _ffn(x, mod, gains, w_in, w_out, layer, tm):
    b, s, d = x.shape
    hidden = w_out.shape[1]
    tm = min(tm, s)
    x_spec, mod_spec = _row_specs(x, mod, tm)
    vmem = _nbytes(w_in[0], w_out[0]) + 6 * tm * d * 4 + 4 * tm * 1024 * 4 + (8 << 20)
    return pl.pallas_call(
        functools.partial(_ffn_kernel, hidden=hidden, chunks=_split_chunks(hidden, 4 * MXU_WIDTH)),
        grid=(b, s // tm),
        in_specs=[x_spec, mod_spec, _const_spec(gains.shape), _layer_spec(w_in.shape, layer),
                  _layer_spec(w_out.shape, layer)],
        out_specs=x_spec,
        out_shape=jax.ShapeDtypeStruct(x.shape, F32),
        compiler_params=_params(2, vmem),
        name="ffn",
    )(x, mod, gains, w_in, w_out)


def _gmlp_kernel(x_ref, mod_ref, g_ref, win_ref, lng_ref, lnb_ref, wbd_ref, brow_ref, wout_ref,
                 *out_and_scratch, half, groups, pair, emit_v):
    if emit_v:
        y_ref, vn_ref, v_scr = out_and_scratch
    else:
        y_ref, v_scr = out_and_scratch
    gw = half // groups
    cw = pair * gw
    nblk = half // cw
    rows = x_ref.shape[0]
    sub = wbd_ref.shape[1]
    x = x_ref[...]
    h_all = _mod_in(x, g_ref[0:1, :], mod_ref, 0, 1).astype(BF16)
    for r0 in range(0, rows, sub):
        rs = slice(r0, r0 + sub)
        h = h_all[rs]
        s1 = None
        s2 = None
        for j in range(nblk):
            v = _gelu(_dot(h, win_ref[:, half + j * cw:half + (j + 1) * cw]))
            v_scr[rs, j * cw:(j + 1) * cw] = v
            a1 = jnp.sum(v, axis=-1, keepdims=True)
            a2 = jnp.sum(v * v, axis=-1, keepdims=True)
            s1 = a1 if s1 is None else s1 + a1
            s2 = a2 if s2 is None else s2 + a2
        mu = s1 * (1.0 / half)
        rstd = lax.rsqrt(s2 * (1.0 / half) - mu * mu + EPS)
        acc = None
        for j in range(nblk):
            cols = slice(j * cw, (j + 1) * cw)
            vn = (v_scr[rs, cols] - mu) * rstd * lng_ref[:, cols] + lnb_ref[:, cols]
            if emit_v:
                vn_ref[rs, cols] = vn
            vnb = vn.astype(BF16)
            u = _gelu(_dot(h, win_ref[:, cols]))
            mixed = []
            for gg in range(pair):
                g = j * pair + gg
                mixed.append(_dot(wbd_ref[g], vnb[:, gg * gw:(gg + 1) * gw]) + brow_ref[:, g:g + 1])
            out = (u * jnp.concatenate(mixed, axis=1)).astype(BF16)
            part = _dot(out, wout_ref[cols, :])
            acc = part if acc is None else acc + part
        gate = mod_ref[2] if mod_ref.shape[1] == 1 else mod_ref[2, rs, :]
        y_ref[rs, :] = _resid_out(x[rs], acc, g_ref[1:2, :], gate)


def _gmlp(x, mod, gains, w_in, ln_g, ln_b, w_s, b_s, w_out, tm, sub, t_chunk, emit_v):
    b, s, d = x.shape
    half = w_out.shape[0]
    groups = w_s.shape[0]
    tm = min(tm, s)
    sub = min(sub, tm)
    n_rep = sub // t_chunk
    pos = np.arange(sub)
    expand = jnp.asarray(pos[:, None] % t_chunk == np.arange(t_chunk)[None, :], w_s.dtype)
    same = jnp.asarray(pos[:, None] // t_chunk == pos[None, :] // t_chunk)
    wt = jnp.tril(w_s[:, :t_chunk, :t_chunk])
    w_bd = jnp.einsum("rt,gts,cs->grc", expand, wt, expand, precision=lax.Precision.HIGHEST)
    w_bd = jnp.where(same[None], w_bd, 0.0).astype(BF16)
    b_rows = jnp.tile(b_s[:, :t_chunk].T, (n_rep, 1))
    x_spec, mod_spec = _row_specs(x, mod, tm)
    out_shape = [jax.ShapeDtypeStruct(x.shape, F32)]
    out_specs = [x_spec]
    if emit_v:
        out_shape.append(jax.ShapeDtypeStruct((b, s, half), F32))
        out_specs.append(pl.BlockSpec((None, tm, half), lambda bb, i: (bb, i, 0)))
    vmem = (_nbytes(w_in, w_out, w_bd) + tm * half * 4 * (5 if emit_v else 1)
            + 6 * tm * d * 4 + (12 << 20))
    res = pl.pallas_call(
        functools.partial(_gmlp_kernel, half=half, groups=groups, pair=2, emit_v=emit_v),
        grid=(b, s // tm),
        in_specs=[x_spec, mod_spec, _const_spec(gains.shape), _const_spec(w_in.shape),
                  _const_spec((1, half)), _const_spec((1, half)), _const_spec(w_bd.shape),
                  _const_spec(b_rows.shape), _const_spec(w_out.shape)],
        out_specs=out_specs,
        out_shape=out_shape,
        scratch_shapes=[pltpu.VMEM((tm, half), F32)],
        compiler_params=_params(2, vmem),
        name="gmlp_v" if emit_v else "gmlp",
    )(x, mod, gains, w_in, ln_g.reshape(1, half), ln_b.reshape(1, half), w_bd, b_rows, w_out)
    return res if emit_v else res[0]


def _outproj_kernel(o_ref, x_ref, mod_ref, g_ref, w_ref, y_ref):
    y_ref[...] = _resid_out(x_ref[...], _dot(o_ref[...], w_ref[...]), g_ref[1:2, :], mod_ref[2])


def _outproj(o, x, mod, gains, w_out, tm):
    b, s, d = x.shape
    tm = min(tm, s)
    k = o.shape[-1]
    x_spec, mod_spec = _row_specs(x, mod, tm)
    return pl.pallas_call(
        _outproj_kernel,
        grid=(b, s // tm),
        in_specs=[pl.BlockSpec((None, tm, k), lambda bb, i: (bb, i, 0)), x_spec, mod_spec,
                  _const_spec(gains.shape), _const_spec(w_out.shape)],
        out_specs=x_spec,
        out_shape=jax.ShapeDtypeStruct(x.shape, F32),
        compiler_params=_params(2, _nbytes(w_out) + 8 * tm * d * 4 + 2 * tm * k * 2 + (8 << 20)),
        name="outproj",
    )(o, x, mod, gains, w_out)


def _max_head_norm2(x_t, heads):
    d, rows = x_t.shape
    n2 = jnp.sum((x_t * x_t).reshape(heads, d // heads, rows), axis=1)
    return jnp.full((8, LANES), jnp.max(n2), F32)


def _qkv_kernel(x_ref, mod_ref, g_ref, w_ref, q_ref, k_ref, v_ref, k16_ref, v16_ref, *norm_refs,
                scale, transposed):
    d = x_ref.shape[-1]
    h = _mod_in(x_ref[...], g_ref[0:1, :], mod_ref, 0, 1).astype(BF16)
    q = _dot(h, w_ref[:, :d]) * scale
    k = _dot(h, w_ref[:, d:2 * d])
    k_ref[...] = k
    k16_ref[...] = k.astype(BF16)
    v = _dot(h, w_ref[:, 2 * d:])
    v_ref[...] = v
    if transposed:
        qn_ref, kn_ref = norm_refs
        q_t = q.T
        q_ref[...] = q_t.astype(BF16)
        v16_ref[...] = v.T.astype(BF16)
        qn_ref[...] = _max_head_norm2(q_t, 2 * DIFF_HEADS)
        kn_ref[...] = _max_head_norm2(k.T, 2 * DIFF_HEADS)
    else:
        q_ref[...] = q.astype(BF16)
        v16_ref[...] = v.astype(BF16)


def _qkv(x, mod, gains, w_in, tm, scale, transposed):
    b, s, d = x.shape
    tm = min(tm, s)
    x_spec, mod_spec = _row_specs(x, mod, tm)
    sds = jax.ShapeDtypeStruct
    out_specs = [x_spec, x_spec, x_spec, x_spec, x_spec]
    out_shape = [sds(x.shape, BF16), sds(x.shape, F32), sds(x.shape, F32), sds(x.shape, BF16),
                 sds(x.shape, BF16)]
    if transposed:
        t_spec = pl.BlockSpec((None, d, tm), lambda bb, i: (bb, 0, i))
        n_spec = pl.BlockSpec((None, None, 8, LANES), lambda bb, i: (bb, i, 0, 0))
        out_specs[0] = out_specs[4] = t_spec
        out_shape[0] = out_shape[4] = sds((b, d, s), BF16)
        out_specs += [n_spec, n_spec]
        out_shape += [sds((b, s // tm, 8, LANES), F32)] * 2
    return pl.pallas_call(
        functools.partial(_qkv_kernel, scale=scale, transposed=transposed),
        grid=(b, s // tm),
        in_specs=[x_spec, mod_spec, _const_spec(gains.shape), _const_spec(w_in.shape)],
        out_specs=out_specs,
        out_shape=out_shape,
        compiler_params=_params(2, _nbytes(w_in) + 24 * tm * d * 4 + (8 << 20)),
        name="diff_qkv_t" if transposed else "diff_qkv",
    )(x, mod, gains, w_in)


def _diff_lambda(lam_ref, lam_init):
    lp = lam_ref[...]
    e1 = jnp.exp(jnp.sum(lp[0:1, :] * lp[1:2, :], axis=-1, keepdims=True))
    e2 = jnp.exp(jnp.sum(lp[2:3, :] * lp[3:4, :], axis=-1, keepdims=True))
    return e1 - e2 + lam_init


def _stack_q_halves(q, qs_scr, rows):
    lane = lax.broadcasted_iota(jnp.int32, q.shape, 1)
    first = (lane & (LANES - 1)) < (LANES // 2)
    zero = jnp.zeros_like(q)
    qs_scr[0:rows, :] = jnp.where(first, q, zero)
    qs_scr[rows:2 * rows, :] = jnp.where(first, zero, q)


def _flash_init(m_scr, l_scr, acc_scr):
    m_scr[...] = jnp.full(m_scr.shape, NEG_INF, F32)
    l_scr[...] = jnp.zeros(l_scr.shape, F32)
    acc_scr[...] = jnp.zeros(acc_scr.shape, F32)


def _flash_step(qs_scr, scores, v_blk, m_scr, l_scr, acc_scr, mask):
    sl = lambda h: slice(LANES * h, LANES * (h + 1))
    s_all = [scores(h, qs_scr[:, sl(h)]) for h in range(DIFF_HEADS)]
    stats = []
    for h, s in enumerate(s_all):
        if mask is not None:
            s = jnp.where(mask, s, NEG_INF)
        m_old = m_scr[h]
        m_new = jnp.maximum(m_old, jnp.max(s, axis=1, keepdims=True))
        alpha = jnp.exp(m_old - m_new)
        p = jnp.exp(s - m_new)
        l_scr[h] = alpha * l_scr[h] + jnp.sum(p, axis=1, keepdims=True)
        m_scr[h] = m_new
        stats.append((alpha, p.astype(BF16)))
    for h, (alpha, p) in enumerate(stats):
        acc_scr[:, sl(h)] = alpha * acc_scr[:, sl(h)] + _dot(p, v_blk(h))


def _flash_finish(l_scr, acc_scr, lam, subln, lam_init, rows):
    outs = []
    for h in range(DIFF_HEADS):
        sl = slice(LANES * h, LANES * (h + 1))
        l = l_scr[h]
        o = acc_scr[0:rows, sl] / l[0:rows] - lam * (acc_scr[rows:2 * rows, sl] / l[rows:2 * rows])
        outs.append((_rms_rows(o) * subln * (1.0 - lam_init)).astype(BF16))
    return jnp.concatenate(outs, axis=1)


def _chunk_mask(rows, cols, row_pos0, col_pos0):
    r = lax.broadcasted_iota(jnp.int32, (2 * rows, cols), 0)
    c = lax.broadcasted_iota(jnp.int32, (2 * rows, cols), 1)
    r = jnp.where(r >= rows, r - rows, r)
    return ((c + col_pos0) // CHUNK) <= ((r + row_pos0) // CHUNK)


def _flash_t_step(qz_scr, k_ref, vt_ref, m_scr, l_scr, acc_scr, mask, bounded=False):
    heads = [(h, c) for h in range(DIFF_HEADS) for c in range(2)]
    rows = lambda h: slice(LANES * h, LANES * (h + 1))

    def scores(i):
        h, c = heads[i]
        return _dot(k_ref[:, rows(h)], qz_scr[c, rows(h), :])

    def softmax(i, s):
        r = 2 * heads[i][0] + heads[i][1]
        if mask is not None:
            s = jnp.where(mask, s, NEG_INF)
        m_old = m_scr[r:r + 1, :]
        block_ref = 0.0 if bounded else jnp.max(s, axis=0, keepdims=True)
        m_new = jnp.maximum(m_old, block_ref)
        alpha = jnp.exp2(m_old - m_new)
        p = jnp.exp2(s - m_new)
        l_scr[r:r + 1, :] = alpha * l_scr[r:r + 1, :] + jnp.sum(p, axis=0, keepdims=True)
        m_scr[r:r + 1, :] = m_new
        return alpha, p.astype(BF16)

    def accumulate(i, alpha, p):
        h, c = heads[i]
        acc_scr[c, rows(h), :] = alpha * acc_scr[c, rows(h), :] + _dot(vt_ref[rows(h), :], p)

    n = len(heads)
    ahead, behind = 2, 1
    pending_s = {i: scores(i) for i in range(ahead)}
    pending_p = {}
    for i in range(n):
        if i + ahead < n:
            pending_s[i + ahead] = scores(i + ahead)
        pending_p[i] = softmax(i, pending_s.pop(i))
        if i - behind >= 0:
            accumulate(i - behind, *pending_p.pop(i - behind))
    for i in sorted(pending_p):
        accumulate(i, *pending_p[i])


def _flash_prompt_kernel(qt_ref, kt_ref, safe_ref, q_ref, k_ref, vt_ref, x_ref, mod_ref, g_ref, lam_ref,
                         subln_ref, woutt_ref, y_ref, qz_scr, m_scr, l_scr, acc_scr, *, lam_init):
    p = pl.program_id(1)
    qi = qt_ref[p]
    ki = kt_ref[p]
    bounded = safe_ref[pl.program_id(0) * pl.num_programs(1) + p] != 0
    tk = k_ref.shape[0]
    tq = q_ref.shape[1]

    @pl.when(ki == 0)
    def _():
        m_scr[...] = jnp.full(m_scr.shape, NEG_INF, F32)
        l_scr[...] = jnp.zeros(l_scr.shape, F32)
        acc_scr[...] = jnp.zeros(acc_scr.shape, F32)
        q = q_ref[...]
        feat = lax.broadcasted_iota(jnp.int32, q.shape, 0)
        first = (feat & (LANES - 1)) < (LANES // 2)
        zero = jnp.zeros_like(q)
        qz_scr[0] = jnp.where(first, q, zero)
        qz_scr[1] = jnp.where(first, zero, q)

    @pl.when((ki < qi) & bounded)
    def _():
        _flash_t_step(qz_scr, k_ref, vt_ref, m_scr, l_scr, acc_scr, None, bounded=True)

    @pl.when((ki < qi) & jnp.logical_not(bounded))
    def _():
        _flash_t_step(qz_scr, k_ref, vt_ref, m_scr, l_scr, acc_scr, None)

    @pl.when(ki == qi)
    def _():
        key = lax.broadcasted_iota(jnp.int32, (tk, tq), 0)
        qry = lax.broadcasted_iota(jnp.int32, (tk, tq), 1)
        _flash_t_step(qz_scr, k_ref, vt_ref, m_scr, l_scr, acc_scr, (key // CHUNK) <= (qry // CHUNK))
        lam = _diff_lambda(lam_ref, lam_init)
        outs = []
        for h in range(DIFF_HEADS):
            rows = slice(LANES * h, LANES * (h + 1))
            o = (acc_scr[0, rows, :] / l_scr[2 * h:2 * h + 1, :]
                 - lam * (acc_scr[1, rows, :] / l_scr[2 * h + 1:2 * h + 2, :]))
            o = o * lax.rsqrt(jnp.mean(o * o, axis=0, keepdims=True) + EPS)
            outs.append((o * (subln_ref[...] * (1.0 - lam_init))).astype(BF16))
        out_t = _dot(woutt_ref[...], jnp.concatenate(outs, axis=0))
        y_ref[...] = _resid_out(x_ref[...], out_t.T, g_ref[1:2, :], mod_ref[2])


def _flash_prompt(q_t, k16, v_t, qn2, kn2, x, mod, gains, lam_p, subln, w_out, lam_init, tq):
    b, s, d = x.shape
    tq = min(tq, s)
    nq = s // tq
    assert qn2.shape == kn2.shape == (b, nq)
    pairs = [(qi, ki) for qi in range(nq) for ki in range(qi + 1)]
    qt = jnp.asarray([pq for pq, _ in pairs], jnp.int32)
    kt = jnp.asarray([pk for _, pk in pairs], jnp.int32)
    safe = (qn2[:, qt] * kn2[:, kt] <= SOFTMAX_SAFE_LOG2 ** 2).astype(jnp.int32).reshape(-1)
    x_spec = pl.BlockSpec((None, tq, d), lambda bb, p, qt_, kt_, safe_: (bb, qt_[p], 0))
    q_spec = pl.BlockSpec((None, d, tq), lambda bb, p, qt_, kt_, safe_: (bb, 0, qt_[p]))
    k_spec = pl.BlockSpec((None, tq, d), lambda bb, p, qt_, kt_, safe_: (bb, kt_[p], 0))
    v_spec = pl.BlockSpec((None, d, tq), lambda bb, p, qt_, kt_, safe_: (bb, 0, kt_[p]))
    mod_spec = pl.BlockSpec((None, 6, 1, d), lambda bb, p, qt_, kt_, safe_: (bb, 0, 0, 0))
    w_out_t = w_out.T
    vmem = (_nbytes(w_out) + 6 * tq * d * 2 + 4 * tq * d * 4 + 2 * tq * d * 2 + 2 * tq * d * 4
            + 8 * tq * tq * 4 + (8 << 20))
    grid_spec = pltpu.PrefetchScalarGridSpec(
        num_scalar_prefetch=3,
        grid=(b, len(pairs)),
        in_specs=[q_spec, k_spec, v_spec, x_spec, mod_spec, _const_spec(gains.shape),
                  _const_spec(lam_p.shape), _const_spec((LANES, 1)), _const_spec(w_out_t.shape)],
        out_specs=x_spec,
        scratch_shapes=[pltpu.VMEM((2, d, tq), BF16),
                        pltpu.VMEM((2 * DIFF_HEADS, tq), F32),
                        pltpu.VMEM((2 * DIFF_HEADS, tq), F32),
                        pltpu.VMEM((2, d, tq), F32)])
    return pl.pallas_call(
        functools.partial(_flash_prompt_kernel, lam_init=lam_init),
        grid_spec=grid_spec,
        out_shape=jax.ShapeDtypeStruct(x.shape, F32),
        compiler_params=_params(2, vmem),
        name="diff_flash_prompt",
    )(qt, kt, safe, q_t, k16, v_t, x, mod, gains, lam_p, subln.reshape(LANES, 1), w_out_t)


def _flash_sample_kernel(q_ref, ck_ref, cv_ref, kn_ref, vn_ref, lam_ref, subln_ref, o_ref,
                         qs_scr, kpad_scr, vpad_scr, m_scr, l_scr, acc_scr,
                         *, rows, past, lam_init, new_mask_needed):
    kb = pl.program_id(1)

    @pl.when(kb == 0)
    def _():
        _flash_init(m_scr, l_scr, acc_scr)
        _stack_q_halves(q_ref[...], qs_scr, rows)
        kpad_scr[...] = jnp.zeros(kpad_scr.shape, BF16)
        vpad_scr[...] = jnp.zeros(vpad_scr.shape, BF16)
        kpad_scr[0:rows, :] = kn_ref[...]
        vpad_scr[0:rows, :] = vn_ref[...]
        c = lax.broadcasted_iota(jnp.int32, (2 * rows, LANES), 1)
        mask = c < rows
        if new_mask_needed:
            mask = mask & _chunk_mask(rows, LANES, past, past)
        _flash_step(qs_scr,
                    lambda h, qp: _dot_nt(qp, kpad_scr[:, LANES * h:LANES * (h + 1)]),
                    lambda h: vpad_scr[:, LANES * h:LANES * (h + 1)],
                    m_scr, l_scr, acc_scr, mask)

    tk = ck_ref.shape[1]
    _flash_step(qs_scr,
                lambda h, qp: _dot(qp, ck_ref[LANES * h:LANES * (h + 1), :].astype(BF16)),
                lambda h: cv_ref[pl.ds(h, tk, stride=DIFF_HEADS), :].astype(BF16),
                m_scr, l_scr, acc_scr, None)

    @pl.when(kb == pl.num_programs(1) - 1)
    def _():
        lam = _diff_lambda(lam_ref, lam_init)
        o_ref[...] = _flash_finish(l_scr, acc_scr, lam, subln_ref[...], lam_init, rows)


def _flash_sample(q, cache_k_t, cache_v, k_new, v_new, lam_p, subln, lam_init, rows, tk):
    bs, d, past = cache_k_t.shape
    tk = min(tk, past)
    pos = past + np.arange(rows)
    new_mask_needed = not bool(np.all((pos[None, :] // CHUNK) <= (pos[:, None] // CHUNK)))
    row_spec = pl.BlockSpec((rows, d), lambda b, kb: (b, 0))
    k_spec = pl.BlockSpec((None, d, tk), lambda b, kb: (b, 0, kb))
    v_spec = pl.BlockSpec((None, tk * DIFF_HEADS, LANES), lambda b, kb: (b, kb, 0))
    vmem = 4 * tk * d * 4 + 4 * tk * d * 2 + (12 << 20)
    return pl.pallas_call(
        functools.partial(_flash_sample_kernel, rows=rows, past=past, lam_init=lam_init,
                          new_mask_needed=new_mask_needed),
        grid=(bs, past // tk),
        in_specs=[row_spec, k_spec, v_spec, row_spec, row_spec,
                  _const_spec(lam_p.shape), _const_spec((1, LANES))],
        out_specs=row_spec,
        out_shape=jax.ShapeDtypeStruct(q.shape, BF16),
        scratch_shapes=[pltpu.VMEM((2 * rows, d), BF16),
                        pltpu.VMEM((LANES, d), BF16), pltpu.VMEM((LANES, d), BF16),
                        pltpu.VMEM((DIFF_HEADS, 2 * rows, 1), F32),
                        pltpu.VMEM((DIFF_HEADS, 2 * rows, 1), F32),
                        pltpu.VMEM((2 * rows, d), F32)],
        compiler_params=_params(2, vmem),
        name="diff_flash_sample",
    )(q, cache_k_t, cache_v, k_new, v_new, lam_p, subln.reshape(1, LANES))


def _ret_log_gamma(h):
    return float(np.log(np.float32(1.0) - np.float32(2.0) ** np.float32(-5.0 - h)))


def _rotate_pairs(x):
    n = x.shape[-1]
    lane = lax.broadcasted_iota(jnp.int32, x.shape, 1)
    return jnp.where((lane & 1) == 0, -pltpu.roll(x, n - 1, 1), pltpu.roll(x, 1, 1))


def _ret_project(h, win_ref, cos, sin, d, dk):
    cos4 = jnp.concatenate([cos] * (d // dk), axis=1)
    sin4 = jnp.concatenate([sin] * (d // dk), axis=1)
    q = _dot(h, win_ref[:, 0:d])
    q = q * cos4 + _rotate_pairs(q) * sin4
    k = _dot(h, win_ref[:, d:2 * d])
    k = (k * cos4 + _rotate_pairs(k) * sin4) * (dk ** -0.5)
    v = _dot(h, win_ref[:, 2 * d:4 * d])
    return q, k, v


def _ret_gate(h, win_ref, d):
    return _silu(_dot(h, win_ref[:, 4 * d:6 * d]))


def _ret_decay(lg, rows, same_seq=None):
    t = lax.broadcasted_iota(jnp.int32, (rows, rows), 0)
    s = lax.broadcasted_iota(jnp.int32, (rows, rows), 1)
    ok = t >= s
    if same_seq is not None:
        ok = ok & ((t // same_seq) == (s // same_seq))
    diff = jnp.maximum(t - s, 0).astype(F32)
    return jnp.where(ok, jnp.exp(lg * diff), 0.0)


def _ret_prompt_kernel(x_ref, mod_ref, g_ref, win_ref, cos_ref, sin_ref, wout_ref, y_ref, st_ref,
                       state_scr, o_scr, *, heads):
    i = pl.program_id(1)
    rows, d = x_ref.shape
    dk = d // heads
    dv = 2 * dk

    @pl.when(i == 0)
    def _():
        state_scr[...] = jnp.zeros(state_scr.shape, F32)

    x = x_ref[...]
    h = _mod_in(x, g_ref[0:1, :], mod_ref, 0, 1).astype(BF16)
    sub = min(rows, SUB_ROWS)
    t = lax.broadcasted_iota(jnp.int32, (sub, 1), 0).astype(F32)
    lgs = [_ret_log_gamma(hh) for hh in range(heads)]
    states = [state_scr[hh] for hh in range(heads)]
    for r0 in range(0, rows, sub):
        hs = h[r0:r0 + sub]
        q, k, v = _ret_project(hs, win_ref, cos_ref[r0:r0 + sub, :], sin_ref[r0:r0 + sub, :], d, dk)
        sg = _ret_gate(hs, win_ref, d)
        qs = [q[:, hh * dk:(hh + 1) * dk].astype(BF16) for hh in range(heads)]
        ks = [k[:, hh * dk:(hh + 1) * dk] for hh in range(heads)]
        vs = [v[:, hh * dv:(hh + 1) * dv].astype(BF16) for hh in range(heads)]
        scores = [_dot_nt(qs[hh], ks[hh].astype(BF16)) for hh in range(heads)]
        cross = [_dot(qs[hh], states[hh].astype(BF16)) for hh in range(heads)]
        kv = [_dot_tn((ks[hh] * jnp.exp(lgs[hh] * (sub - 1.0 - t))).astype(BF16), vs[hh])
              for hh in range(heads)]
        states = [math.exp(lgs[hh] * sub) * states[hh] + kv[hh] for hh in range(heads)]
        inner = [_dot((scores[hh] * _ret_decay(lgs[hh], sub)).astype(BF16), vs[hh]) for hh in range(heads)]
        gated = []
        for hh in range(heads):
            o = inner[hh] + cross[hh] * jnp.exp(lgs[hh] * (t + 1.0))
            gated.append((_rms_rows(o) * sg[:, hh * dv:(hh + 1) * dv]).astype(BF16))
        o_scr[r0:r0 + sub, :] = jnp.concatenate(gated, axis=1)
    for hh in range(heads):
        state_scr[hh] = states[hh]
    y_ref[...] = _resid_out(x, _dot(o_scr[...], wout_ref[...]), g_ref[1:2, :], mod_ref[2])

    @pl.when(i == pl.num_programs(1) - 1)
    def _():
        st_ref[...] = state_scr[...]


def _xpos_tables(pos, dk):
    inv = 1.0 / (10000.0 ** jnp.linspace(0.0, 1.0, dk // 2, dtype=F32))
    ang = pos.astype(F32)[:, None] * jnp.repeat(inv, 2)[None, :]
    return jnp.cos(ang), jnp.sin(ang)


def _ret_prompt(x, mod, gains, w_in, w_out, tm):
    b, s, d = x.shape
    tm = min(tm, s)
    heads = RET_HEADS
    dk = d // heads
    dv = 2 * dk
    cos, sin = _xpos_tables(jnp.arange(s), dk)
    x_spec, mod_spec = _row_specs(x, mod, tm)
    tab_spec = pl.BlockSpec((tm, dk), lambda bb, i: (i, 0))
    st_spec = pl.BlockSpec((None, heads, dk, dv), lambda bb, i: (bb, 0, 0, 0))
    vmem = _nbytes(w_in, w_out) + 3 * heads * dk * dv * 4 + 40 * tm * d * 4 + (8 << 20)
    return pl.pallas_call(
        functools.partial(_ret_prompt_kernel, heads=heads),
        grid=(b, s // tm),
        in_specs=[x_spec, mod_spec, _const_spec(gains.shape), _const_spec(w_in.shape),
                  tab_spec, tab_spec, _const_spec(w_out.shape)],
        out_specs=[x_spec, st_spec],
        out_shape=[jax.ShapeDtypeStruct(x.shape, F32),
                   jax.ShapeDtypeStruct((b, heads, dk, dv), F32)],
        scratch_shapes=[pltpu.VMEM((heads, dk, dv), F32), pltpu.VMEM((tm, heads * dv), BF16)],
        compiler_params=_params(2, vmem),
        name="ret_prompt",
    )(x, mod, gains, w_in, cos, sin, w_out)


def _ret_sample_kernel(x_ref, mod_ref, g_ref, win_ref, cos_ref, sin_ref, st_in_ref, wout_ref,
                       y_ref, st_out_ref, q_scr, k_scr, v_scr, o_scr, *, heads, rows):
    b = pl.program_id(0)
    total, d = x_ref.shape
    dk = d // heads
    dv = 2 * dk

    @pl.when(b == 0)
    def _():
        h = _mod_in(x_ref[...], g_ref[0:1, :], mod_ref, 0, 1).astype(BF16)
        q, k, v = _ret_project(h, win_ref, cos_ref[...], sin_ref[...], d, dk)
        q_scr[...] = q.astype(BF16)
        k_scr[...] = k
        v_scr[...] = v.astype(BF16)
        for hh in range(heads):
            scores = (_dot_nt(q[:, hh * dk:(hh + 1) * dk].astype(BF16),
                              k[:, hh * dk:(hh + 1) * dk].astype(BF16))
                      * _ret_decay(_ret_log_gamma(hh), total, same_seq=rows))
            o_scr[:, hh * dv:(hh + 1) * dv] = _dot(scores.astype(BF16),
                                                   v[:, hh * dv:(hh + 1) * dv].astype(BF16))

    r0 = pl.multiple_of(b * rows, rows)
    t = lax.broadcasted_iota(jnp.int32, (rows, 1), 0).astype(F32)
    ta = lax.broadcasted_iota(jnp.int32, (total, 1), 0)
    mine = (ta >= r0) & (ta < r0 + rows)
    t_all = (ta - r0).astype(F32)
    for hh in range(heads):
        lg = _ret_log_gamma(hh)
        state = st_in_ref[hh]
        qh = q_scr[pl.ds(r0, rows), hh * dk:(hh + 1) * dk]
        cross = _dot(qh, state.astype(BF16)) * jnp.exp(lg * (t + 1.0))
        o_scr[pl.ds(r0, rows), hh * dv:(hh + 1) * dv] += cross
        k_dec = jnp.where(mine, k_scr[:, hh * dk:(hh + 1) * dk] * jnp.exp(lg * (rows - 1.0 - t_all)), 0.0)
        st_out_ref[hh] = (math.exp(lg * rows) * state
                          + _dot_tn(k_dec.astype(BF16), v_scr[:, hh * dv:(hh + 1) * dv]))

    @pl.when(b == pl.num_programs(0) - 1)
    def _():
        h = _mod_in(x_ref[...], g_ref[0:1, :], mod_ref, 0, 1).astype(BF16)
        gated = []
        for hh in range(heads):
            sl = slice(hh * dv, (hh + 1) * dv)
            sg = _silu(_dot(h, win_ref[:, 4 * d + hh * dv:4 * d + (hh + 1) * dv]))
            gated.append((_rms_rows(o_scr[:, sl]) * sg).astype(BF16))
        oc = jnp.concatenate(gated, axis=1)
        y_ref[...] = _resid_out(x_ref[...], _dot(oc, wout_ref[...]), g_ref[1:2, :], mod_ref[2])


def _ret_sample(x, mod, gains, w_in, w_out, state, rows, past):
    total, d = x.shape
    bs, heads, dk, dv = state.shape
    cos, sin = _xpos_tables(past + jnp.arange(rows), dk)
    cos = jnp.tile(cos, (bs, 1))
    sin = jnp.tile(sin, (bs, 1))
    st_spec = pl.BlockSpec((None, heads, dk, dv), lambda b: (b, 0, 0, 0))
    vmem = _nbytes(w_in, w_out, x, x, mod) + 4 * heads * dk * dv * 4 + 60 * total * d * 4 + (8 << 20)
    return pl.pallas_call(
        functools.partial(_ret_sample_kernel, heads=heads, rows=rows),
        grid=(bs,),
        in_specs=[_const_spec(x.shape), _const_spec(mod.shape), _const_spec(gains.shape),
                  _const_spec(w_in.shape), _const_spec(cos.shape), _const_spec(sin.shape),
                  st_spec, _const_spec(w_out.shape)],
        out_specs=[pl.BlockSpec(x.shape, lambda b: (0, 0)), st_spec],
        out_shape=[jax.ShapeDtypeStruct(x.shape, F32), jax.ShapeDtypeStruct(state.shape, F32)],
        scratch_shapes=[pltpu.VMEM((total, d), BF16), pltpu.VMEM((total, d), F32),
                        pltpu.VMEM((total, 2 * d), BF16), pltpu.VMEM((total, 2 * d), F32)],
        compiler_params=_params(1, vmem),
        name="ret_sample",
    )(x, mod, gains, w_in, cos, sin, state, w_out)


def _hgrn_lower(lb_ref, layer):
    lb = lb_ref[...]
    e = jnp.exp(lb - jnp.max(lb, axis=0, keepdims=True))
    p = e / jnp.sum(e, axis=0, keepdims=True)
    if layer == 0:
        return jnp.zeros_like(p[0:1, :])
    return jnp.sum(p[1:layer + 1, :], axis=0, keepdims=True)


def _block_cumsum(x, block):
    rows = x.shape[0]
    t = lax.broadcasted_iota(jnp.int32, (rows, rows), 0)
    s = lax.broadcasted_iota(jnp.int32, (rows, rows), 1)
    tri = jnp.where((s <= t) & ((t // block) == (s // block)), 1.0, 0.0).astype(BF16)
    hi = x.astype(BF16)
    lo = (x - hi.astype(F32)).astype(BF16)
    return _dot(tri, hi) + _dot(tri, lo)


def _block_last(x, block):
    rows = x.shape[0]
    parts = [jnp.broadcast_to(x[c * block + block - 1:c * block + block, :], (block, x.shape[1]))
             for c in range(rows // block)]
    return parts[0] if len(parts) == 1 else jnp.concatenate(parts, axis=0)


def _hgrn_project(h, win_ref, lower, d, block):
    q = _silu(_dot(h, win_ref[:, 0:d]))
    f = _dot(h, win_ref[:, d:2 * d])
    v = _dot(h, win_ref[:, 2 * d:3 * d])
    sg = _silu(_dot(h, win_ref[:, 3 * d:4 * d]))
    sig = jax.nn.sigmoid(f)
    forget = lower + (1.0 - lower) * sig
    k = (1.0 - lower) * (1.0 - sig)
    b = _block_cumsum(jnp.log(forget), block)
    b_last = _block_last(b, block)
    return q * jnp.exp(b), k * jnp.exp(-b), k * jnp.exp(b_last - b), v, sg, jnp.exp(b_last)


def _hgrn_prompt_kernel(x_ref, mod_ref, g_ref, win_ref, lb_ref, ng_ref, wout_ref, y_ref, st_ref,
                        state_scr, o_scr, *, layer):
    i = pl.program_id(1)
    rows, d = x_ref.shape
    heads = d // HG_WIDTH
    w = HG_WIDTH

    @pl.when(i == 0)
    def _():
        state_scr[...] = jnp.zeros(state_scr.shape, F32)

    x = x_ref[...]
    h = _mod_in(x, g_ref[0:1, :], mod_ref, 0, 1).astype(BF16)
    lower = _hgrn_lower(lb_ref, layer)
    t = lax.broadcasted_iota(jnp.int32, (CHUNK, CHUNK), 0)
    s = lax.broadcasted_iota(jnp.int32, (CHUNK, CHUNK), 1)
    causal = t >= s
    ch = lambda hh: slice(hh * w, (hh + 1) * w)
    sub = min(rows, SUB_ROWS)
    n_chunks = sub // CHUNK
    cells = [(c, hh) for c in range(n_chunks) for hh in range(heads)]
    rc = lambda c: slice(c * CHUNK, (c + 1) * CHUNK)
    states = [state_scr[hh] for hh in range(heads)]
    for r0 in range(0, rows, sub):
        q_dec, k_inv, k_end, v, sg, e_last = _hgrn_project(h[r0:r0 + sub], win_ref, lower, d, CHUNK)
        qd16, ki16, ke16, v16 = (a.astype(BF16) for a in (q_dec, k_inv, k_end, v))
        kv = {(c, hh): _dot_tn(v16[rc(c), ch(hh)], ke16[rc(c), ch(hh)]) for c, hh in cells}
        scores = {(c, hh): jnp.where(causal, _dot_nt(qd16[rc(c), ch(hh)], ki16[rc(c), ch(hh)]),
                                     0.0).astype(BF16) for c, hh in cells}
        state_in = {}
        for hh in range(heads):
            for c in range(n_chunks):
                state_in[c, hh] = states[hh].astype(BF16)
                states[hh] = e_last[c * CHUNK:c * CHUNK + 1, ch(hh)] * states[hh] + kv[c, hh]
        o = [jnp.concatenate([_dot(scores[c, hh], v16[rc(c), ch(hh)])
                              + _dot_nt(qd16[rc(c), ch(hh)], state_in[c, hh]) for c in range(n_chunks)],
                             axis=0) for hh in range(heads)]
        o_scr[r0:r0 + sub, :] = jnp.concatenate(
            [(_rms_rows(o[hh]) * ng_ref[:, ch(hh)] * sg[:, ch(hh)]).astype(BF16) for hh in range(heads)], axis=1)
    for hh in range(heads):
        state_scr[hh] = states[hh]
    y_ref[...] = _resid_out(x, _dot(o_scr[...], wout_ref[...]), g_ref[1:2, :], mod_ref[2])

    @pl.when(i == pl.num_programs(1) - 1)
    def _():
        for hh in range(heads):
            st_ref[hh] = state_scr[hh].T


def _hgrn_prompt(x, mod, gains, w_in, norm_g, w_out, lower_bounds, layer, tm):
    b, s, d = x.shape
    tm = min(tm, s)
    heads = d // HG_WIDTH
    x_spec, mod_spec = _row_specs(x, mod, tm)
    st_spec = pl.BlockSpec((None, heads, HG_WIDTH, HG_WIDTH), lambda bb, i: (bb, 0, 0, 0))
    vmem = _nbytes(w_in, w_out) + 40 * tm * d * 4 + (8 << 20)
    return pl.pallas_call(
        functools.partial(_hgrn_prompt_kernel, layer=layer),
        grid=(b, s // tm),
        in_specs=[x_spec, mod_spec, _const_spec(gains.shape), _const_spec(w_in.shape),
                  _const_spec(lower_bounds.shape), _const_spec((1, d)), _const_spec(w_out.shape)],
        out_specs=[x_spec, st_spec],
        out_shape=[jax.ShapeDtypeStruct(x.shape, F32),
                   jax.ShapeDtypeStruct((b, heads, HG_WIDTH, HG_WIDTH), F32)],
        scratch_shapes=[pltpu.VMEM((heads, HG_WIDTH, HG_WIDTH), F32), pltpu.VMEM((tm, d), BF16)],
        compiler_params=_params(2, vmem),
        name="hgrn_prompt",
    )(x, mod, gains, w_in, lower_bounds, norm_g.reshape(1, d), w_out)


def _hgrn_sample_kernel(x_ref, mod_ref, g_ref, win_ref, lb_ref, ng_ref, st_in_ref, wout_ref,
                        y_ref, st_out_ref, qd_scr, ke_scr, v_scr, sg_scr, el_scr, o_scr,
                        *, layer, rows):
    b = pl.program_id(0)
    total, d = x_ref.shape
    heads = d // HG_WIDTH
    w = HG_WIDTH

    @pl.when(b == 0)
    def _():
        h = _mod_in(x_ref[...], g_ref[0:1, :], mod_ref, 0, 1).astype(BF16)
        q_dec, k_inv, k_end, v, sg, e_last = _hgrn_project(h, win_ref, _hgrn_lower(lb_ref, layer), d, rows)
        qd_scr[...] = q_dec
        ke_scr[...] = k_end
        v_scr[...] = v
        sg_scr[...] = sg
        el_scr[...] = e_last
        t = lax.broadcasted_iota(jnp.int32, (total, total), 0)
        s = lax.broadcasted_iota(jnp.int32, (total, total), 1)
        ok = (t >= s) & ((t // rows) == (s // rows))
        for hh in range(heads):
            ch = slice(hh * w, (hh + 1) * w)
            scores = jnp.where(ok, _dot_nt(q_dec[:, ch].astype(BF16), k_inv[:, ch].astype(BF16)), 0.0)
            o_scr[:, ch] = _dot(scores.astype(BF16), v[:, ch].astype(BF16))

    r0 = pl.multiple_of(b * rows, rows)
    ta = lax.broadcasted_iota(jnp.int32, (total, 1), 0)
    mine = (ta >= r0) & (ta < r0 + rows)
    for hh in range(heads):
        ch = slice(hh * w, (hh + 1) * w)
        state = st_in_ref[hh]
        qd = qd_scr[pl.ds(r0, rows), ch].astype(BF16)
        o_scr[pl.ds(r0, rows), ch] += _dot(qd, state.astype(BF16))
        k_end = jnp.where(mine, ke_scr[:, ch], 0.0).astype(BF16)
        new_t = (el_scr[pl.ds(r0, 1), ch] * state.T
                 + _dot_tn(v_scr[:, ch].astype(BF16), k_end))
        st_out_ref[hh] = new_t.T

    @pl.when(b == pl.num_programs(0) - 1)
    def _():
        outs = []
        for hh in range(heads):
            ch = slice(hh * w, (hh + 1) * w)
            outs.append((_rms_rows(o_scr[:, ch]) * ng_ref[:, ch] * sg_scr[:, ch]).astype(BF16))
        oc = jnp.concatenate(outs, axis=1)
        y_ref[...] = _resid_out(x_ref[...], _dot(oc, wout_ref[...]), g_ref[1:2, :], mod_ref[2])


def _hgrn_sample(x, mod, gains, w_in, norm_g, w_out, lower_bounds, layer, state, rows):
    total, d = x.shape
    bs, heads, dk, dv = state.shape
    st_spec = pl.BlockSpec((None, heads, dk, dv), lambda b: (b, 0, 0, 0))
    scr = pltpu.VMEM((total, d), F32)
    vmem = _nbytes(w_in, w_out, x, x, mod) + 60 * total * d * 4 + (8 << 20)
    return pl.pallas_call(
        functools.partial(_hgrn_sample_kernel, layer=layer, rows=rows),
        grid=(bs,),
        in_specs=[_const_spec(x.shape), _const_spec(mod.shape), _const_spec(gains.shape),
                  _const_spec(w_in.shape), _const_spec(lower_bounds.shape), _const_spec((1, d)),
                  st_spec, _const_spec(w_out.shape)],
        out_specs=[pl.BlockSpec(x.shape, lambda b: (0, 0)), st_spec],
        out_shape=[jax.ShapeDtypeStruct(x.shape, F32), jax.ShapeDtypeStruct(state.shape, F32)],
        scratch_shapes=[scr, scr, scr, scr, scr, scr],
        compiler_params=_params(1, vmem),
        name="hgrn_sample",
    )(x, mod, gains, w_in, lower_bounds, norm_g.reshape(1, d), state, w_out)


def kernel(x_prompt, x_sample, cache_k_diff, cache_v_diff, state_retention, state_hgrn, c_prompt, c_sample, w_ada, b_ada, norm_gains, gmlp_w_in, gmlp_ln_g, gmlp_ln_b, gmlp_w_s, gmlp_b_s, gmlp_w_out, diff_w_in, diff_lambda, diff_subln, diff_w_out, ret_w_in, ret_w_out, hgrn_w_in, hgrn_norm, hgrn_w_out, hgrn_lower_bounds, ffn_w_in, ffn_w_out):
    bp, s, d = x_prompt.shape
    bs, ls, _ = x_sample.shape
    ms = bs * ls
    depth = w_ada.shape[0]
    past = cache_k_diff.shape[2]
    n_mix = 4

    m_all = _ada(jnp.concatenate([c_prompt, c_sample], axis=0), w_ada, b_ada)
    ffn_in16, ffn_out16 = ffn_w_in.astype(BF16), ffn_w_out.astype(BF16)
    yp = x_prompt
    ys = x_sample.reshape(1, ms, d)
    outs = {name: [] for name in ("gv", "kp", "vp", "ks", "vs", "rp", "rs", "hp", "hs")}
    for i in range(depth):
        kind, j = i % n_mix, i // n_mix
        m = m_all[i].reshape(bp + bs, 6, d)
        mod_p = m[:bp].reshape(bp, 6, 1, d)
        mod_s = jnp.repeat(m[bp:], ls, axis=0).transpose(1, 0, 2).reshape(1, 6, ms, d)
        gains = norm_gains[i]
        if kind == 0:
            w_in, w_out = gmlp_w_in[j].astype(BF16), gmlp_w_out[j].astype(BF16)
            args = (gmlp_ln_g[j], gmlp_ln_b[j], gmlp_w_s[j], gmlp_b_s[j], w_out)
            yp = _gmlp(yp, mod_p, gains, w_in, *args, tm=512, sub=SUB_ROWS, t_chunk=GMLP_CHUNK, emit_v=False)
            ys, v_rows = _gmlp(ys, mod_s, gains, w_in, *args, tm=ms, sub=ms, t_chunk=ls, emit_v=True)
            outs["gv"].append(v_rows.reshape(bs, ls, -1))
        elif kind == 1:
            lam_init = 0.8 - 0.6 * math.exp(-0.3 * i)
            w_in, w_out = diff_w_in[j].astype(BF16), diff_w_out[j].astype(BF16)
            hk, hv = 2 * DIFF_HEADS, DIFF_HEADS
            scale = (d // hk) ** -0.5
            q_t, k, v, k16, v_t, qn2, kn2 = _qkv(yp, mod_p, gains, w_in, tm=512,
                                                 scale=scale * math.log2(math.e), transposed=True)
            outs["kp"].append(k.reshape(bp, s, hk, d // hk))
            outs["vp"].append(v.reshape(bp, s, hv, d // hv))
            yp = _flash_prompt(q_t, k16, v_t, qn2[:, :, 0, 0], kn2[:, :, 0, 0], yp, mod_p, gains,
                               diff_lambda[j], diff_subln[j], w_out, lam_init, tq=512)
            q, k, v, k16, v16 = _qkv(ys, mod_s, gains, w_in, tm=512, scale=scale, transposed=False)
            outs["ks"].append(k.reshape(bs, ls, hk, d // hk))
            outs["vs"].append(v.reshape(bs, ls, hv, d // hv))
            cache_k_t = cache_k_diff[j].transpose(0, 2, 3, 1).reshape(bs, d, past)
            cache_v = cache_v_diff[j].reshape(bs, past * hv, d // hv)
            oc = _flash_sample(q[0], cache_k_t, cache_v, k16[0], v16[0], diff_lambda[j], diff_subln[j],
                               lam_init, rows=ls, tk=1024)
            ys = _outproj(oc[None], ys, mod_s, gains, w_out, tm=512)
        elif kind == 2:
            w_in, w_out = ret_w_in[j].astype(BF16), ret_w_out[j].astype(BF16)
            yp, st = _ret_prompt(yp, mod_p, gains, w_in, w_out, tm=512)
            outs["rp"].append(st)
            y2, st = _ret_sample(ys[0], mod_s[0, :3], gains, w_in, w_out, state_retention[j], rows=ls, past=past)
            ys = y2[None]
            outs["rs"].append(st)
        else:
            w_in, w_out = hgrn_w_in[j].astype(BF16), hgrn_w_out[j].astype(BF16)
            yp, st = _hgrn_prompt(yp, mod_p, gains, w_in, hgrn_norm[j], w_out, hgrn_lower_bounds, i, tm=512)
            outs["hp"].append(st)
            y2, st = _hgrn_sample(ys[0], mod_s[0, :3], gains, w_in, hgrn_norm[j], w_out, hgrn_lower_bounds, i,
                                  state_hgrn[j], rows=ls)
            ys = y2[None]
            outs["hs"].append(st)
        yp = _ffn(yp, mod_p, gains, ffn_in16, ffn_out16, i, tm=512)
        ys = _ffn(ys, mod_s, gains, ffn_in16, ffn_out16, i, tm=512)

    return (yp, ys.reshape(bs, ls, d), jnp.stack(outs["gv"]), jnp.stack(outs["kp"]), jnp.stack(outs["vp"]),
            jnp.stack(outs["ks"]), jnp.stack(outs["vs"]), jnp.stack(outs["rp"]), jnp.stack(outs["rs"]),
            jnp.stack(outs["hp"]), jnp.stack(outs["hs"]))
```

```python
import functools
import math

import numpy as np
import jax
import jax.numpy as jnp
from jax import lax
from jax.experimental import pallas as pl
from jax.experimental.pallas import tpu as pltpu

F32 = jnp.float32
BF16 = jnp.bfloat16
EPS = 1e-6
NEG_INF = -1e30

CHUNK = 64
GMLP_CHUNK = 128
GMLP_GROUPS = 8
DIFF_HEADS = 8
RET_HEADS = 4
HG_WIDTH = 128
LANES = 128
MXU_WIDTH = 256
SUB_ROWS = 256
VMEM_CAP = 60 << 20
SOFTMAX_SAFE_LOG2 = 96.0

_NT = (((1,), (1,)), ((), ()))
_TN = (((0,), (0,)), ((), ()))


def _dot(a, b):
    return jnp.dot(a, b, preferred_element_type=F32)


def _dot_nt(a, b):
    return lax.dot_general(a, b, _NT, preferred_element_type=F32)


def _dot_tn(a, b):
    return lax.dot_general(a, b, _TN, preferred_element_type=F32)


def _silu(x):
    return x * jax.nn.sigmoid(x)


def _gelu(x):
    return 0.5 * x * (1.0 + lax.erf(x * (2.0 ** -0.5)))


def _rms_rows(x):
    return x * lax.rsqrt(jnp.mean(x * x, axis=-1, keepdims=True) + EPS)


def _mod_in(x, g, mod_ref, k_shift, k_scale):
    return _rms_rows(x) * g * (1.0 + mod_ref[k_scale]) + mod_ref[k_shift]


def _resid_out(x, o, g, gate):
    return x + gate * (_rms_rows(o) * g)


def _params(n_grid, vmem_bytes):
    return pltpu.CompilerParams(
        dimension_semantics=("arbitrary",) * n_grid,
        vmem_limit_bytes=int(min(max(vmem_bytes, 32 << 20), VMEM_CAP)))


def _const_spec(shape):
    nd = len(shape)
    return pl.BlockSpec(shape, lambda *_: (0,) * nd, pipeline_mode=pl.Buffered(1))


def _row_specs(x, mod, tm):
    _, _, d = x.shape
    r = mod.shape[2]
    x_spec = pl.BlockSpec((None, tm, d), lambda b, i, *_: (b, i, 0))
    if r == 1:
        mod_spec = pl.BlockSpec((None, 6, 1, d), lambda b, i, *_: (b, 0, 0, 0))
    else:
        mod_spec = pl.BlockSpec((None, 6, tm, d), lambda b, i, *_: (b, 0, i, 0))
    return x_spec, mod_spec


def _nbytes(*arrays):
    return sum(int(np.prod(a.shape)) * jnp.dtype(a.dtype).itemsize for a in arrays)


def _ada_kernel(c_ref, w_ref, b_ref, o_ref):
    a = _silu(c_ref[...]).astype(BF16)
    o_ref[...] = _dot(a, w_ref[...].astype(BF16)) + b_ref[...]


def _ada(c_all, w_ada, b_ada):
    depth, d, n = w_ada.shape
    rows = c_all.shape[0]
    tn = n // 4
    return pl.pallas_call(
        _ada_kernel,
        grid=(depth, n // tn),
        in_specs=[pl.BlockSpec((rows, d), lambda l, j: (0, 0)),
                  pl.BlockSpec((None, d, tn), lambda l, j: (l, 0, j)),
                  pl.BlockSpec((None, 1, tn), lambda l, j: (l, 0, j))],
        out_specs=pl.BlockSpec((None, rows, tn), lambda l, j: (l, 0, j)),
        out_shape=jax.ShapeDtypeStruct((depth, rows, n), F32),
        compiler_params=_params(2, 3 * d * tn * 4 + (8 << 20)),
        name="ada_mod",
    )(c_all, w_ada, b_ada.reshape(depth, 1, n))


def _ffn_kernel(x_ref, mod_ref, g_ref, win_ref, wout_ref, y_ref, *, hidden, chunks, sub):
    rows = x_ref.shape[0]
    per_row = mod_ref.shape[1] != 1
    for r0 in range(0, rows, sub):
        rs = slice(r0, r0 + sub)
        x = x_ref[rs, :]
        shift, scale, gate_out = (mod_ref[k, rs, :] if per_row else mod_ref[k] for k in (3, 4, 5))
        h = (_rms_rows(x) * g_ref[2:3, :] * (1.0 + scale) + shift).astype(BF16)
        acc = None
        for c0, cw in chunks:
            gate = _dot(h, win_ref[:, c0:c0 + cw])
            up = _dot(h, win_ref[:, hidden + c0:hidden + c0 + cw])
            act = (_silu(gate) * up).astype(BF16)
            part = _dot(act, wout_ref[c0:c0 + cw, :])
            acc = part if acc is None else acc + part
        y_ref[rs, :] = _resid_out(x, acc, g_ref[3:4, :], gate_out)


def _split_chunks(total, width):
    out, c0 = [], 0
    while c0 < total:
        out.append((c0, min(width, total - c0)))
        c0 += width
    return tuple(out)


def _layer_spec(shape, layer):
    nd = len(shape) - 1
    return pl.BlockSpec((None,) + tuple(shape[1:]), lambda *_: (layer,) + (0,) * nd,
                        pipeline_mode=pl.Buffered(1))


def _ffn(x, mod, gains, w_in, w_out, layer, tm, sub):
    b, s, d = x.shape
    hidden = w_out.shape[1]
    tm = min(tm, s)
    sub = min(sub, tm)
    x_spec, mod_spec = _row_specs(x, mod, tm)
    vmem = _nbytes(w_in[0], w_out[0]) + 6 * tm * d * 4 + 4 * tm * 1024 * 4 + (8 << 20)
    return pl.pallas_call(
        functools.partial(_ffn_kernel, hidden=hidden, chunks=_split_chunks(hidden, 4 * MXU_WIDTH), sub=sub),
        grid=(b, s // tm),
        in_specs=[x_spec, mod_spec, _const_spec(gains.shape), _layer_spec(w_in.shape, layer),
                  _layer_spec(w_out.shape, layer)],
        out_specs=x_spec,
        out_shape=jax.ShapeDtypeStruct(x.shape, F32),
        compiler_params=_params(2, vmem),
        name="ffn",
    )(x, mod, gains, w_in, w_out)


def _gmlp_kernel(x_ref, mod_ref, g_ref, win_ref, lng_ref, lnb_ref, wbd_ref, brow_ref, wout_ref,
                 *out_and_scratch, half, groups, pair, emit_v):
    if emit_v:
        y_ref, vn_ref, v_scr = out_and_scratch
    else:
        y_ref, v_scr = out_and_scratch
    gw = half // groups
    cw = pair * gw
    nblk = half // cw
    rows = x_ref.shape[0]
    sub = wbd_ref.shape[1]
    x = x_ref[...]
    h_all = _mod_in(x, g_ref[0:1, :], mod_ref, 0, 1).astype(BF16)
    for r0 in range(0, rows, sub):
        rs = slice(r0, r0 + sub)
        h = h_all[rs]
        s1 = None
        s2 = None
        for j in range(nblk):
            v = _gelu(_dot(h, win_ref[:, half + j * cw:half + (j + 1) * cw]))
            v_scr[rs, j * cw:(j + 1) * cw] = v
            a1 = jnp.sum(v, axis=-1, keepdims=True)
            a2 = jnp.sum(v * v, axis=-1, keepdims=True)
            s1 = a1 if s1 is None else s1 + a1
            s2 = a2 if s2 is None else s2 + a2
        mu = s1 * (1.0 / half)
        rstd = lax.rsqrt(s2 * (1.0 / half) - mu * mu + EPS)
        acc = None
        for j in range(nblk):
            cols = slice(j * cw, (j + 1) * cw)
            vn = (v_scr[rs, cols] - mu) * rstd * lng_ref[:, cols] + lnb_ref[:, cols]
            if emit_v:
                vn_ref[rs, cols] = vn
            vnb = vn.astype(BF16)
            u = _gelu(_dot(h, win_ref[:, cols]))
            mixed = []
            for gg in range(pair):
                g = j * pair + gg
                mixed.append(_dot(wbd_ref[g], vnb[:, gg * gw:(gg + 1) * gw]) + brow_ref[:, g:g + 1])
            out = (u * jnp.concatenate(mixed, axis=1)).astype(BF16)
            part = _dot(out, wout_ref[cols, :])
            acc = part if acc is None else acc + part
        gate = mod_ref[2] if mod_ref.shape[1] == 1 else mod_ref[2, rs, :]
        y_ref[rs, :] = _resid_out(x[rs], acc, g_ref[1:2, :], gate)


def _gmlp(x, mod, gains, w_in, ln_g, ln_b, w_s, b_s, w_out, tm, sub, t_chunk, emit_v):
    b, s, d = x.shape
    half = w_out.shape[0]
    groups = w_s.shape[0]
    tm = min(tm, s)
    sub = min(sub, tm)
    n_rep = sub // t_chunk
    pos = np.arange(sub)
    expand = jnp.asarray(pos[:, None] % t_chunk == np.arange(t_chunk)[None, :], w_s.dtype)
    same = jnp.asarray(pos[:, None] // t_chunk == pos[None, :] // t_chunk)
    wt = jnp.tril(w_s[:, :t_chunk, :t_chunk])
    w_bd = jnp.einsum("rt,gts,cs->grc", expand, wt, expand, precision=lax.Precision.HIGHEST)
    w_bd = jnp.where(same[None], w_bd, 0.0).astype(BF16)
    b_rows = jnp.tile(b_s[:, :t_chunk].T, (n_rep, 1))
    x_spec, mod_spec = _row_specs(x, mod, tm)
    out_shape = [jax.ShapeDtypeStruct(x.shape, F32)]
    out_specs = [x_spec]
    if emit_v:
        out_shape.append(jax.ShapeDtypeStruct((b, s, half), F32))
        out_specs.append(pl.BlockSpec((None, tm, half), lambda bb, i: (bb, i, 0)))
    vmem = (_nbytes(w_in, w_out, w_bd) + tm * half * 4 * (5 if emit_v else 1)
            + 6 * tm * d * 4 + (12 << 20))
    res = pl.pallas_call(
        functools.partial(_gmlp_kernel, half=half, groups=groups, pair=2, emit_v=emit_v),
        grid=(b, s // tm),
        in_specs=[x_spec, mod_spec, _const_spec(gains.shape), _const_spec(w_in.shape),
                  _const_spec((1, half)), _const_spec((1, half)), _const_spec(w_bd.shape),
                  _const_spec(b_rows.shape), _const_spec(w_out.shape)],
        out_specs=out_specs,
        out_shape=out_shape,
        scratch_shapes=[pltpu.VMEM((tm, half), F32)],
        compiler_params=_params(2, vmem),
        name="gmlp_v" if emit_v else "gmlp",
    )(x, mod, gains, w_in, ln_g.reshape(1, half), ln_b.reshape(1, half), w_bd, b_rows, w_out)
    return res if emit_v else res[0]


def _outproj_kernel(o_ref, x_ref, mod_ref, g_ref, w_ref, y_ref):
    y_ref[...] = _resid_out(x_ref[...], _dot(o_ref[...], w_ref[...]), g_ref[1:2, :], mod_ref[2])


def _outproj(o, x, mod, gains, w_out, tm):
    b, s, d = x.shape
    tm = min(tm, s)
    k = o.shape[-1]
    x_spec, mod_spec = _row_specs(x, mod, tm)
    return pl.pallas_call(
        _outproj_kernel,
        grid=(b, s // tm),
        in_specs=[pl.BlockSpec((None, tm, k), lambda bb, i: (bb, i, 0)), x_spec, mod_spec,
                  _const_spec(gains.shape), _const_spec(w_out.shape)],
        out_specs=x_spec,
        out_shape=jax.ShapeDtypeStruct(x.shape, F32),
        compiler_params=_params(2, _nbytes(w_out) + 8 * tm * d * 4 + 2 * tm * k * 2 + (8 << 20)),
        name="outproj",
    )(o, x, mod, gains, w_out)


def _max_head_norm2(x_t, heads):
    d, rows = x_t.shape
    n2 = jnp.sum((x_t * x_t).reshape(heads, d // heads, rows), axis=1)
    return jnp.full((8, LANES), jnp.max(n2), F32)


def _qkv_kernel(x_ref, mod_ref, g_ref, w_ref, q_ref, k_ref, v_ref, k16_ref, v16_ref, *norm_refs,
                scale, transposed):
    d = x_ref.shape[-1]
    h = _mod_in(x_ref[...], g_ref[0:1, :], mod_ref, 0, 1).astype(BF16)
    q = _dot(h, w_ref[:, :d]) * scale
    k = _dot(h, w_ref[:, d:2 * d])
    k_ref[...] = k
    k16_ref[...] = k.astype(BF16)
    v = _dot(h, w_ref[:, 2 * d:])
    v_ref[...] = v
    if transposed:
        qn_ref, kn_ref = norm_refs
        q_t = q.T
        q_ref[...] = q_t.astype(BF16)
        v16_ref[...] = v.T.astype(BF16)
        qn_ref[...] = _max_head_norm2(q_t, 2 * DIFF_HEADS)
        kn_ref[...] = _max_head_norm2(k.T, 2 * DIFF_HEADS)
    else:
        q_ref[...] = q.astype(BF16)
        v16_ref[...] = v.astype(BF16)


def _qkv(x, mod, gains, w_in, tm, scale, transposed):
    b, s, d = x.shape
    tm = min(tm, s)
    x_spec, mod_spec = _row_specs(x, mod, tm)
    sds = jax.ShapeDtypeStruct
    out_specs = [x_spec, x_spec, x_spec, x_spec, x_spec]
    out_shape = [sds(x.shape, BF16), sds(x.shape, F32), sds(x.shape, F32), sds(x.shape, BF16),
                 sds(x.shape, BF16)]
    if transposed:
        t_spec = pl.BlockSpec((None, d, tm), lambda bb, i: (bb, 0, i))
        n_spec = pl.BlockSpec((None, None, 8, LANES), lambda bb, i: (bb, i, 0, 0))
        out_specs[0] = out_specs[4] = t_spec
        out_shape[0] = out_shape[4] = sds((b, d, s), BF16)
        out_specs += [n_spec, n_spec]
        out_shape += [sds((b, s // tm, 8, LANES), F32)] * 2
    return pl.pallas_call(
        functools.partial(_qkv_kernel, scale=scale, transposed=transposed),
        grid=(b, s // tm),
        in_specs=[x_spec, mod_spec, _const_spec(gains.shape), _const_spec(w_in.shape)],
        out_specs=out_specs,
        out_shape=out_shape,
        compiler_params=_params(2, _nbytes(w_in) + 24 * tm * d * 4 + (8 << 20)),
        name="diff_qkv_t" if transposed else "diff_qkv",
    )(x, mod, gains, w_in)


def _diff_lambda(lam_ref, lam_init):
    lp = lam_ref[...]
    e1 = jnp.exp(jnp.sum(lp[0:1, :] * lp[1:2, :], axis=-1, keepdims=True))
    e2 = jnp.exp(jnp.sum(lp[2:3, :] * lp[3:4, :], axis=-1, keepdims=True))
    return e1 - e2 + lam_init


def _stack_q_halves(q, qs_scr, rows):
    lane = lax.broadcasted_iota(jnp.int32, q.shape, 1)
    first = (lane & (LANES - 1)) < (LANES // 2)
    zero = jnp.zeros_like(q)
    qs_scr[0:rows, :] = jnp.where(first, q, zero)
    qs_scr[rows:2 * rows, :] = jnp.where(first, zero, q)


def _flash_init(m_scr, l_scr, acc_scr):
    m_scr[...] = jnp.full(m_scr.shape, NEG_INF, F32)
    l_scr[...] = jnp.zeros(l_scr.shape, F32)
    acc_scr[...] = jnp.zeros(acc_scr.shape, F32)


def _flash_step(qs_scr, scores, v_blk, m_scr, l_scr, acc_scr, mask):
    sl = lambda h: slice(LANES * h, LANES * (h + 1))
    s_all = [scores(h, qs_scr[:, sl(h)]) for h in range(DIFF_HEADS)]
    stats = []
    for h, s in enumerate(s_all):
        if mask is not None:
            s = jnp.where(mask, s, NEG_INF)
        m_old = m_scr[h]
        m_new = jnp.maximum(m_old, jnp.max(s, axis=1, keepdims=True))
        alpha = jnp.exp(m_old - m_new)
        p = jnp.exp(s - m_new)
        l_scr[h] = alpha * l_scr[h] + jnp.sum(p, axis=1, keepdims=True)
        m_scr[h] = m_new
        stats.append((alpha, p.astype(BF16)))
    for h, (alpha, p) in enumerate(stats):
        acc_scr[:, sl(h)] = alpha * acc_scr[:, sl(h)] + _dot(p, v_blk(h))


def _flash_finish(l_scr, acc_scr, lam, subln, lam_init, rows):
    outs = []
    for h in range(DIFF_HEADS):
        sl = slice(LANES * h, LANES * (h + 1))
        l = l_scr[h]
        o = acc_scr[0:rows, sl] / l[0:rows] - lam * (acc_scr[rows:2 * rows, sl] / l[rows:2 * rows])
        outs.append((_rms_rows(o) * subln * (1.0 - lam_init)).astype(BF16))
    return jnp.concatenate(outs, axis=1)


def _chunk_mask(rows, cols, row_pos0, col_pos0):
    r = lax.broadcasted_iota(jnp.int32, (2 * rows, cols), 0)
    c = lax.broadcasted_iota(jnp.int32, (2 * rows, cols), 1)
    r = jnp.where(r >= rows, r - rows, r)
    return ((c + col_pos0) // CHUNK) <= ((r + row_pos0) // CHUNK)


def _flash_t_step(qz_scr, k_ref, vt_ref, m_scr, l_scr, acc_scr, mask, bounded=False):
    heads = [(h, c) for h in range(DIFF_HEADS) for c in range(2)]
    rows = lambda h: slice(LANES * h, LANES * (h + 1))

    def scores(i):
        h, c = heads[i]
        return _dot(k_ref[:, rows(h)], qz_scr[c, rows(h), :])

    def softmax(i, s):
        r = 2 * heads[i][0] + heads[i][1]
        if mask is not None:
            s = jnp.where(mask, s, NEG_INF)
        m_old = m_scr[r:r + 1, :]
        block_ref = 0.0 if bounded else jnp.max(s, axis=0, keepdims=True)
        m_new = jnp.maximum(m_old, block_ref)
        alpha = jnp.exp2(m_old - m_new)
        p = jnp.exp2(s - m_new)
        l_scr[r:r + 1, :] = alpha * l_scr[r:r + 1, :] + jnp.sum(p, axis=0, keepdims=True)
        m_scr[r:r + 1, :] = m_new
        return alpha, p.astype(BF16)

    def accumulate(i, alpha, p):
        h, c = heads[i]
        acc_scr[c, rows(h), :] = alpha * acc_scr[c, rows(h), :] + _dot(vt_ref[rows(h), :], p)

    n = len(heads)
    ahead, behind = 2, 1
    pending_s = {i: scores(i) for i in range(ahead)}
    pending_p = {}
    for i in range(n):
        if i + ahead < n:
            pending_s[i + ahead] = scores(i + ahead)
        pending_p[i] = softmax(i, pending_s.pop(i))
        if i - behind >= 0:
            accumulate(i - behind, *pending_p.pop(i - behind))
    for i in sorted(pending_p):
        accumulate(i, *pending_p[i])


def _flash_prompt_kernel(qt_ref, kt_ref, safe_ref, q_ref, k_ref, vt_ref, x_ref, mod_ref, g_ref, lam_ref,
                         subln_ref, woutt_ref, y_ref, qz_scr, m_scr, l_scr, acc_scr, *, lam_init):
    p = pl.program_id(1)
    qi = qt_ref[p]
    ki = kt_ref[p]
    bounded = safe_ref[pl.program_id(0) * pl.num_programs(1) + p] != 0
    tk = k_ref.shape[0]
    tq = q_ref.shape[1]

    @pl.when(ki == 0)
    def _():
        m_scr[...] = jnp.full(m_scr.shape, NEG_INF, F32)
        l_scr[...] = jnp.zeros(l_scr.shape, F32)
        acc_scr[...] = jnp.zeros(acc_scr.shape, F32)
        q = q_ref[...]
        feat = lax.broadcasted_iota(jnp.int32, q.shape, 0)
        first = (feat & (LANES - 1)) < (LANES // 2)
        zero = jnp.zeros_like(q)
        qz_scr[0] = jnp.where(first, q, zero)
        qz_scr[1] = jnp.where(first, zero, q)

    @pl.when((ki < qi) & bounded)
    def _():
        _flash_t_step(qz_scr, k_ref, vt_ref, m_scr, l_scr, acc_scr, None, bounded=True)

    @pl.when((ki < qi) & jnp.logical_not(bounded))
    def _():
        _flash_t_step(qz_scr, k_ref, vt_ref, m_scr, l_scr, acc_scr, None)

    def chunk_causal():
        key = lax.broadcasted_iota(jnp.int32, (tk, tq), 0)
        qry = lax.broadcasted_iota(jnp.int32, (tk, tq), 1)
        return (key // CHUNK) <= (qry // CHUNK)

    @pl.when((ki == qi) & bounded)
    def _():
        _flash_t_step(qz_scr, k_ref, vt_ref, m_scr, l_scr, acc_scr, chunk_causal(), bounded=True)

    @pl.when((ki == qi) & jnp.logical_not(bounded))
    def _():
        _flash_t_step(qz_scr, k_ref, vt_ref, m_scr, l_scr, acc_scr, chunk_causal())

    @pl.when(ki == qi)
    def _():
        lam = _diff_lambda(lam_ref, lam_init)
        outs = []
        for h in range(DIFF_HEADS):
            rows = slice(LANES * h, LANES * (h + 1))
            o = (acc_scr[0, rows, :] / l_scr[2 * h:2 * h + 1, :]
                 - lam * (acc_scr[1, rows, :] / l_scr[2 * h + 1:2 * h + 2, :]))
            o = o * lax.rsqrt(jnp.mean(o * o, axis=0, keepdims=True) + EPS)
            outs.append((o * (subln_ref[...] * (1.0 - lam_init))).astype(BF16))
        out_t = _dot(woutt_ref[...], jnp.concatenate(outs, axis=0))
        y_ref[...] = _resid_out(x_ref[...], out_t.T, g_ref[1:2, :], mod_ref[2])


def _flash_prompt(q_t, k16, v_t, qn2, kn2, x, mod, gains, lam_p, subln, w_out, lam_init, tq):
    b, s, d = x.shape
    tq = min(tq, s)
    nq = s // tq
    assert qn2.shape == kn2.shape == (b, nq)
    pairs = [(qi, ki) for qi in range(nq) for ki in range(qi + 1)]
    qt = jnp.asarray([pq for pq, _ in pairs], jnp.int32)
    kt = jnp.asarray([pk for _, pk in pairs], jnp.int32)
    safe = (qn2[:, qt] * kn2[:, kt] <= SOFTMAX_SAFE_LOG2 ** 2).astype(jnp.int32).reshape(-1)
    x_spec = pl.BlockSpec((None, tq, d), lambda bb, p, qt_, kt_, safe_: (bb, qt_[p], 0))
    q_spec = pl.BlockSpec((None, d, tq), lambda bb, p, qt_, kt_, safe_: (bb, 0, qt_[p]))
    k_spec = pl.BlockSpec((None, tq, d), lambda bb, p, qt_, kt_, safe_: (bb, kt_[p], 0))
    v_spec = pl.BlockSpec((None, d, tq), lambda bb, p, qt_, kt_, safe_: (bb, 0, kt_[p]))
    mod_spec = pl.BlockSpec((None, 6, 1, d), lambda bb, p, qt_, kt_, safe_: (bb, 0, 0, 0))
    w_out_t = w_out.T
    vmem = (_nbytes(w_out) + 6 * tq * d * 2 + 4 * tq * d * 4 + 2 * tq * d * 2 + 2 * tq * d * 4
            + 8 * tq * tq * 4 + (8 << 20))
    grid_spec = pltpu.PrefetchScalarGridSpec(
        num_scalar_prefetch=3,
        grid=(b, len(pairs)),
        in_specs=[q_spec, k_spec, v_spec, x_spec, mod_spec, _const_spec(gains.shape),
                  _const_spec(lam_p.shape), _const_spec((LANES, 1)), _const_spec(w_out_t.shape)],
        out_specs=x_spec,
        scratch_shapes=[pltpu.VMEM((2, d, tq), BF16),
                        pltpu.VMEM((2 * DIFF_HEADS, tq), F32),
                        pltpu.VMEM((2 * DIFF_HEADS, tq), F32),
                        pltpu.VMEM((2, d, tq), F32)])
    return pl.pallas_call(
        functools.partial(_flash_prompt_kernel, lam_init=lam_init),
        grid_spec=grid_spec,
        out_shape=jax.ShapeDtypeStruct(x.shape, F32),
        compiler_params=_params(2, vmem),
        name="diff_flash_prompt",
    )(qt, kt, safe, q_t, k16, v_t, x, mod, gains, lam_p, subln.reshape(LANES, 1), w_out_t)


def _flash_sample_kernel(q_ref, ck_ref, cv_ref, kn_ref, vn_ref, lam_ref, subln_ref, o_ref,
                         qs_scr, kpad_scr, vpad_scr, m_scr, l_scr, acc_scr,
                         *, rows, past, lam_init, new_mask_needed):
    kb = pl.program_id(1)

    @pl.when(kb == 0)
    def _():
        _flash_init(m_scr, l_scr, acc_scr)
        _stack_q_halves(q_ref[...], qs_scr, rows)
        kpad_scr[...] = jnp.zeros(kpad_scr.shape, BF16)
        vpad_scr[...] = jnp.zeros(vpad_scr.shape, BF16)
        kpad_scr[0:rows, :] = kn_ref[...]
        vpad_scr[0:rows, :] = vn_ref[...]
        c = lax.broadcasted_iota(jnp.int32, (2 * rows, LANES), 1)
        mask = c < rows
        if new_mask_needed:
            mask = mask & _chunk_mask(rows, LANES, past, past)
        _flash_step(qs_scr,
                    lambda h, qp: _dot_nt(qp, kpad_scr[:, LANES * h:LANES * (h + 1)]),
                    lambda h: vpad_scr[:, LANES * h:LANES * (h + 1)],
                    m_scr, l_scr, acc_scr, mask)

    tk = ck_ref.shape[1]
    _flash_step(qs_scr,
                lambda h, qp: _dot(qp, ck_ref[LANES * h:LANES * (h + 1), :].astype(BF16)),
                lambda h: cv_ref[pl.ds(h, tk, stride=DIFF_HEADS), :].astype(BF16),
                m_scr, l_scr, acc_scr, None)

    @pl.when(kb == pl.num_programs(1) - 1)
    def _():
        lam = _diff_lambda(lam_ref, lam_init)
        o_ref[...] = _flash_finish(l_scr, acc_scr, lam, subln_ref[...], lam_init, rows)


def _flash_sample(q, cache_k_t, cache_v, k_new, v_new, lam_p, subln, lam_init, rows, tk):
    bs, d, past = cache_k_t.shape
    tk = min(tk, past)
    pos = past + np.arange(rows)
    new_mask_needed = not bool(np.all((pos[None, :] // CHUNK) <= (pos[:, None] // CHUNK)))
    row_spec = pl.BlockSpec((rows, d), lambda b, kb: (b, 0))
    k_spec = pl.BlockSpec((None, d, tk), lambda b, kb: (b, 0, kb))
    v_spec = pl.BlockSpec((None, tk * DIFF_HEADS, LANES), lambda b, kb: (b, kb, 0))
    vmem = 4 * tk * d * 4 + 4 * tk * d * 2 + (12 << 20)
    return pl.pallas_call(
        functools.partial(_flash_sample_kernel, rows=rows, past=past, lam_init=lam_init,
                          new_mask_needed=new_mask_needed),
        grid=(bs, past // tk),
        in_specs=[row_spec, k_spec, v_spec, row_spec, row_spec,
                  _const_spec(lam_p.shape), _const_spec((1, LANES))],
        out_specs=row_spec,
        out_shape=jax.ShapeDtypeStruct(q.shape, BF16),
        scratch_shapes=[pltpu.VMEM((2 * rows, d), BF16),
                        pltpu.VMEM((LANES, d), BF16), pltpu.VMEM((LANES, d), BF16),
                        pltpu.VMEM((DIFF_HEADS, 2 * rows, 1), F32),
                        pltpu.VMEM((DIFF_HEADS, 2 * rows, 1), F32),
                        pltpu.VMEM((2 * rows, d), F32)],
        compiler_params=_params(2, vmem),
        name="diff_flash_sample",
    )(q, cache_k_t, cache_v, k_new, v_new, lam_p, subln.reshape(1, LANES))


def _ret_log_gamma(h):
    return float(np.log(np.float32(1.0) - np.float32(2.0) ** np.float32(-5.0 - h)))


def _rotate_pairs(x):
    n = x.shape[-1]
    lane = lax.broadcasted_iota(jnp.int32, x.shape, 1)
    return jnp.where((lane & 1) == 0, -pltpu.roll(x, n - 1, 1), pltpu.roll(x, 1, 1))


def _ret_project(h, win_ref, cos, sin, d, dk):
    cos4 = jnp.concatenate([cos] * (d // dk), axis=1)
    sin4 = jnp.concatenate([sin] * (d // dk), axis=1)
    q = _dot(h, win_ref[:, 0:d])
    q = q * cos4 + _rotate_pairs(q) * sin4
    k = _dot(h, win_ref[:, d:2 * d])
    k = (k * cos4 + _rotate_pairs(k) * sin4) * (dk ** -0.5)
    v = _dot(h, win_ref[:, 2 * d:4 * d])
    return q, k, v


def _ret_gate(h, win_ref, d):
    return _silu(_dot(h, win_ref[:, 4 * d:6 * d]))


def _ret_decay(lg, rows, same_seq=None):
    t = lax.broadcasted_iota(jnp.int32, (rows, rows), 0)
    s = lax.broadcasted_iota(jnp.int32, (rows, rows), 1)
    ok = t >= s
    if same_seq is not None:
        ok = ok & ((t // same_seq) == (s // same_seq))
    diff = jnp.maximum(t - s, 0).astype(F32)
    return jnp.where(ok, jnp.exp(lg * diff), 0.0)


def _ret_prompt_kernel(x_ref, mod_ref, g_ref, win_ref, cos_ref, sin_ref, wout_ref, y_ref, st_ref,
                       state_scr, o_scr, *, heads):
    i = pl.program_id(1)
    rows, d = x_ref.shape
    dk = d // heads
    dv = 2 * dk

    @pl.when(i == 0)
    def _():
        state_scr[...] = jnp.zeros(state_scr.shape, F32)

    x = x_ref[...]
    h = _mod_in(x, g_ref[0:1, :], mod_ref, 0, 1).astype(BF16)
    sub = min(rows, SUB_ROWS)
    t = lax.broadcasted_iota(jnp.int32, (sub, 1), 0).astype(F32)
    lgs = [_ret_log_gamma(hh) for hh in range(heads)]
    states = [state_scr[hh] for hh in range(heads)]
    for r0 in range(0, rows, sub):
        hs = h[r0:r0 + sub]
        q, k, v = _ret_project(hs, win_ref, cos_ref[r0:r0 + sub, :], sin_ref[r0:r0 + sub, :], d, dk)
        sg = _ret_gate(hs, win_ref, d)
        qs = [q[:, hh * dk:(hh + 1) * dk].astype(BF16) for hh in range(heads)]
        ks = [k[:, hh * dk:(hh + 1) * dk] for hh in range(heads)]
        vs = [v[:, hh * dv:(hh + 1) * dv].astype(BF16) for hh in range(heads)]
        scores = [_dot_nt(qs[hh], ks[hh].astype(BF16)) for hh in range(heads)]
        cross = [_dot(qs[hh], states[hh].astype(BF16)) for hh in range(heads)]
        kv = [_dot_tn((ks[hh] * jnp.exp(lgs[hh] * (sub - 1.0 - t))).astype(BF16), vs[hh])
              for hh in range(heads)]
        states = [math.exp(lgs[hh] * sub) * states[hh] + kv[hh] for hh in range(heads)]
        inner = [_dot((scores[hh] * _ret_decay(lgs[hh], sub)).astype(BF16), vs[hh]) for hh in range(heads)]
        gated = []
        for hh in range(heads):
            o = inner[hh] + cross[hh] * jnp.exp(lgs[hh] * (t + 1.0))
            gated.append((_rms_rows(o) * sg[:, hh * dv:(hh + 1) * dv]).astype(BF16))
        o_scr[r0:r0 + sub, :] = jnp.concatenate(gated, axis=1)
    for hh in range(heads):
        state_scr[hh] = states[hh]
    y_ref[...] = _resid_out(x, _dot(o_scr[...], wout_ref[...]), g_ref[1:2, :], mod_ref[2])

    @pl.when(i == pl.num_programs(1) - 1)
    def _():
        st_ref[...] = state_scr[...]


def _xpos_tables(pos, dk):
    inv = 1.0 / (10000.0 ** jnp.linspace(0.0, 1.0, dk // 2, dtype=F32))
    ang = pos.astype(F32)[:, None] * jnp.repeat(inv, 2)[None, :]
    return jnp.cos(ang), jnp.sin(ang)


def _ret_prompt(x, mod, gains, w_in, w_out, tm):
    b, s, d = x.shape
    tm = min(tm, s)
    heads = RET_HEADS
    dk = d // heads
    dv = 2 * dk
    cos, sin = _xpos_tables(jnp.arange(s), dk)
    x_spec, mod_spec = _row_specs(x, mod, tm)
    tab_spec = pl.BlockSpec((tm, dk), lambda bb, i: (i, 0))
    st_spec = pl.BlockSpec((None, heads, dk, dv), lambda bb, i: (bb, 0, 0, 0))
    vmem = _nbytes(w_in, w_out) + 3 * heads * dk * dv * 4 + 40 * tm * d * 4 + (8 << 20)
    return pl.pallas_call(
        functools.partial(_ret_prompt_kernel, heads=heads),
        grid=(b, s // tm),
        in_specs=[x_spec, mod_spec, _const_spec(gains.shape), _const_spec(w_in.shape),
                  tab_spec, tab_spec, _const_spec(w_out.shape)],
        out_specs=[x_spec, st_spec],
        out_shape=[jax.ShapeDtypeStruct(x.shape, F32),
                   jax.ShapeDtypeStruct((b, heads, dk, dv), F32)],
        scratch_shapes=[pltpu.VMEM((heads, dk, dv), F32), pltpu.VMEM((tm, heads * dv), BF16)],
        compiler_params=_params(2, vmem),
        name="ret_prompt",
    )(x, mod, gains, w_in, cos, sin, w_out)


def _ret_sample_kernel(x_ref, mod_ref, g_ref, win_ref, cos_ref, sin_ref, st_in_ref, wout_ref,
                       y_ref, st_out_ref, q_scr, k_scr, v_scr, o_scr, *, heads, rows):
    b = pl.program_id(0)
    total, d = x_ref.shape
    dk = d // heads
    dv = 2 * dk

    @pl.when(b == 0)
    def _():
        h = _mod_in(x_ref[...], g_ref[0:1, :], mod_ref, 0, 1).astype(BF16)
        q, k, v = _ret_project(h, win_ref, cos_ref[...], sin_ref[...], d, dk)
        q_scr[...] = q.astype(BF16)
        k_scr[...] = k
        v_scr[...] = v.astype(BF16)
        for hh in range(heads):
            scores = (_dot_nt(q[:, hh * dk:(hh + 1) * dk].astype(BF16),
                              k[:, hh * dk:(hh + 1) * dk].astype(BF16))
                      * _ret_decay(_ret_log_gamma(hh), total, same_seq=rows))
            o_scr[:, hh * dv:(hh + 1) * dv] = _dot(scores.astype(BF16),
                                                   v[:, hh * dv:(hh + 1) * dv].astype(BF16))

    r0 = pl.multiple_of(b * rows, rows)
    t = lax.broadcasted_iota(jnp.int32, (rows, 1), 0).astype(F32)
    ta = lax.broadcasted_iota(jnp.int32, (total, 1), 0)
    mine = (ta >= r0) & (ta < r0 + rows)
    t_all = (ta - r0).astype(F32)
    for hh in range(heads):
        lg = _ret_log_gamma(hh)
        state = st_in_ref[hh]
        qh = q_scr[pl.ds(r0, rows), hh * dk:(hh + 1) * dk]
        cross = _dot(qh, state.astype(BF16)) * jnp.exp(lg * (t + 1.0))
        o_scr[pl.ds(r0, rows), hh * dv:(hh + 1) * dv] += cross
        k_dec = jnp.where(mine, k_scr[:, hh * dk:(hh + 1) * dk] * jnp.exp(lg * (rows - 1.0 - t_all)), 0.0)
        st_out_ref[hh] = (math.exp(lg * rows) * state
                          + _dot_tn(k_dec.astype(BF16), v_scr[:, hh * dv:(hh + 1) * dv]))

    @pl.when(b == pl.num_programs(0) - 1)
    def _():
        h = _mod_in(x_ref[...], g_ref[0:1, :], mod_ref, 0, 1).astype(BF16)
        gated = []
        for hh in range(heads):
            sl = slice(hh * dv, (hh + 1) * dv)
            sg = _silu(_dot(h, win_ref[:, 4 * d + hh * dv:4 * d + (hh + 1) * dv]))
            gated.append((_rms_rows(o_scr[:, sl]) * sg).astype(BF16))
        oc = jnp.concatenate(gated, axis=1)
        y_ref[...] = _resid_out(x_ref[...], _dot(oc, wout_ref[...]), g_ref[1:2, :], mod_ref[2])


def _ret_sample(x, mod, gains, w_in, w_out, state, rows, past):
    total, d = x.shape
    bs, heads, dk, dv = state.shape
    cos, sin = _xpos_tables(past + jnp.arange(rows), dk)
    cos = jnp.tile(cos, (bs, 1))
    sin = jnp.tile(sin, (bs, 1))
    st_spec = pl.BlockSpec((None, heads, dk, dv), lambda b: (b, 0, 0, 0))
    vmem = _nbytes(w_in, w_out, x, x, mod) + 4 * heads * dk * dv * 4 + 60 * total * d * 4 + (8 << 20)
    return pl.pallas_call(
        functools.partial(_ret_sample_kernel, heads=heads, rows=rows),
        grid=(bs,),
        in_specs=[_const_spec(x.shape), _const_spec(mod.shape), _const_spec(gains.shape),
                  _const_spec(w_in.shape), _const_spec(cos.shape), _const_spec(sin.shape),
                  st_spec, _const_spec(w_out.shape)],
        out_specs=[pl.BlockSpec(x.shape, lambda b: (0, 0)), st_spec],
        out_shape=[jax.ShapeDtypeStruct(x.shape, F32), jax.ShapeDtypeStruct(state.shape, F32)],
        scratch_shapes=[pltpu.VMEM((total, d), BF16), pltpu.VMEM((total, d), F32),
                        pltpu.VMEM((total, 2 * d), BF16), pltpu.VMEM((total, 2 * d), F32)],
        compiler_params=_params(1, vmem),
        name="ret_sample",
    )(x, mod, gains, w_in, cos, sin, state, w_out)


def _hgrn_lower(lb_ref, layer):
    lb = lb_ref[...]
    e = jnp.exp(lb - jnp.max(lb, axis=0, keepdims=True))
    p = e / jnp.sum(e, axis=0, keepdims=True)
    if layer == 0:
        return jnp.zeros_like(p[0:1, :])
    return jnp.sum(p[1:layer + 1, :], axis=0, keepdims=True)


def _block_cumsum(x, block):
    rows = x.shape[0]
    t = lax.broadcasted_iota(jnp.int32, (rows, rows), 0)
    s = lax.broadcasted_iota(jnp.int32, (rows, rows), 1)
    tri = jnp.where((s <= t) & ((t // block) == (s // block)), 1.0, 0.0).astype(BF16)
    hi = x.astype(BF16)
    lo = (x - hi.astype(F32)).astype(BF16)
    return _dot(tri, hi) + _dot(tri, lo)


def _block_last(x, block):
    rows = x.shape[0]
    parts = [jnp.broadcast_to(x[c * block + block - 1:c * block + block, :], (block, x.shape[1]))
             for c in range(rows // block)]
    return parts[0] if len(parts) == 1 else jnp.concatenate(parts, axis=0)


def _hgrn_project(h, win_ref, lower, d, block):
    q = _silu(_dot(h, win_ref[:, 0:d]))
    f = _dot(h, win_ref[:, d:2 * d])
    v = _dot(h, win_ref[:, 2 * d:3 * d])
    sg = _silu(_dot(h, win_ref[:, 3 * d:4 * d]))
    sig = jax.nn.sigmoid(f)
    forget = lower + (1.0 - lower) * sig
    k = (1.0 - lower) * (1.0 - sig)
    b = _block_cumsum(jnp.log(forget), block)
    b_last = _block_last(b, block)
    return q * jnp.exp(b), k * jnp.exp(-b), k * jnp.exp(b_last - b), v, sg, jnp.exp(b_last)


def _hgrn_prompt_kernel(x_ref, mod_ref, g_ref, win_ref, lb_ref, ng_ref, wout_ref, y_ref, st_ref,
                        state_scr, o_scr, *, layer):
    i = pl.program_id(1)
    rows, d = x_ref.shape
    heads = d // HG_WIDTH
    w = HG_WIDTH

    @pl.when(i == 0)
    def _():
        state_scr[...] = jnp.zeros(state_scr.shape, F32)

    x = x_ref[...]
    h = _mod_in(x, g_ref[0:1, :], mod_ref, 0, 1).astype(BF16)
    lower = _hgrn_lower(lb_ref, layer)
    t = lax.broadcasted_iota(jnp.int32, (CHUNK, CHUNK), 0)
    s = lax.broadcasted_iota(jnp.int32, (CHUNK, CHUNK), 1)
    causal = t >= s
    ch = lambda hh: slice(hh * w, (hh + 1) * w)
    sub = min(rows, SUB_ROWS)
    n_chunks = sub // CHUNK
    cells = [(c, hh) for c in range(n_chunks) for hh in range(heads)]
    rc = lambda c: slice(c * CHUNK, (c + 1) * CHUNK)
    states = [state_scr[hh] for hh in range(heads)]
    for r0 in range(0, rows, sub):
        q_dec, k_inv, k_end, v, sg, e_last = _hgrn_project(h[r0:r0 + sub], win_ref, lower, d, CHUNK)
        qd16, ki16, ke16, v16 = (a.astype(BF16) for a in (q_dec, k_inv, k_end, v))
        kv = {(c, hh): _dot_tn(v16[rc(c), ch(hh)], ke16[rc(c), ch(hh)]) for c, hh in cells}
        scores = {(c, hh): jnp.where(causal, _dot_nt(qd16[rc(c), ch(hh)], ki16[rc(c), ch(hh)]),
                                     0.0).astype(BF16) for c, hh in cells}
        state_in = {}
        for hh in range(heads):
            for c in range(n_chunks):
                state_in[c, hh] = states[hh].astype(BF16)
                states[hh] = e_last[c * CHUNK:c * CHUNK + 1, ch(hh)] * states[hh] + kv[c, hh]
        o = [jnp.concatenate([_dot(scores[c, hh], v16[rc(c), ch(hh)])
                              + _dot_nt(qd16[rc(c), ch(hh)], state_in[c, hh]) for c in range(n_chunks)],
                             axis=0) for hh in range(heads)]
        o_scr[r0:r0 + sub, :] = jnp.concatenate(
            [(_rms_rows(o[hh]) * ng_ref[:, ch(hh)] * sg[:, ch(hh)]).astype(BF16) for hh in range(heads)], axis=1)
    for hh in range(heads):
        state_scr[hh] = states[hh]
    y_ref[...] = _resid_out(x, _dot(o_scr[...], wout_ref[...]), g_ref[1:2, :], mod_ref[2])

    @pl.when(i == pl.num_programs(1) - 1)
    def _():
        for hh in range(heads):
            st_ref[hh] = state_scr[hh].T


def _hgrn_prompt(x, mod, gains, w_in, norm_g, w_out, lower_bounds, layer, tm):
    b, s, d = x.shape
    tm = min(tm, s)
    heads = d // HG_WIDTH
    x_spec, mod_spec = _row_specs(x, mod, tm)
    st_spec = pl.BlockSpec((None, heads, HG_WIDTH, HG_WIDTH), lambda bb, i: (bb, 0, 0, 0))
    vmem = _nbytes(w_in, w_out) + 40 * tm * d * 4 + (8 << 20)
    return pl.pallas_call(
        functools.partial(_hgrn_prompt_kernel, layer=layer),
        grid=(b, s // tm),
        in_specs=[x_spec, mod_spec, _const_spec(gains.shape), _const_spec(w_in.shape),
                  _const_spec(lower_bounds.shape), _const_spec((1, d)), _const_spec(w_out.shape)],
        out_specs=[x_spec, st_spec],
        out_shape=[jax.ShapeDtypeStruct(x.shape, F32),
                   jax.ShapeDtypeStruct((b, heads, HG_WIDTH, HG_WIDTH), F32)],
        scratch_shapes=[pltpu.VMEM((heads, HG_WIDTH, HG_WIDTH), F32), pltpu.VMEM((tm, d), BF16)],
        compiler_params=_params(2, vmem),
        name="hgrn_prompt",
    )(x, mod, gains, w_in, lower_bounds, norm_g.reshape(1, d), w_out)


def _hgrn_sample_kernel(x_ref, mod_ref, g_ref, win_ref, lb_ref, ng_ref, st_in_ref, wout_ref,
                        y_ref, st_out_ref, qd_scr, ke_scr, v_scr, sg_scr, el_scr, o_scr,
                        *, layer, rows):
    b = pl.program_id(0)
    total, d = x_ref.shape
    heads = d // HG_WIDTH
    w = HG_WIDTH

    @pl.when(b == 0)
    def _():
        h = _mod_in(x_ref[...], g_ref[0:1, :], mod_ref, 0, 1).astype(BF16)
        q_dec, k_inv, k_end, v, sg, e_last = _hgrn_project(h, win_ref, _hgrn_lower(lb_ref, layer), d, rows)
        qd_scr[...] = q_dec
        ke_scr[...] = k_end
        v_scr[...] = v
        sg_scr[...] = sg
        el_scr[...] = e_last
        t = lax.broadcasted_iota(jnp.int32, (total, total), 0)
        s = lax.broadcasted_iota(jnp.int32, (total, total), 1)
        ok = (t >= s) & ((t // rows) == (s // rows))
        for hh in range(heads):
            ch = slice(hh * w, (hh + 1) * w)
            scores = jnp.where(ok, _dot_nt(q_dec[:, ch].astype(BF16), k_inv[:, ch].astype(BF16)), 0.0)
            o_scr[:, ch] = _dot(scores.astype(BF16), v[:, ch].astype(BF16))

    r0 = pl.multiple_of(b * rows, rows)
    ta = lax.broadcasted_iota(jnp.int32, (total, 1), 0)
    mine = (ta >= r0) & (ta < r0 + rows)
    for hh in range(heads):
        ch = slice(hh * w, (hh + 1) * w)
        state = st_in_ref[hh]
        qd = qd_scr[pl.ds(r0, rows), ch].astype(BF16)
        o_scr[pl.ds(r0, rows), ch] += _dot(qd, state.astype(BF16))
        k_end = jnp.where(mine, ke_scr[:, ch], 0.0).astype(BF16)
        new_t = (el_scr[pl.ds(r0, 1), ch] * state.T
                 + _dot_tn(v_scr[:, ch].astype(BF16), k_end))
        st_out_ref[hh] = new_t.T

    @pl.when(b == pl.num_programs(0) - 1)
    def _():
        outs = []
        for hh in range(heads):
            ch = slice(hh * w, (hh + 1) * w)
            outs.append((_rms_rows(o_scr[:, ch]) * ng_ref[:, ch] * sg_scr[:, ch]).astype(BF16))
        oc = jnp.concatenate(outs, axis=1)
        y_ref[...] = _resid_out(x_ref[...], _dot(oc, wout_ref[...]), g_ref[1:2, :], mod_ref[2])


def _hgrn_sample(x, mod, gains, w_in, norm_g, w_out, lower_bounds, layer, state, rows):
    total, d = x.shape
    bs, heads, dk, dv = state.shape
    st_spec = pl.BlockSpec((None, heads, dk, dv), lambda b: (b, 0, 0, 0))
    scr = pltpu.VMEM((total, d), F32)
    vmem = _nbytes(w_in, w_out, x, x, mod) + 60 * total * d * 4 + (8 << 20)
    return pl.pallas_call(
        functools.partial(_hgrn_sample_kernel, layer=layer, rows=rows),
        grid=(bs,),
        in_specs=[_const_spec(x.shape), _const_spec(mod.shape), _const_spec(gains.shape),
                  _const_spec(w_in.shape), _const_spec(lower_bounds.shape), _const_spec((1, d)),
                  st_spec, _const_spec(w_out.shape)],
        out_specs=[pl.BlockSpec(x.shape, lambda b: (0, 0)), st_spec],
        out_shape=[jax.ShapeDtypeStruct(x.shape, F32), jax.ShapeDtypeStruct(state.shape, F32)],
        scratch_shapes=[scr, scr, scr, scr, scr, scr],
        compiler_params=_params(1, vmem),
        name="hgrn_sample",
    )(x, mod, gains, w_in, lower_bounds, norm_g.reshape(1, d), state, w_out)


def kernel(x_prompt, x_sample, cache_k_diff, cache_v_diff, state_retention, state_hgrn, c_prompt, c_sample, w_ada, b_ada, norm_gains, gmlp_w_in, gmlp_ln_g, gmlp_ln_b, gmlp_w_s, gmlp_b_s, gmlp_w_out, diff_w_in, diff_lambda, diff_subln, diff_w_out, ret_w_in, ret_w_out, hgrn_w_in, hgrn_norm, hgrn_w_out, hgrn_lower_bounds, ffn_w_in, ffn_w_out):
    bp, s, d = x_prompt.shape
    bs, ls, _ = x_sample.shape
    ms = bs * ls
    depth = w_ada.shape[0]
    past = cache_k_diff.shape[2]
    n_mix = 4

    m_all = _ada(jnp.concatenate([c_prompt, c_sample], axis=0), w_ada, b_ada)
    ffn_in16, ffn_out16 = ffn_w_in.astype(BF16), ffn_w_out.astype(BF16)
    yp = x_prompt
    ys = x_sample.reshape(1, ms, d)
    outs = {name: [] for name in ("gv", "kp", "vp", "ks", "vs", "rp", "rs", "hp", "hs")}
    for i in range(depth):
        kind, j = i % n_mix, i // n_mix
        m = m_all[i].reshape(bp + bs, 6, d)
        mod_p = m[:bp].reshape(bp, 6, 1, d)
        mod_s = jnp.repeat(m[bp:], ls, axis=0).transpose(1, 0, 2).reshape(1, 6, ms, d)
        gains = norm_gains[i]
        if kind == 0:
            w_in, w_out = gmlp_w_in[j].astype(BF16), gmlp_w_out[j].astype(BF16)
            args = (gmlp_ln_g[j], gmlp_ln_b[j], gmlp_w_s[j], gmlp_b_s[j], w_out)
            yp = _gmlp(yp, mod_p, gains, w_in, *args, tm=512, sub=SUB_ROWS, t_chunk=GMLP_CHUNK, emit_v=False)
            ys, v_rows = _gmlp(ys, mod_s, gains, w_in, *args, tm=ms, sub=ms, t_chunk=ls, emit_v=True)
            outs["gv"].append(v_rows.reshape(bs, ls, -1))
        elif kind == 1:
            lam_init = 0.8 - 0.6 * math.exp(-0.3 * i)
            w_in, w_out = diff_w_in[j].astype(BF16), diff_w_out[j].astype(BF16)
            hk, hv = 2 * DIFF_HEADS, DIFF_HEADS
            scale = (d // hk) ** -0.5
            q_t, k, v, k16, v_t, qn2, kn2 = _qkv(yp, mod_p, gains, w_in, tm=512,
                                                 scale=scale * math.log2(math.e), transposed=True)
            outs["kp"].append(k.reshape(bp, s, hk, d // hk))
            outs["vp"].append(v.reshape(bp, s, hv, d // hv))
            yp = _flash_prompt(q_t, k16, v_t, qn2[:, :, 0, 0], kn2[:, :, 0, 0], yp, mod_p, gains,
                               diff_lambda[j], diff_subln[j], w_out, lam_init, tq=512)
            q, k, v, k16, v16 = _qkv(ys, mod_s, gains, w_in, tm=512, scale=scale, transposed=False)
            outs["ks"].append(k.reshape(bs, ls, hk, d // hk))
            outs["vs"].append(v.reshape(bs, ls, hv, d // hv))
            cache_k_t = cache_k_diff[j].transpose(0, 2, 3, 1).reshape(bs, d, past)
            cache_v = cache_v_diff[j].reshape(bs, past * hv, d // hv)
            oc = _flash_sample(q[0], cache_k_t, cache_v, k16[0], v16[0], diff_lambda[j], diff_subln[j],
                               lam_init, rows=ls, tk=1024)
            ys = _outproj(oc[None], ys, mod_s, gains, w_out, tm=512)
        elif kind == 2:
            w_in, w_out = ret_w_in[j].astype(BF16), ret_w_out[j].astype(BF16)
            yp, st = _ret_prompt(yp, mod_p, gains, w_in, w_out, tm=512)
            outs["rp"].append(st)
            y2, st = _ret_sample(ys[0], mod_s[0, :3], gains, w_in, w_out, state_retention[j], rows=ls, past=past)
            ys = y2[None]
            outs["rs"].append(st)
        else:
            w_in, w_out = hgrn_w_in[j].astype(BF16), hgrn_w_out[j].astype(BF16)
            yp, st = _hgrn_prompt(yp, mod_p, gains, w_in, hgrn_norm[j], w_out, hgrn_lower_bounds, i, tm=512)
            outs["hp"].append(st)
            y2, st = _hgrn_sample(ys[0], mod_s[0, :3], gains, w_in, hgrn_norm[j], w_out, hgrn_lower_bounds, i,
                                  state_hgrn[j], rows=ls)
            ys = y2[None]
            outs["hs"].append(st)
        yp = _ffn(yp, mod_p, gains, ffn_in16, ffn_out16, i, tm=1024, sub=SUB_ROWS)
        ys = _ffn(ys, mod_s, gains, ffn_in16, ffn_out16, i, tm=512, sub=SUB_ROWS)

    return (yp, ys.reshape(bs, ls, d), jnp.stack(outs["gv"]), jnp.stack(outs["kp"]), jnp.stack(outs["vp"]),
            jnp.stack(outs["ks"]), jnp.stack(outs["vs"]), jnp.stack(outs["rp"]), jnp.stack(outs["rs"]),
            jnp.stack(outs["hp"]), jnp.stack(outs["hs"]))
```

```python
import functools
import math

import numpy as np
import jax
import jax.numpy as jnp
from jax import lax
from jax.experimental import pallas as pl
from jax.experimental.pallas import tpu as pltpu

F32 = jnp.float32
BF16 = jnp.bfloat16
EPS = 1e-6
NEG_INF = -1e30

CHUNK = 64
GMLP_CHUNK = 128
GMLP_GROUPS = 8
DIFF_HEADS = 8
RET_HEADS = 4
HG_WIDTH = 128
LANES = 128
MXU_WIDTH = 256
SUB_ROWS = 256
VMEM_CAP = 60 << 20
SOFTMAX_SAFE_LOG2 = 96.0

_NT = (((1,), (1,)), ((), ()))
_TN = (((0,), (0,)), ((), ()))


def _dot(a, b):
    return jnp.dot(a, b, preferred_element_type=F32)


def _dot_nt(a, b):
    return lax.dot_general(a, b, _NT, preferred_element_type=F32)


def _dot_tn(a, b):
    return lax.dot_general(a, b, _TN, preferred_element_type=F32)


def _silu(x):
    return x * jax.nn.sigmoid(x)


def _gelu(x):
    return 0.5 * x * (1.0 + lax.erf(x * (2.0 ** -0.5)))


def _rms_rows(x):
    return x * lax.rsqrt(jnp.mean(x * x, axis=-1, keepdims=True) + EPS)


def _mod_in(x, g, mod_ref, k_shift, k_scale):
    return _rms_rows(x) * g * (1.0 + mod_ref[k_scale]) + mod_ref[k_shift]


def _resid_out(x, o, g, gate):
    return x + gate * (_rms_rows(o) * g)


def _params(n_grid, vmem_bytes):
    return pltpu.CompilerParams(
        dimension_semantics=("arbitrary",) * n_grid,
        vmem_limit_bytes=int(min(max(vmem_bytes, 32 << 20), VMEM_CAP)))


def _const_spec(shape):
    nd = len(shape)
    return pl.BlockSpec(shape, lambda *_: (0,) * nd, pipeline_mode=pl.Buffered(1))


def _row_specs(x, mod, tm):
    _, _, d = x.shape
    r = mod.shape[2]
    x_spec = pl.BlockSpec((None, tm, d), lambda b, i, *_: (b, i, 0))
    if r == 1:
        mod_spec = pl.BlockSpec((None, 6, 1, d), lambda b, i, *_: (b, 0, 0, 0))
    else:
        mod_spec = pl.BlockSpec((None, 6, tm, d), lambda b, i, *_: (b, 0, i, 0))
    return x_spec, mod_spec


def _interleave(*stages):
    live = list(stages)
    while live:
        for g in list(live):
            try:
                next(g)
            except StopIteration:
                live.remove(g)


def _nbytes(*arrays):
    return sum(int(np.prod(a.shape)) * jnp.dtype(a.dtype).itemsize for a in arrays)


def _ada_kernel(c_ref, w_ref, b_ref, o_ref):
    a = _silu(c_ref[...]).astype(BF16)
    o_ref[...] = _dot(a, w_ref[...].astype(BF16)) + b_ref[...]


def _ada(c_all, w_ada, b_ada):
    depth, d, n = w_ada.shape
    rows = c_all.shape[0]
    tn = n // 4
    return pl.pallas_call(
        _ada_kernel,
        grid=(depth, n // tn),
        in_specs=[pl.BlockSpec((rows, d), lambda l, j: (0, 0)),
                  pl.BlockSpec((None, d, tn), lambda l, j: (l, 0, j)),
                  pl.BlockSpec((None, 1, tn), lambda l, j: (l, 0, j))],
        out_specs=pl.BlockSpec((None, rows, tn), lambda l, j: (l, 0, j)),
        out_shape=jax.ShapeDtypeStruct((depth, rows, n), F32),
        compiler_params=_params(2, 3 * d * tn * 4 + (8 << 20)),
        name="ada_mod",
    )(c_all, w_ada, b_ada.reshape(depth, 1, n))


def _ffn_kernel(x_ref, mod_ref, g_ref, win_ref, wout_ref, y_ref, *, hidden, chunks, sub):
    rows = x_ref.shape[0]
    per_row = mod_ref.shape[1] != 1
    for r0 in range(0, rows, sub):
        rs = slice(r0, r0 + sub)
        x = x_ref[rs, :]
        shift, scale, gate_out = (mod_ref[k, rs, :] if per_row else mod_ref[k] for k in (3, 4, 5))
        h = (_rms_rows(x) * g_ref[2:3, :] * (1.0 + scale) + shift).astype(BF16)
        acc = None
        for c0, cw in chunks:
            gate = _dot(h, win_ref[:, c0:c0 + cw])
            up = _dot(h, win_ref[:, hidden + c0:hidden + c0 + cw])
            act = (_silu(gate) * up).astype(BF16)
            part = _dot(act, wout_ref[c0:c0 + cw, :])
            acc = part if acc is None else acc + part
        y_ref[rs, :] = _resid_out(x, acc, g_ref[3:4, :], gate_out)


def _split_chunks(total, width):
    out, c0 = [], 0
    while c0 < total:
        out.append((c0, min(width, total - c0)))
        c0 += width
    return tuple(out)


def _layer_spec(shape, layer):
    nd = len(shape) - 1
    return pl.BlockSpec((None,) + tuple(shape[1:]), lambda *_: (layer,) + (0,) * nd,
                        pipeline_mode=pl.Buffered(1))


def _ffn(x, mod, gains, w_in, w_out, layer, tm, sub):
    b, s, d = x.shape
    hidden = w_out.shape[1]
    tm = min(tm, s)
    sub = min(sub, tm)
    x_spec, mod_spec = _row_specs(x, mod, tm)
    vmem = _nbytes(w_in[0], w_out[0]) + 6 * tm * d * 4 + 4 * tm * 1024 * 4 + (8 << 20)
    return pl.pallas_call(
        functools.partial(_ffn_kernel, hidden=hidden, chunks=_split_chunks(hidden, 4 * MXU_WIDTH), sub=sub),
        grid=(b, s // tm),
        in_specs=[x_spec, mod_spec, _const_spec(gains.shape), _layer_spec(w_in.shape, layer),
                  _layer_spec(w_out.shape, layer)],
        out_specs=x_spec,
        out_shape=jax.ShapeDtypeStruct(x.shape, F32),
        compiler_params=_params(2, vmem),
        name="ffn",
    )(x, mod, gains, w_in, w_out)


def _gmlp_kernel(x_ref, mod_ref, g_ref, win_ref, lng_ref, lnb_ref, wbd_ref, brow_ref, wout_ref,
                 *out_and_scratch, half, groups, pair, emit_v):
    if emit_v:
        y_ref, vn_ref, v_scr = out_and_scratch
    else:
        y_ref, v_scr = out_and_scratch
    gw = half // groups
    cw = pair * gw
    nblk = half // cw
    rows = x_ref.shape[0]
    sub = wbd_ref.shape[1]
    x = x_ref[...]
    h_all = _mod_in(x, g_ref[0:1, :], mod_ref, 0, 1).astype(BF16)
    for r0 in range(0, rows, sub):
        rs = slice(r0, r0 + sub)
        h = h_all[rs]
        s1 = None
        s2 = None
        for j in range(nblk):
            v = _gelu(_dot(h, win_ref[:, half + j * cw:half + (j + 1) * cw]))
            v_scr[rs, j * cw:(j + 1) * cw] = v
            a1 = jnp.sum(v, axis=-1, keepdims=True)
            a2 = jnp.sum(v * v, axis=-1, keepdims=True)
            s1 = a1 if s1 is None else s1 + a1
            s2 = a2 if s2 is None else s2 + a2
        mu = s1 * (1.0 / half)
        rstd = lax.rsqrt(s2 * (1.0 / half) - mu * mu + EPS)
        acc = None
        for j in range(nblk):
            cols = slice(j * cw, (j + 1) * cw)
            vn = (v_scr[rs, cols] - mu) * rstd * lng_ref[:, cols] + lnb_ref[:, cols]
            if emit_v:
                vn_ref[rs, cols] = vn
            vnb = vn.astype(BF16)
            u = _gelu(_dot(h, win_ref[:, cols]))
            mixed = []
            for gg in range(pair):
                g = j * pair + gg
                mixed.append(_dot(wbd_ref[g], vnb[:, gg * gw:(gg + 1) * gw]) + brow_ref[:, g:g + 1])
            out = (u * jnp.concatenate(mixed, axis=1)).astype(BF16)
            part = _dot(out, wout_ref[cols, :])
            acc = part if acc is None else acc + part
        gate = mod_ref[2] if mod_ref.shape[1] == 1 else mod_ref[2, rs, :]
        y_ref[rs, :] = _resid_out(x[rs], acc, g_ref[1:2, :], gate)


def _gmlp(x, mod, gains, w_in, ln_g, ln_b, w_s, b_s, w_out, tm, sub, t_chunk, emit_v):
    b, s, d = x.shape
    half = w_out.shape[0]
    groups = w_s.shape[0]
    tm = min(tm, s)
    sub = min(sub, tm)
    n_rep = sub // t_chunk
    pos = np.arange(sub)
    expand = jnp.asarray(pos[:, None] % t_chunk == np.arange(t_chunk)[None, :], w_s.dtype)
    same = jnp.asarray(pos[:, None] // t_chunk == pos[None, :] // t_chunk)
    wt = jnp.tril(w_s[:, :t_chunk, :t_chunk])
    w_bd = jnp.einsum("rt,gts,cs->grc", expand, wt, expand, precision=lax.Precision.HIGHEST)
    w_bd = jnp.where(same[None], w_bd, 0.0).astype(BF16)
    b_rows = jnp.tile(b_s[:, :t_chunk].T, (n_rep, 1))
    x_spec, mod_spec = _row_specs(x, mod, tm)
    out_shape = [jax.ShapeDtypeStruct(x.shape, F32)]
    out_specs = [x_spec]
    if emit_v:
        out_shape.append(jax.ShapeDtypeStruct((b, s, half), F32))
        out_specs.append(pl.BlockSpec((None, tm, half), lambda bb, i: (bb, i, 0)))
    vmem = (_nbytes(w_in, w_out, w_bd) + tm * half * 4 * (5 if emit_v else 1)
            + 6 * tm * d * 4 + (12 << 20))
    res = pl.pallas_call(
        functools.partial(_gmlp_kernel, half=half, groups=groups, pair=2, emit_v=emit_v),
        grid=(b, s // tm),
        in_specs=[x_spec, mod_spec, _const_spec(gains.shape), _const_spec(w_in.shape),
                  _const_spec((1, half)), _const_spec((1, half)), _const_spec(w_bd.shape),
                  _const_spec(b_rows.shape), _const_spec(w_out.shape)],
        out_specs=out_specs,
        out_shape=out_shape,
        scratch_shapes=[pltpu.VMEM((tm, half), F32)],
        compiler_params=_params(2, vmem),
        name="gmlp_v" if emit_v else "gmlp",
    )(x, mod, gains, w_in, ln_g.reshape(1, half), ln_b.reshape(1, half), w_bd, b_rows, w_out)
    return res if emit_v else res[0]


def _outproj_kernel(o_ref, x_ref, mod_ref, g_ref, w_ref, y_ref):
    y_ref[...] = _resid_out(x_ref[...], _dot(o_ref[...], w_ref[...]), g_ref[1:2, :], mod_ref[2])


def _outproj(o, x, mod, gains, w_out, tm):
    b, s, d = x.shape
    tm = min(tm, s)
    k = o.shape[-1]
    x_spec, mod_spec = _row_specs(x, mod, tm)
    return pl.pallas_call(
        _outproj_kernel,
        grid=(b, s // tm),
        in_specs=[pl.BlockSpec((None, tm, k), lambda bb, i: (bb, i, 0)), x_spec, mod_spec,
                  _const_spec(gains.shape), _const_spec(w_out.shape)],
        out_specs=x_spec,
        out_shape=jax.ShapeDtypeStruct(x.shape, F32),
        compiler_params=_params(2, _nbytes(w_out) + 8 * tm * d * 4 + 2 * tm * k * 2 + (8 << 20)),
        name="outproj",
    )(o, x, mod, gains, w_out)


def _max_head_norm2(x_t, heads):
    d, rows = x_t.shape
    n2 = jnp.sum((x_t * x_t).reshape(heads, d // heads, rows), axis=1)
    return jnp.full((8, LANES), jnp.max(n2), F32)


def _qkv_kernel(x_ref, mod_ref, g_ref, w_ref, q_ref, k_ref, v_ref, k16_ref, v16_ref, *norm_refs,
                scale, transposed):
    d = x_ref.shape[-1]
    h = _mod_in(x_ref[...], g_ref[0:1, :], mod_ref, 0, 1).astype(BF16)
    q = _dot(h, w_ref[:, :d]) * scale
    k = _dot(h, w_ref[:, d:2 * d])
    k_ref[...] = k
    k16_ref[...] = k.astype(BF16)
    v = _dot(h, w_ref[:, 2 * d:])
    v_ref[...] = v
    if transposed:
        qn_ref, kn_ref = norm_refs
        q_t = q.T
        q_ref[...] = q_t.astype(BF16)
        v16_ref[...] = v.T.astype(BF16)
        qn_ref[...] = _max_head_norm2(q_t, 2 * DIFF_HEADS)
        kn_ref[...] = _max_head_norm2(k.T, 2 * DIFF_HEADS)
    else:
        q_ref[...] = q.astype(BF16)
        v16_ref[...] = v.astype(BF16)


def _qkv(x, mod, gains, w_in, tm, scale, transposed):
    b, s, d = x.shape
    tm = min(tm, s)
    x_spec, mod_spec = _row_specs(x, mod, tm)
    sds = jax.ShapeDtypeStruct
    out_specs = [x_spec, x_spec, x_spec, x_spec, x_spec]
    out_shape = [sds(x.shape, BF16), sds(x.shape, F32), sds(x.shape, F32), sds(x.shape, BF16),
                 sds(x.shape, BF16)]
    if transposed:
        t_spec = pl.BlockSpec((None, d, tm), lambda bb, i: (bb, 0, i))
        n_spec = pl.BlockSpec((None, None, 8, LANES), lambda bb, i: (bb, i, 0, 0))
        out_specs[0] = out_specs[4] = t_spec
        out_shape[0] = out_shape[4] = sds((b, d, s), BF16)
        out_specs += [n_spec, n_spec]
        out_shape += [sds((b, s // tm, 8, LANES), F32)] * 2
    return pl.pallas_call(
        functools.partial(_qkv_kernel, scale=scale, transposed=transposed),
        grid=(b, s // tm),
        in_specs=[x_spec, mod_spec, _const_spec(gains.shape), _const_spec(w_in.shape)],
        out_specs=out_specs,
        out_shape=out_shape,
        compiler_params=_params(2, _nbytes(w_in) + 24 * tm * d * 4 + (8 << 20)),
        name="diff_qkv_t" if transposed else "diff_qkv",
    )(x, mod, gains, w_in)


def _diff_lambda(lam_ref, lam_init):
    lp = lam_ref[...]
    e1 = jnp.exp(jnp.sum(lp[0:1, :] * lp[1:2, :], axis=-1, keepdims=True))
    e2 = jnp.exp(jnp.sum(lp[2:3, :] * lp[3:4, :], axis=-1, keepdims=True))
    return e1 - e2 + lam_init


def _stack_q_halves(q, qs_scr, rows):
    lane = lax.broadcasted_iota(jnp.int32, q.shape, 1)
    first = (lane & (LANES - 1)) < (LANES // 2)
    zero = jnp.zeros_like(q)
    qs_scr[0:rows, :] = jnp.where(first, q, zero)
    qs_scr[rows:2 * rows, :] = jnp.where(first, zero, q)


def _flash_init(m_scr, l_scr, acc_scr):
    m_scr[...] = jnp.full(m_scr.shape, NEG_INF, F32)
    l_scr[...] = jnp.zeros(l_scr.shape, F32)
    acc_scr[...] = jnp.zeros(acc_scr.shape, F32)


def _flash_step(qs_scr, scores, v_blk, m_scr, l_scr, acc_scr, mask):
    sl = lambda h: slice(LANES * h, LANES * (h + 1))
    s_all = [scores(h, qs_scr[:, sl(h)]) for h in range(DIFF_HEADS)]
    stats = []
    for h, s in enumerate(s_all):
        if mask is not None:
            s = jnp.where(mask, s, NEG_INF)
        m_old = m_scr[h]
        m_new = jnp.maximum(m_old, jnp.max(s, axis=1, keepdims=True))
        alpha = jnp.exp(m_old - m_new)
        p = jnp.exp(s - m_new)
        l_scr[h] = alpha * l_scr[h] + jnp.sum(p, axis=1, keepdims=True)
        m_scr[h] = m_new
        stats.append((alpha, p.astype(BF16)))
    for h, (alpha, p) in enumerate(stats):
        acc_scr[:, sl(h)] = alpha * acc_scr[:, sl(h)] + _dot(p, v_blk(h))


def _flash_finish(l_scr, acc_scr, lam, subln, lam_init, rows):
    outs = []
    for h in range(DIFF_HEADS):
        sl = slice(LANES * h, LANES * (h + 1))
        l = l_scr[h]
        o = acc_scr[0:rows, sl] / l[0:rows] - lam * (acc_scr[rows:2 * rows, sl] / l[rows:2 * rows])
        outs.append((_rms_rows(o) * subln * (1.0 - lam_init)).astype(BF16))
    return jnp.concatenate(outs, axis=1)


def _chunk_mask(rows, cols, row_pos0, col_pos0):
    r = lax.broadcasted_iota(jnp.int32, (2 * rows, cols), 0)
    c = lax.broadcasted_iota(jnp.int32, (2 * rows, cols), 1)
    r = jnp.where(r >= rows, r - rows, r)
    return ((c + col_pos0) // CHUNK) <= ((r + row_pos0) // CHUNK)


def _flash_t_step(qz_scr, k_ref, vt_ref, m_scr, l_scr, acc_scr, mask, bounded=False):
    heads = [(h, c) for h in range(DIFF_HEADS) for c in range(2)]
    rows = lambda h: slice(LANES * h, LANES * (h + 1))

    def scores(i):
        h, c = heads[i]
        return _dot(k_ref[:, rows(h)], qz_scr[c, rows(h), :])

    def softmax(i, s):
        r = 2 * heads[i][0] + heads[i][1]
        if mask is not None:
            s = jnp.where(mask, s, NEG_INF)
        m_old = m_scr[r:r + 1, :]
        block_ref = 0.0 if bounded else jnp.max(s, axis=0, keepdims=True)
        m_new = jnp.maximum(m_old, block_ref)
        alpha = jnp.exp2(m_old - m_new)
        p = jnp.exp2(s - m_new)
        l_scr[r:r + 1, :] = alpha * l_scr[r:r + 1, :] + jnp.sum(p, axis=0, keepdims=True)
        m_scr[r:r + 1, :] = m_new
        return alpha, p.astype(BF16)

    def accumulate(i, alpha, p):
        h, c = heads[i]
        acc_scr[c, rows(h), :] = alpha * acc_scr[c, rows(h), :] + _dot(vt_ref[rows(h), :], p)

    n = len(heads)
    ahead, behind = 2, 1
    pending_s = {i: scores(i) for i in range(ahead)}
    pending_p = {}
    for i in range(n):
        if i + ahead < n:
            pending_s[i + ahead] = scores(i + ahead)
        pending_p[i] = softmax(i, pending_s.pop(i))
        if i - behind >= 0:
            accumulate(i - behind, *pending_p.pop(i - behind))
    for i in sorted(pending_p):
        accumulate(i, *pending_p[i])


def _flash_prompt_kernel(qt_ref, kt_ref, safe_ref, q_ref, k_ref, vt_ref, x_ref, mod_ref, g_ref, lam_ref,
                         subln_ref, woutt_ref, y_ref, qz_scr, m_scr, l_scr, acc_scr, *, lam_init):
    p = pl.program_id(1)
    qi = qt_ref[p]
    ki = kt_ref[p]
    bounded = safe_ref[pl.program_id(0) * pl.num_programs(1) + p] != 0
    tk = k_ref.shape[0]
    tq = q_ref.shape[1]

    @pl.when(ki == 0)
    def _():
        m_scr[...] = jnp.full(m_scr.shape, NEG_INF, F32)
        l_scr[...] = jnp.zeros(l_scr.shape, F32)
        acc_scr[...] = jnp.zeros(acc_scr.shape, F32)
        q = q_ref[...]
        feat = lax.broadcasted_iota(jnp.int32, q.shape, 0)
        first = (feat & (LANES - 1)) < (LANES // 2)
        zero = jnp.zeros_like(q)
        qz_scr[0] = jnp.where(first, q, zero)
        qz_scr[1] = jnp.where(first, zero, q)

    @pl.when((ki < qi) & bounded)
    def _():
        _flash_t_step(qz_scr, k_ref, vt_ref, m_scr, l_scr, acc_scr, None, bounded=True)

    @pl.when((ki < qi) & jnp.logical_not(bounded))
    def _():
        _flash_t_step(qz_scr, k_ref, vt_ref, m_scr, l_scr, acc_scr, None)

    def chunk_causal():
        key = lax.broadcasted_iota(jnp.int32, (tk, tq), 0)
        qry = lax.broadcasted_iota(jnp.int32, (tk, tq), 1)
        return (key // CHUNK) <= (qry // CHUNK)

    @pl.when((ki == qi) & bounded)
    def _():
        _flash_t_step(qz_scr, k_ref, vt_ref, m_scr, l_scr, acc_scr, chunk_causal(), bounded=True)

    @pl.when((ki == qi) & jnp.logical_not(bounded))
    def _():
        _flash_t_step(qz_scr, k_ref, vt_ref, m_scr, l_scr, acc_scr, chunk_causal())

    @pl.when(ki == qi)
    def _():
        lam = _diff_lambda(lam_ref, lam_init)
        outs = []
        for h in range(DIFF_HEADS):
            rows = slice(LANES * h, LANES * (h + 1))
            o = (acc_scr[0, rows, :] / l_scr[2 * h:2 * h + 1, :]
                 - lam * (acc_scr[1, rows, :] / l_scr[2 * h + 1:2 * h + 2, :]))
            o = o * lax.rsqrt(jnp.mean(o * o, axis=0, keepdims=True) + EPS)
            outs.append((o * (subln_ref[...] * (1.0 - lam_init))).astype(BF16))
        out_t = _dot(woutt_ref[...], jnp.concatenate(outs, axis=0))
        y_ref[...] = _resid_out(x_ref[...], out_t.T, g_ref[1:2, :], mod_ref[2])


def _flash_prompt(q_t, k16, v_t, qn2, kn2, x, mod, gains, lam_p, subln, w_out, lam_init, tq):
    b, s, d = x.shape
    tq = min(tq, s)
    nq = s // tq
    assert qn2.shape == kn2.shape == (b, nq)
    pairs = [(qi, ki) for qi in range(nq) for ki in range(qi + 1)]
    qt = jnp.asarray([pq for pq, _ in pairs], jnp.int32)
    kt = jnp.asarray([pk for _, pk in pairs], jnp.int32)
    safe = (qn2[:, qt] * kn2[:, kt] <= SOFTMAX_SAFE_LOG2 ** 2).astype(jnp.int32).reshape(-1)
    x_spec = pl.BlockSpec((None, tq, d), lambda bb, p, qt_, kt_, safe_: (bb, qt_[p], 0))
    q_spec = pl.BlockSpec((None, d, tq), lambda bb, p, qt_, kt_, safe_: (bb, 0, qt_[p]))
    k_spec = pl.BlockSpec((None, tq, d), lambda bb, p, qt_, kt_, safe_: (bb, kt_[p], 0))
    v_spec = pl.BlockSpec((None, d, tq), lambda bb, p, qt_, kt_, safe_: (bb, 0, kt_[p]))
    mod_spec = pl.BlockSpec((None, 6, 1, d), lambda bb, p, qt_, kt_, safe_: (bb, 0, 0, 0))
    w_out_t = w_out.T
    vmem = (_nbytes(w_out) + 6 * tq * d * 2 + 4 * tq * d * 4 + 2 * tq * d * 2 + 2 * tq * d * 4
            + 8 * tq * tq * 4 + (8 << 20))
    grid_spec = pltpu.PrefetchScalarGridSpec(
        num_scalar_prefetch=3,
        grid=(b, len(pairs)),
        in_specs=[q_spec, k_spec, v_spec, x_spec, mod_spec, _const_spec(gains.shape),
                  _const_spec(lam_p.shape), _const_spec((LANES, 1)), _const_spec(w_out_t.shape)],
        out_specs=x_spec,
        scratch_shapes=[pltpu.VMEM((2, d, tq), BF16),
                        pltpu.VMEM((2 * DIFF_HEADS, tq), F32),
                        pltpu.VMEM((2 * DIFF_HEADS, tq), F32),
                        pltpu.VMEM((2, d, tq), F32)])
    return pl.pallas_call(
        functools.partial(_flash_prompt_kernel, lam_init=lam_init),
        grid_spec=grid_spec,
        out_shape=jax.ShapeDtypeStruct(x.shape, F32),
        compiler_params=_params(2, vmem),
        name="diff_flash_prompt",
    )(qt, kt, safe, q_t, k16, v_t, x, mod, gains, lam_p, subln.reshape(LANES, 1), w_out_t)


def _flash_sample_kernel(q_ref, ck_ref, cv_ref, kn_ref, vn_ref, lam_ref, subln_ref, o_ref,
                         qs_scr, kpad_scr, vpad_scr, m_scr, l_scr, acc_scr,
                         *, rows, past, lam_init, new_mask_needed):
    kb = pl.program_id(1)

    @pl.when(kb == 0)
    def _():
        _flash_init(m_scr, l_scr, acc_scr)
        _stack_q_halves(q_ref[...], qs_scr, rows)
        kpad_scr[...] = jnp.zeros(kpad_scr.shape, BF16)
        vpad_scr[...] = jnp.zeros(vpad_scr.shape, BF16)
        kpad_scr[0:rows, :] = kn_ref[...]
        vpad_scr[0:rows, :] = vn_ref[...]
        c = lax.broadcasted_iota(jnp.int32, (2 * rows, LANES), 1)
        mask = c < rows
        if new_mask_needed:
            mask = mask & _chunk_mask(rows, LANES, past, past)
        _flash_step(qs_scr,
                    lambda h, qp: _dot_nt(qp, kpad_scr[:, LANES * h:LANES * (h + 1)]),
                    lambda h: vpad_scr[:, LANES * h:LANES * (h + 1)],
                    m_scr, l_scr, acc_scr, mask)

    tk = ck_ref.shape[1]
    _flash_step(qs_scr,
                lambda h, qp: _dot(qp, ck_ref[LANES * h:LANES * (h + 1), :].astype(BF16)),
                lambda h: cv_ref[pl.ds(h, tk, stride=DIFF_HEADS), :].astype(BF16),
                m_scr, l_scr, acc_scr, None)

    @pl.when(kb == pl.num_programs(1) - 1)
    def _():
        lam = _diff_lambda(lam_ref, lam_init)
        o_ref[...] = _flash_finish(l_scr, acc_scr, lam, subln_ref[...], lam_init, rows)


def _flash_sample(q, cache_k_t, cache_v, k_new, v_new, lam_p, subln, lam_init, rows, tk):
    bs, d, past = cache_k_t.shape
    tk = min(tk, past)
    pos = past + np.arange(rows)
    new_mask_needed = not bool(np.all((pos[None, :] // CHUNK) <= (pos[:, None] // CHUNK)))
    row_spec = pl.BlockSpec((rows, d), lambda b, kb: (b, 0))
    k_spec = pl.BlockSpec((None, d, tk), lambda b, kb: (b, 0, kb))
    v_spec = pl.BlockSpec((None, tk * DIFF_HEADS, LANES), lambda b, kb: (b, kb, 0))
    vmem = 4 * tk * d * 4 + 4 * tk * d * 2 + (12 << 20)
    return pl.pallas_call(
        functools.partial(_flash_sample_kernel, rows=rows, past=past, lam_init=lam_init,
                          new_mask_needed=new_mask_needed),
        grid=(bs, past // tk),
        in_specs=[row_spec, k_spec, v_spec, row_spec, row_spec,
                  _const_spec(lam_p.shape), _const_spec((1, LANES))],
        out_specs=row_spec,
        out_shape=jax.ShapeDtypeStruct(q.shape, BF16),
        scratch_shapes=[pltpu.VMEM((2 * rows, d), BF16),
                        pltpu.VMEM((LANES, d), BF16), pltpu.VMEM((LANES, d), BF16),
                        pltpu.VMEM((DIFF_HEADS, 2 * rows, 1), F32),
                        pltpu.VMEM((DIFF_HEADS, 2 * rows, 1), F32),
                        pltpu.VMEM((2 * rows, d), F32)],
        compiler_params=_params(2, vmem),
        name="diff_flash_sample",
    )(q, cache_k_t, cache_v, k_new, v_new, lam_p, subln.reshape(1, LANES))


def _ret_log_gamma(h):
    return float(np.log(np.float32(1.0) - np.float32(2.0) ** np.float32(-5.0 - h)))


def _rotate_pairs(x):
    n = x.shape[-1]
    lane = lax.broadcasted_iota(jnp.int32, x.shape, 1)
    return jnp.where((lane & 1) == 0, -pltpu.roll(x, n - 1, 1), pltpu.roll(x, 1, 1))


def _ret_project(h, win_ref, cos, sin, d, dk):
    cos4 = jnp.concatenate([cos] * (d // dk), axis=1)
    sin4 = jnp.concatenate([sin] * (d // dk), axis=1)
    q = _dot(h, win_ref[:, 0:d])
    q = q * cos4 + _rotate_pairs(q) * sin4
    k = _dot(h, win_ref[:, d:2 * d])
    k = (k * cos4 + _rotate_pairs(k) * sin4) * (dk ** -0.5)
    v = _dot(h, win_ref[:, 2 * d:4 * d])
    return q, k, v


def _ret_decay(lg, rows, same_seq=None):
    t = lax.broadcasted_iota(jnp.int32, (rows, rows), 0)
    s = lax.broadcasted_iota(jnp.int32, (rows, rows), 1)
    ok = t >= s
    if same_seq is not None:
        ok = ok & ((t // same_seq) == (s // same_seq))
    diff = jnp.maximum(t - s, 0).astype(F32)
    return jnp.where(ok, jnp.exp(lg * diff), 0.0)


def _ret_prompt_kernel(x_ref, mod_ref, g_ref, win_ref, cos_ref, sin_ref, wout_ref, y_ref, st_ref,
                       state_scr, o_scr, *, heads):
    i = pl.program_id(1)
    rows, d = x_ref.shape
    dk = d // heads
    dv = 2 * dk

    @pl.when(i == 0)
    def _():
        state_scr[...] = jnp.zeros(state_scr.shape, F32)

    x = x_ref[...]
    h = _mod_in(x, g_ref[0:1, :], mod_ref, 0, 1).astype(BF16)
    sub = min(rows, SUB_ROWS)
    t = lax.broadcasted_iota(jnp.int32, (sub, 1), 0).astype(F32)
    lgs = [_ret_log_gamma(hh) for hh in range(heads)]
    states = [state_scr[hh] for hh in range(heads)]
    pre = {}

    def project(r0):
        hs = h[r0:r0 + sub]
        parts = []
        for c0, c1 in ((0, d), (d, 2 * d), (2 * d, 4 * d), (4 * d, 6 * d)):
            parts.append(_dot(hs, win_ref[:, c0:c1]))
            yield
        pre[r0] = tuple(parts)

    def mix(r0):
        q, k, v, g = pre.pop(r0)
        cos4 = jnp.concatenate([cos_ref[r0:r0 + sub, :]] * heads, axis=1)
        sin4 = jnp.concatenate([sin_ref[r0:r0 + sub, :]] * heads, axis=1)
        q = q * cos4 + _rotate_pairs(q) * sin4
        k = (k * cos4 + _rotate_pairs(k) * sin4) * (dk ** -0.5)
        sg = _silu(g)
        qs = [q[:, hh * dk:(hh + 1) * dk].astype(BF16) for hh in range(heads)]
        ks = [k[:, hh * dk:(hh + 1) * dk] for hh in range(heads)]
        vs = [v[:, hh * dv:(hh + 1) * dv].astype(BF16) for hh in range(heads)]
        yield
        scores = [_dot_nt(qs[hh], ks[hh].astype(BF16)) for hh in range(heads)]
        cross = [_dot(qs[hh], states[hh].astype(BF16)) for hh in range(heads)]
        kv = [_dot_tn((ks[hh] * jnp.exp(lgs[hh] * (sub - 1.0 - t))).astype(BF16), vs[hh])
              for hh in range(heads)]
        for hh in range(heads):
            states[hh] = math.exp(lgs[hh] * sub) * states[hh] + kv[hh]
        yield
        inner = [_dot((scores[hh] * _ret_decay(lgs[hh], sub)).astype(BF16), vs[hh]) for hh in range(heads)]
        yield
        gated = []
        for hh in range(heads):
            o = inner[hh] + cross[hh] * jnp.exp(lgs[hh] * (t + 1.0))
            gated.append((_rms_rows(o) * sg[:, hh * dv:(hh + 1) * dv]).astype(BF16))
        o_scr[r0:r0 + sub, :] = jnp.concatenate(gated, axis=1)

    starts = list(range(0, rows, sub))
    _interleave(project(starts[0]))
    for n, r0 in enumerate(starts):
        nxt = [project(starts[n + 1])] if n + 1 < len(starts) else []
        _interleave(mix(r0), *nxt)
    for hh in range(heads):
        state_scr[hh] = states[hh]
    y_ref[...] = _resid_out(x, _dot(o_scr[...], wout_ref[...]), g_ref[1:2, :], mod_ref[2])

    @pl.when(i == pl.num_programs(1) - 1)
    def _():
        st_ref[...] = state_scr[...]


def _xpos_tables(pos, dk):
    inv = 1.0 / (10000.0 ** jnp.linspace(0.0, 1.0, dk // 2, dtype=F32))
    ang = pos.astype(F32)[:, None] * jnp.repeat(inv, 2)[None, :]
    return jnp.cos(ang), jnp.sin(ang)


def _ret_prompt(x, mod, gains, w_in, w_out, tm):
    b, s, d = x.shape
    tm = min(tm, s)
    heads = RET_HEADS
    dk = d // heads
    dv = 2 * dk
    cos, sin = _xpos_tables(jnp.arange(s), dk)
    x_spec, mod_spec = _row_specs(x, mod, tm)
    tab_spec = pl.BlockSpec((tm, dk), lambda bb, i: (i, 0))
    st_spec = pl.BlockSpec((None, heads, dk, dv), lambda bb, i: (bb, 0, 0, 0))
    vmem = _nbytes(w_in, w_out) + 3 * heads * dk * dv * 4 + 40 * tm * d * 4 + (8 << 20)
    return pl.pallas_call(
        functools.partial(_ret_prompt_kernel, heads=heads),
        grid=(b, s // tm),
        in_specs=[x_spec, mod_spec, _const_spec(gains.shape), _const_spec(w_in.shape),
                  tab_spec, tab_spec, _const_spec(w_out.shape)],
        out_specs=[x_spec, st_spec],
        out_shape=[jax.ShapeDtypeStruct(x.shape, F32),
                   jax.ShapeDtypeStruct((b, heads, dk, dv), F32)],
        scratch_shapes=[pltpu.VMEM((heads, dk, dv), F32), pltpu.VMEM((tm, heads * dv), BF16)],
        compiler_params=_params(2, vmem),
        name="ret_prompt",
    )(x, mod, gains, w_in, cos, sin, w_out)


def _ret_sample_kernel(x_ref, mod_ref, g_ref, win_ref, cos_ref, sin_ref, st_in_ref, wout_ref,
                       y_ref, st_out_ref, q_scr, k_scr, v_scr, o_scr, *, heads, rows):
    b = pl.program_id(0)
    total, d = x_ref.shape
    dk = d // heads
    dv = 2 * dk

    @pl.when(b == 0)
    def _():
        h = _mod_in(x_ref[...], g_ref[0:1, :], mod_ref, 0, 1).astype(BF16)
        q, k, v = _ret_project(h, win_ref, cos_ref[...], sin_ref[...], d, dk)
        q_scr[...] = q.astype(BF16)
        k_scr[...] = k
        v_scr[...] = v.astype(BF16)
        for hh in range(heads):
            scores = (_dot_nt(q[:, hh * dk:(hh + 1) * dk].astype(BF16),
                              k[:, hh * dk:(hh + 1) * dk].astype(BF16))
                      * _ret_decay(_ret_log_gamma(hh), total, same_seq=rows))
            o_scr[:, hh * dv:(hh + 1) * dv] = _dot(scores.astype(BF16),
                                                   v[:, hh * dv:(hh + 1) * dv].astype(BF16))

    r0 = pl.multiple_of(b * rows, rows)
    t = lax.broadcasted_iota(jnp.int32, (rows, 1), 0).astype(F32)
    ta = lax.broadcasted_iota(jnp.int32, (total, 1), 0)
    mine = (ta >= r0) & (ta < r0 + rows)
    t_all = (ta - r0).astype(F32)
    lgs = [_ret_log_gamma(hh) for hh in range(heads)]
    states = [st_in_ref[hh] for hh in range(heads)]
    cross = [_dot(q_scr[pl.ds(r0, rows), hh * dk:(hh + 1) * dk], states[hh].astype(BF16)) for hh in range(heads)]
    kv = [_dot_tn(jnp.where(mine, k_scr[:, hh * dk:(hh + 1) * dk] * jnp.exp(lgs[hh] * (rows - 1.0 - t_all)),
                            0.0).astype(BF16), v_scr[:, hh * dv:(hh + 1) * dv]) for hh in range(heads)]
    for hh in range(heads):
        o_scr[pl.ds(r0, rows), hh * dv:(hh + 1) * dv] += cross[hh] * jnp.exp(lgs[hh] * (t + 1.0))
        st_out_ref[hh] = math.exp(lgs[hh] * rows) * states[hh] + kv[hh]

    @pl.when(b == pl.num_programs(0) - 1)
    def _():
        h = _mod_in(x_ref[...], g_ref[0:1, :], mod_ref, 0, 1).astype(BF16)
        gated = []
        for hh in range(heads):
            sl = slice(hh * dv, (hh + 1) * dv)
            sg = _silu(_dot(h, win_ref[:, 4 * d + hh * dv:4 * d + (hh + 1) * dv]))
            gated.append((_rms_rows(o_scr[:, sl]) * sg).astype(BF16))
        oc = jnp.concatenate(gated, axis=1)
        y_ref[...] = _resid_out(x_ref[...], _dot(oc, wout_ref[...]), g_ref[1:2, :], mod_ref[2])


def _ret_sample(x, mod, gains, w_in, w_out, state, rows, past):
    total, d = x.shape
    bs, heads, dk, dv = state.shape
    cos, sin = _xpos_tables(past + jnp.arange(rows), dk)
    cos = jnp.tile(cos, (bs, 1))
    sin = jnp.tile(sin, (bs, 1))
    st_spec = pl.BlockSpec((None, heads, dk, dv), lambda b: (b, 0, 0, 0))
    vmem = _nbytes(w_in, w_out, x, x, mod) + 4 * heads * dk * dv * 4 + 60 * total * d * 4 + (8 << 20)
    return pl.pallas_call(
        functools.partial(_ret_sample_kernel, heads=heads, rows=rows),
        grid=(bs,),
        in_specs=[_const_spec(x.shape), _const_spec(mod.shape), _const_spec(gains.shape),
                  _const_spec(w_in.shape), _const_spec(cos.shape), _const_spec(sin.shape),
                  st_spec, _const_spec(w_out.shape)],
        out_specs=[pl.BlockSpec(x.shape, lambda b: (0, 0)), st_spec],
        out_shape=[jax.ShapeDtypeStruct(x.shape, F32), jax.ShapeDtypeStruct(state.shape, F32)],
        scratch_shapes=[pltpu.VMEM((total, d), BF16), pltpu.VMEM((total, d), F32),
                        pltpu.VMEM((total, 2 * d), BF16), pltpu.VMEM((total, 2 * d), F32)],
        compiler_params=_params(1, vmem),
        name="ret_sample",
    )(x, mod, gains, w_in, cos, sin, state, w_out)


def _hgrn_lower(lb_ref, layer):
    lb = lb_ref[...]
    e = jnp.exp(lb - jnp.max(lb, axis=0, keepdims=True))
    p = e / jnp.sum(e, axis=0, keepdims=True)
    if layer == 0:
        return jnp.zeros_like(p[0:1, :])
    return jnp.sum(p[1:layer + 1, :], axis=0, keepdims=True)


def _block_cumsum(x, block):
    rows = x.shape[0]
    t = lax.broadcasted_iota(jnp.int32, (rows, rows), 0)
    s = lax.broadcasted_iota(jnp.int32, (rows, rows), 1)
    tri = jnp.where((s <= t) & ((t // block) == (s // block)), 1.0, 0.0).astype(BF16)
    hi = x.astype(BF16)
    lo = (x - hi.astype(F32)).astype(BF16)
    return _dot(tri, hi) + _dot(tri, lo)


def _block_last(x, block):
    rows = x.shape[0]
    parts = [jnp.broadcast_to(x[c * block + block - 1:c * block + block, :], (block, x.shape[1]))
             for c in range(rows // block)]
    return parts[0] if len(parts) == 1 else jnp.concatenate(parts, axis=0)


def _hgrn_matmuls(h, win_ref, d):
    return tuple(_dot(h, win_ref[:, n * d:(n + 1) * d]) for n in range(4))


def _hgrn_gates(pre, lower, block):
    q, f, v, g = pre
    q = _silu(q)
    sg = _silu(g)
    sig = jax.nn.sigmoid(f)
    forget = lower + (1.0 - lower) * sig
    k = (1.0 - lower) * (1.0 - sig)
    b = _block_cumsum(jnp.log(forget), block) * math.log2(math.e)
    b_last = _block_last(b, block)
    return q * jnp.exp2(b), k * jnp.exp2(-b), k * jnp.exp2(b_last - b), v, sg, jnp.exp2(b_last)


def _hgrn_project(h, win_ref, lower, d, block):
    return _hgrn_gates(_hgrn_matmuls(h, win_ref, d), lower, block)


def _hgrn_prompt_kernel(x_ref, mod_ref, g_ref, win_ref, lb_ref, ng_ref, wout_ref, y_ref, st_ref,
                        state_scr, o_scr, *, layer):
    i = pl.program_id(1)
    rows, d = x_ref.shape
    heads = d // HG_WIDTH
    w = HG_WIDTH

    @pl.when(i == 0)
    def _():
        state_scr[...] = jnp.zeros(state_scr.shape, F32)

    x = x_ref[...]
    h = _mod_in(x, g_ref[0:1, :], mod_ref, 0, 1).astype(BF16)
    lower = _hgrn_lower(lb_ref, layer)
    t = lax.broadcasted_iota(jnp.int32, (CHUNK, CHUNK), 0)
    s = lax.broadcasted_iota(jnp.int32, (CHUNK, CHUNK), 1)
    causal = t >= s
    ch = lambda hh: slice(hh * w, (hh + 1) * w)
    sub = min(rows, SUB_ROWS)
    n_chunks = sub // CHUNK
    cells = [(c, hh) for c in range(n_chunks) for hh in range(heads)]
    rc = lambda c: slice(c * CHUNK, (c + 1) * CHUNK)
    states = [state_scr[hh] for hh in range(heads)]
    starts = list(range(0, rows, sub))
    pre = {}

    def project(r0):
        parts = []
        for n in range(4):
            parts.append(_dot(h[r0:r0 + sub], win_ref[:, n * d:(n + 1) * d]))
            yield
        pre[r0] = tuple(parts)

    def mix(r0):
        q_dec, k_inv, k_end, v, sg, e_last = _hgrn_gates(pre.pop(r0), lower, CHUNK)
        yield
        qd16, ki16, ke16, v16 = (a.astype(BF16) for a in (q_dec, k_inv, k_end, v))
        kv = {(c, hh): _dot_tn(v16[rc(c), ch(hh)], ke16[rc(c), ch(hh)]) for c, hh in cells}
        scores = {(c, hh): jnp.where(causal, _dot_nt(qd16[rc(c), ch(hh)], ki16[rc(c), ch(hh)]),
                                     0.0).astype(BF16) for c, hh in cells}
        yield
        state_in = {}
        for hh in range(heads):
            for c in range(n_chunks):
                state_in[c, hh] = states[hh].astype(BF16)
                states[hh] = e_last[c * CHUNK:c * CHUNK + 1, ch(hh)] * states[hh] + kv[c, hh]
        o = [jnp.concatenate([_dot(scores[c, hh], v16[rc(c), ch(hh)])
                              + _dot_nt(qd16[rc(c), ch(hh)], state_in[c, hh]) for c in range(n_chunks)],
                             axis=0) for hh in range(heads)]
        yield
        o_scr[r0:r0 + sub, :] = jnp.concatenate(
            [(_rms_rows(o[hh]) * ng_ref[:, ch(hh)] * sg[:, ch(hh)]).astype(BF16) for hh in range(heads)], axis=1)

    _interleave(project(starts[0]))
    for n, r0 in enumerate(starts):
        nxt = [project(starts[n + 1])] if n + 1 < len(starts) else []
        _interleave(mix(r0), *nxt)
    for hh in range(heads):
        state_scr[hh] = states[hh]
    y_ref[...] = _resid_out(x, _dot(o_scr[...], wout_ref[...]), g_ref[1:2, :], mod_ref[2])

    @pl.when(i == pl.num_programs(1) - 1)
    def _():
        for hh in range(heads):
            st_ref[hh] = state_scr[hh].T


def _hgrn_prompt(x, mod, gains, w_in, norm_g, w_out, lower_bounds, layer, tm):
    b, s, d = x.shape
    tm = min(tm, s)
    heads = d // HG_WIDTH
    x_spec, mod_spec = _row_specs(x, mod, tm)
    st_spec = pl.BlockSpec((None, heads, HG_WIDTH, HG_WIDTH), lambda bb, i: (bb, 0, 0, 0))
    vmem = _nbytes(w_in, w_out) + 40 * tm * d * 4 + (8 << 20)
    return pl.pallas_call(
        functools.partial(_hgrn_prompt_kernel, layer=layer),
        grid=(b, s // tm),
        in_specs=[x_spec, mod_spec, _const_spec(gains.shape), _const_spec(w_in.shape),
                  _const_spec(lower_bounds.shape), _const_spec((1, d)), _const_spec(w_out.shape)],
        out_specs=[x_spec, st_spec],
        out_shape=[jax.ShapeDtypeStruct(x.shape, F32),
                   jax.ShapeDtypeStruct((b, heads, HG_WIDTH, HG_WIDTH), F32)],
        scratch_shapes=[pltpu.VMEM((heads, HG_WIDTH, HG_WIDTH), F32), pltpu.VMEM((tm, d), BF16)],
        compiler_params=_params(2, vmem),
        name="hgrn_prompt",
    )(x, mod, gains, w_in, lower_bounds, norm_g.reshape(1, d), w_out)


def _hgrn_sample_kernel(x_ref, mod_ref, g_ref, win_ref, lb_ref, ng_ref, st_in_ref, wout_ref,
                        y_ref, st_out_ref, qd_scr, ke_scr, v_scr, sg_scr, el_scr, o_scr,
                        *, layer, rows):
    b = pl.program_id(0)
    total, d = x_ref.shape
    heads = d // HG_WIDTH
    w = HG_WIDTH

    @pl.when(b == 0)
    def _():
        h = _mod_in(x_ref[...], g_ref[0:1, :], mod_ref, 0, 1).astype(BF16)
        q_dec, k_inv, k_end, v, sg, e_last = _hgrn_project(h, win_ref, _hgrn_lower(lb_ref, layer), d, rows)
        qd_scr[...] = q_dec
        ke_scr[...] = k_end
        v_scr[...] = v
        sg_scr[...] = sg
        el_scr[...] = e_last
        t = lax.broadcasted_iota(jnp.int32, (total, total), 0)
        s = lax.broadcasted_iota(jnp.int32, (total, total), 1)
        ok = (t >= s) & ((t // rows) == (s // rows))
        for hh in range(heads):
            ch = slice(hh * w, (hh + 1) * w)
            scores = jnp.where(ok, _dot_nt(q_dec[:, ch].astype(BF16), k_inv[:, ch].astype(BF16)), 0.0)
            o_scr[:, ch] = _dot(scores.astype(BF16), v[:, ch].astype(BF16))

    r0 = pl.multiple_of(b * rows, rows)
    ta = lax.broadcasted_iota(jnp.int32, (total, 1), 0)
    mine = (ta >= r0) & (ta < r0 + rows)
    ch = lambda hh: slice(hh * w, (hh + 1) * w)
    states = [st_in_ref[hh] for hh in range(heads)]
    cross = [_dot(qd_scr[pl.ds(r0, rows), ch(hh)].astype(BF16), states[hh].astype(BF16)) for hh in range(heads)]
    kv_t = [_dot_tn(v_scr[:, ch(hh)].astype(BF16), jnp.where(mine, ke_scr[:, ch(hh)], 0.0).astype(BF16))
            for hh in range(heads)]
    for hh in range(heads):
        o_scr[pl.ds(r0, rows), ch(hh)] += cross[hh]
        st_out_ref[hh] = (el_scr[pl.ds(r0, 1), ch(hh)] * states[hh].T + kv_t[hh]).T

    @pl.when(b == pl.num_programs(0) - 1)
    def _():
        outs = []
        for hh in range(heads):
            ch = slice(hh * w, (hh + 1) * w)
            outs.append((_rms_rows(o_scr[:, ch]) * ng_ref[:, ch] * sg_scr[:, ch]).astype(BF16))
        oc = jnp.concatenate(outs, axis=1)
        y_ref[...] = _resid_out(x_ref[...], _dot(oc, wout_ref[...]), g_ref[1:2, :], mod_ref[2])


def _hgrn_sample(x, mod, gains, w_in, norm_g, w_out, lower_bounds, layer, state, rows):
    total, d = x.shape
    bs, heads, dk, dv = state.shape
    st_spec = pl.BlockSpec((None, heads, dk, dv), lambda b: (b, 0, 0, 0))
    scr = pltpu.VMEM((total, d), F32)
    vmem = _nbytes(w_in, w_out, x, x, mod) + 60 * total * d * 4 + (8 << 20)
    return pl.pallas_call(
        functools.partial(_hgrn_sample_kernel, layer=layer, rows=rows),
        grid=(bs,),
        in_specs=[_const_spec(x.shape), _const_spec(mod.shape), _const_spec(gains.shape),
                  _const_spec(w_in.shape), _const_spec(lower_bounds.shape), _const_spec((1, d)),
                  st_spec, _const_spec(w_out.shape)],
        out_specs=[pl.BlockSpec(x.shape, lambda b: (0, 0)), st_spec],
        out_shape=[jax.ShapeDtypeStruct(x.shape, F32), jax.ShapeDtypeStruct(state.shape, F32)],
        scratch_shapes=[scr, scr, scr, scr, scr, scr],
        compiler_params=_params(1, vmem),
        name="hgrn_sample",
    )(x, mod, gains, w_in, lower_bounds, norm_g.reshape(1, d), state, w_out)


def kernel(x_prompt, x_sample, cache_k_diff, cache_v_diff, state_retention, state_hgrn, c_prompt, c_sample, w_ada, b_ada, norm_gains, gmlp_w_in, gmlp_ln_g, gmlp_ln_b, gmlp_w_s, gmlp_b_s, gmlp_w_out, diff_w_in, diff_lambda, diff_subln, diff_w_out, ret_w_in, ret_w_out, hgrn_w_in, hgrn_norm, hgrn_w_out, hgrn_lower_bounds, ffn_w_in, ffn_w_out):
    bp, s, d = x_prompt.shape
    bs, ls, _ = x_sample.shape
    ms = bs * ls
    depth = w_ada.shape[0]
    past = cache_k_diff.shape[2]
    n_mix = 4

    m_all = _ada(jnp.concatenate([c_prompt, c_sample], axis=0), w_ada, b_ada)
    ffn_in16, ffn_out16 = ffn_w_in.astype(BF16), ffn_w_out.astype(BF16)
    yp = x_prompt
    ys = x_sample.reshape(1, ms, d)
    outs = {name: [] for name in ("gv", "kp", "vp", "ks", "vs", "rp", "rs", "hp", "hs")}
    for i in range(depth):
        kind, j = i % n_mix, i // n_mix
        m = m_all[i].reshape(bp + bs, 6, d)
        mod_p = m[:bp].reshape(bp, 6, 1, d)
        mod_s = jnp.repeat(m[bp:], ls, axis=0).transpose(1, 0, 2).reshape(1, 6, ms, d)
        gains = norm_gains[i]
        if kind == 0:
            w_in, w_out = gmlp_w_in[j].astype(BF16), gmlp_w_out[j].astype(BF16)
            args = (gmlp_ln_g[j], gmlp_ln_b[j], gmlp_w_s[j], gmlp_b_s[j], w_out)
            yp = _gmlp(yp, mod_p, gains, w_in, *args, tm=512, sub=SUB_ROWS, t_chunk=GMLP_CHUNK, emit_v=False)
            ys, v_rows = _gmlp(ys, mod_s, gains, w_in, *args, tm=ms, sub=ms, t_chunk=ls, emit_v=True)
            outs["gv"].append(v_rows.reshape(bs, ls, -1))
        elif kind == 1:
            lam_init = 0.8 - 0.6 * math.exp(-0.3 * i)
            w_in, w_out = diff_w_in[j].astype(BF16), diff_w_out[j].astype(BF16)
            hk, hv = 2 * DIFF_HEADS, DIFF_HEADS
            scale = (d // hk) ** -0.5
            q_t, k, v, k16, v_t, qn2, kn2 = _qkv(yp, mod_p, gains, w_in, tm=512,
                                                 scale=scale * math.log2(math.e), transposed=True)
            outs["kp"].append(k.reshape(bp, s, hk, d // hk))
            outs["vp"].append(v.reshape(bp, s, hv, d // hv))
            yp = _flash_prompt(q_t, k16, v_t, qn2[:, :, 0, 0], kn2[:, :, 0, 0], yp, mod_p, gains,
                               diff_lambda[j], diff_subln[j], w_out, lam_init, tq=512)
            q, k, v, k16, v16 = _qkv(ys, mod_s, gains, w_in, tm=512, scale=scale, transposed=False)
            outs["ks"].append(k.reshape(bs, ls, hk, d // hk))
            outs["vs"].append(v.reshape(bs, ls, hv, d // hv))
            cache_k_t = cache_k_diff[j].transpose(0, 2, 3, 1).reshape(bs, d, past)
            cache_v = cache_v_diff[j].reshape(bs, past * hv, d // hv)
            oc = _flash_sample(q[0], cache_k_t, cache_v, k16[0], v16[0], diff_lambda[j], diff_subln[j],
                               lam_init, rows=ls, tk=1024)
            ys = _outproj(oc[None], ys, mod_s, gains, w_out, tm=512)
        elif kind == 2:
            w_in, w_out = ret_w_in[j].astype(BF16), ret_w_out[j].astype(BF16)
            yp, st = _ret_prompt(yp, mod_p, gains, w_in, w_out, tm=512)
            outs["rp"].append(st)
            y2, st = _ret_sample(ys[0], mod_s[0, :3], gains, w_in, w_out, state_retention[j], rows=ls, past=past)
            ys = y2[None]
            outs["rs"].append(st)
        else:
            w_in, w_out = hgrn_w_in[j].astype(BF16), hgrn_w_out[j].astype(BF16)
            yp, st = _hgrn_prompt(yp, mod_p, gains, w_in, hgrn_norm[j], w_out, hgrn_lower_bounds, i, tm=512)
            outs["hp"].append(st)
            y2, st = _hgrn_sample(ys[0], mod_s[0, :3], gains, w_in, hgrn_norm[j], w_out, hgrn_lower_bounds, i,
                                  state_hgrn[j], rows=ls)
            ys = y2[None]
            outs["hs"].append(st)
        yp = _ffn(yp, mod_p, gains, ffn_in16, ffn_out16, i, tm=1024, sub=SUB_ROWS)
        ys = _ffn(ys, mod_s, gains, ffn_in16, ffn_out16, i, tm=512, sub=SUB_ROWS)

    return (yp, ys.reshape(bs, ls, d), jnp.stack(outs["gv"]), jnp.stack(outs["kp"]), jnp.stack(outs["vp"]),
            jnp.stack(outs["ks"]), jnp.stack(outs["vs"]), jnp.stack(outs["rp"]), jnp.stack(outs["rs"]),
            jnp.stack(outs["hp"]), jnp.stack(outs["hs"]))
```

```python
import functools
import math

import numpy as np
import jax
import jax.numpy as jnp
from jax import lax
from jax.experimental import pallas as pl
from jax.experimental.pallas import tpu as pltpu

F32 = jnp.float32
BF16 = jnp.bfloat16
EPS = 1e-6
NEG_INF = -1e30

CHUNK = 64
GMLP_CHUNK = 128
DIFF_HEADS = 8
RET_HEADS = 4
HG_WIDTH = 128
LANES = 128
MXU_WIDTH = 256
SUB_ROWS = 256
VMEM_CAP = 60 << 20
SOFTMAX_SAFE_LOG2 = 110.0

_NT = (((1,), (1,)), ((), ()))
_TN = (((0,), (0,)), ((), ()))


def _dot(a, b):
    return jnp.dot(a, b, preferred_element_type=F32)


def _dot_nt(a, b):
    return lax.dot_general(a, b, _NT, preferred_element_type=F32)


def _dot_tn(a, b):
    return lax.dot_general(a, b, _TN, preferred_element_type=F32)


def _silu(x):
    return x * jax.nn.sigmoid(x)


def _gelu(x):
    return 0.5 * x * (1.0 + lax.erf(x * (2.0 ** -0.5)))


def _rms_rows(x):
    return x * lax.rsqrt(jnp.mean(x * x, axis=-1, keepdims=True) + EPS)


def _mod_in(x, g, mod_ref, k_shift, k_scale):
    return _rms_rows(x) * g * (1.0 + mod_ref[k_scale]) + mod_ref[k_shift]


def _resid_out(x, o, g, gate):
    return x + gate * (_rms_rows(o) * g)


def _params(n_grid, vmem_bytes):
    return pltpu.CompilerParams(
        dimension_semantics=("arbitrary",) * n_grid,
        vmem_limit_bytes=int(min(max(vmem_bytes, 32 << 20), VMEM_CAP)))


def _const_spec(shape):
    nd = len(shape)
    return pl.BlockSpec(shape, lambda *_: (0,) * nd, pipeline_mode=pl.Buffered(1))


def _row_specs(x, mod, tm):
    _, _, d = x.shape
    r = mod.shape[2]
    x_spec = pl.BlockSpec((None, tm, d), lambda b, i, *_: (b, i, 0))
    if r == 1:
        mod_spec = pl.BlockSpec((None, 6, 1, d), lambda b, i, *_: (b, 0, 0, 0))
    else:
        mod_spec = pl.BlockSpec((None, 6, tm, d), lambda b, i, *_: (b, 0, i, 0))
    return x_spec, mod_spec


def _interleave(*stages):
    live = list(stages)
    while live:
        for g in list(live):
            try:
                next(g)
            except StopIteration:
                live.remove(g)


def _nbytes(*arrays):
    return sum(int(np.prod(a.shape)) * jnp.dtype(a.dtype).itemsize for a in arrays)


def _ada_kernel(c_ref, w_ref, b_ref, o_ref):
    a = _silu(c_ref[...]).astype(BF16)
    o_ref[...] = _dot(a, w_ref[...].astype(BF16)) + b_ref[...]


def _ada(c_all, w_ada, b_ada):
    depth, d, n = w_ada.shape
    rows = c_all.shape[0]
    tn = n // 4
    return pl.pallas_call(
        _ada_kernel,
        grid=(depth, n // tn),
        in_specs=[pl.BlockSpec((rows, d), lambda l, j: (0, 0)),
                  pl.BlockSpec((None, d, tn), lambda l, j: (l, 0, j)),
                  pl.BlockSpec((None, 1, tn), lambda l, j: (l, 0, j))],
        out_specs=pl.BlockSpec((None, rows, tn), lambda l, j: (l, 0, j)),
        out_shape=jax.ShapeDtypeStruct((depth, rows, n), F32),
        compiler_params=_params(2, 3 * d * tn * 4 + (8 << 20)),
        name="ada_mod",
    )(c_all, w_ada, b_ada.reshape(depth, 1, n))


def _ffn_kernel(x_ref, mod_ref, g_ref, win_ref, wout_ref, y_ref, *, hidden, chunks, sub):
    rows = x_ref.shape[0]
    per_row = mod_ref.shape[1] != 1
    for r0 in range(0, rows, sub):
        rs = slice(r0, r0 + sub)
        x = x_ref[rs, :]
        shift, scale, gate_out = (mod_ref[k, rs, :] if per_row else mod_ref[k] for k in (3, 4, 5))
        h = (_rms_rows(x) * g_ref[2:3, :] * (1.0 + scale) + shift).astype(BF16)
        acc = None
        for c0, cw in chunks:
            gate = _dot(h, win_ref[:, c0:c0 + cw])
            up = _dot(h, win_ref[:, hidden + c0:hidden + c0 + cw])
            act = (_silu(gate) * up).astype(BF16)
            part = _dot(act, wout_ref[c0:c0 + cw, :])
            acc = part if acc is None else acc + part
        y_ref[rs, :] = _resid_out(x, acc, g_ref[3:4, :], gate_out)


def _split_chunks(total, width):
    out, c0 = [], 0
    while c0 < total:
        out.append((c0, min(width, total - c0)))
        c0 += width
    return tuple(out)


def _layer_spec(shape, layer):
    nd = len(shape) - 1
    return pl.BlockSpec((None,) + tuple(shape[1:]), lambda *_: (layer,) + (0,) * nd,
                        pipeline_mode=pl.Buffered(1))


def _ffn(x, mod, gains, w_in, w_out, layer, tm, sub):
    b, s, d = x.shape
    hidden = w_out.shape[1]
    tm = min(tm, s)
    sub = min(sub, tm)
    x_spec, mod_spec = _row_specs(x, mod, tm)
    vmem = _nbytes(w_in[0], w_out[0]) + 6 * tm * d * 4 + 4 * tm * 1024 * 4 + (8 << 20)
    return pl.pallas_call(
        functools.partial(_ffn_kernel, hidden=hidden, chunks=_split_chunks(hidden, 4 * MXU_WIDTH), sub=sub),
        grid=(b, s // tm),
        in_specs=[x_spec, mod_spec, _const_spec(gains.shape), _layer_spec(w_in.shape, layer),
                  _layer_spec(w_out.shape, layer)],
        out_specs=x_spec,
        out_shape=jax.ShapeDtypeStruct(x.shape, F32),
        compiler_params=_params(2, vmem),
        name="ffn",
    )(x, mod, gains, w_in, w_out)


def _gmlp_kernel(x_ref, mod_ref, g_ref, win_ref, lng_ref, lnb_ref, wbd_ref, brow_ref, wout_ref,
                 *out_and_scratch, half, groups, pair, emit_v):
    if emit_v:
        y_ref, vn_ref, v_scr = out_and_scratch
    else:
        y_ref, v_scr = out_and_scratch
    gw = half // groups
    cw = pair * gw
    nblk = half // cw
    rows = x_ref.shape[0]
    sub = wbd_ref.shape[1]
    x = x_ref[...]
    h_all = _mod_in(x, g_ref[0:1, :], mod_ref, 0, 1).astype(BF16)
    for r0 in range(0, rows, sub):
        rs = slice(r0, r0 + sub)
        h = h_all[rs]
        s1 = None
        s2 = None
        for j in range(nblk):
            v = _gelu(_dot(h, win_ref[:, half + j * cw:half + (j + 1) * cw]))
            v_scr[rs, j * cw:(j + 1) * cw] = v
            a1 = jnp.sum(v, axis=-1, keepdims=True)
            a2 = jnp.sum(v * v, axis=-1, keepdims=True)
            s1 = a1 if s1 is None else s1 + a1
            s2 = a2 if s2 is None else s2 + a2
        mu = s1 * (1.0 / half)
        rstd = lax.rsqrt(s2 * (1.0 / half) - mu * mu + EPS)
        acc = None
        for j in range(nblk):
            cols = slice(j * cw, (j + 1) * cw)
            vn = (v_scr[rs, cols] - mu) * rstd * lng_ref[:, cols] + lnb_ref[:, cols]
            if emit_v:
                vn_ref[rs, cols] = vn
            vnb = vn.astype(BF16)
            u = _gelu(_dot(h, win_ref[:, cols]))
            mixed = []
            for gg in range(pair):
                g = j * pair + gg
                mixed.append(_dot(wbd_ref[g], vnb[:, gg * gw:(gg + 1) * gw]) + brow_ref[:, g:g + 1])
            out = (u * jnp.concatenate(mixed, axis=1)).astype(BF16)
            part = _dot(out, wout_ref[cols, :])
            acc = part if acc is None else acc + part
        gate = mod_ref[2] if mod_ref.shape[1] == 1 else mod_ref[2, rs, :]
        y_ref[rs, :] = _resid_out(x[rs], acc, g_ref[1:2, :], gate)


def _gmlp(x, mod, gains, w_in, ln_g, ln_b, w_s, b_s, w_out, tm, sub, t_chunk, emit_v):
    b, s, d = x.shape
    half = w_out.shape[0]
    groups = w_s.shape[0]
    tm = min(tm, s)
    sub = min(sub, tm)
    n_rep = sub // t_chunk
    pos = np.arange(sub)
    expand = jnp.asarray(pos[:, None] % t_chunk == np.arange(t_chunk)[None, :], w_s.dtype)
    same = jnp.asarray(pos[:, None] // t_chunk == pos[None, :] // t_chunk)
    wt = jnp.tril(w_s[:, :t_chunk, :t_chunk])
    w_bd = jnp.einsum("rt,gts,cs->grc", expand, wt, expand, precision=lax.Precision.HIGHEST)
    w_bd = jnp.where(same[None], w_bd, 0.0).astype(BF16)
    b_rows = jnp.tile(b_s[:, :t_chunk].T, (n_rep, 1))
    x_spec, mod_spec = _row_specs(x, mod, tm)
    out_shape = [jax.ShapeDtypeStruct(x.shape, F32)]
    out_specs = [x_spec]
    if emit_v:
        out_shape.append(jax.ShapeDtypeStruct((b, s, half), F32))
        out_specs.append(pl.BlockSpec((None, tm, half), lambda bb, i: (bb, i, 0)))
    vmem = (_nbytes(w_in, w_out, w_bd) + tm * half * 4 * (5 if emit_v else 1)
            + 6 * tm * d * 4 + (12 << 20))
    res = pl.pallas_call(
        functools.partial(_gmlp_kernel, half=half, groups=groups, pair=2, emit_v=emit_v),
        grid=(b, s // tm),
        in_specs=[x_spec, mod_spec, _const_spec(gains.shape), _const_spec(w_in.shape),
                  _const_spec((1, half)), _const_spec((1, half)), _const_spec(w_bd.shape),
                  _const_spec(b_rows.shape), _const_spec(w_out.shape)],
        out_specs=out_specs,
        out_shape=out_shape,
        scratch_shapes=[pltpu.VMEM((tm, half), F32)],
        compiler_params=_params(2, vmem),
        name="gmlp_v" if emit_v else "gmlp",
    )(x, mod, gains, w_in, ln_g.reshape(1, half), ln_b.reshape(1, half), w_bd, b_rows, w_out)
    return res if emit_v else res[0]


def _outproj_kernel(o_ref, x_ref, mod_ref, g_ref, w_ref, y_ref):
    y_ref[...] = _resid_out(x_ref[...], _dot(o_ref[...], w_ref[...]), g_ref[1:2, :], mod_ref[2])


def _outproj(o, x, mod, gains, w_out, tm):
    b, s, d = x.shape
    tm = min(tm, s)
    k = o.shape[-1]
    x_spec, mod_spec = _row_specs(x, mod, tm)
    return pl.pallas_call(
        _outproj_kernel,
        grid=(b, s // tm),
        in_specs=[pl.BlockSpec((None, tm, k), lambda bb, i: (bb, i, 0)), x_spec, mod_spec,
                  _const_spec(gains.shape), _const_spec(w_out.shape)],
        out_specs=x_spec,
        out_shape=jax.ShapeDtypeStruct(x.shape, F32),
        compiler_params=_params(2, _nbytes(w_out) + 8 * tm * d * 4 + 2 * tm * k * 2 + (8 << 20)),
        name="outproj",
    )(o, x, mod, gains, w_out)


def _max_head_norm2(x_t, heads):
    d, rows = x_t.shape
    n2 = jnp.sum((x_t * x_t).reshape(heads, d // heads, rows), axis=1)
    return jnp.full((8, LANES), jnp.max(n2), F32)


def _qkv_kernel(x_ref, mod_ref, g_ref, w_ref, q_ref, k_ref, v_ref, k16_ref, v16_ref, *norm_refs,
                scale, transposed):
    d = x_ref.shape[-1]
    h = _mod_in(x_ref[...], g_ref[0:1, :], mod_ref, 0, 1).astype(BF16)
    q = _dot(h, w_ref[:, :d]) * scale
    k = _dot(h, w_ref[:, d:2 * d])
    k_ref[...] = k
    k16_ref[...] = k.astype(BF16)
    v = _dot(h, w_ref[:, 2 * d:])
    v_ref[...] = v
    if transposed:
        qn_ref, kn_ref, vmax_ref = norm_refs
        q_t = q.T
        q_ref[...] = q_t.astype(BF16)
        v16_ref[...] = v.T.astype(BF16)
        qn_ref[...] = _max_head_norm2(q_t, 2 * DIFF_HEADS)
        kn_ref[...] = _max_head_norm2(k.T, 2 * DIFF_HEADS)
        vmax_ref[...] = jnp.full((8, LANES), jnp.max(jnp.abs(v)), F32)
    else:
        q_ref[...] = q.astype(BF16)
        v16_ref[...] = v.astype(BF16)


def _qkv(x, mod, gains, w_in, tm, scale, transposed):
    b, s, d = x.shape
    tm = min(tm, s)
    x_spec, mod_spec = _row_specs(x, mod, tm)
    sds = jax.ShapeDtypeStruct
    out_specs = [x_spec, x_spec, x_spec, x_spec, x_spec]
    out_shape = [sds(x.shape, BF16), sds(x.shape, F32), sds(x.shape, F32), sds(x.shape, BF16),
                 sds(x.shape, BF16)]
    if transposed:
        t_spec = pl.BlockSpec((None, d, tm), lambda bb, i: (bb, 0, i))
        n_spec = pl.BlockSpec((None, None, 8, LANES), lambda bb, i: (bb, i, 0, 0))
        out_specs[0] = out_specs[4] = t_spec
        out_shape[0] = out_shape[4] = sds((b, d, s), BF16)
        out_specs += [n_spec, n_spec, n_spec]
        out_shape += [sds((b, s // tm, 8, LANES), F32)] * 3
    return pl.pallas_call(
        functools.partial(_qkv_kernel, scale=scale, transposed=transposed),
        grid=(b, s // tm),
        in_specs=[x_spec, mod_spec, _const_spec(gains.shape), _const_spec(w_in.shape)],
        out_specs=out_specs,
        out_shape=out_shape,
        compiler_params=_params(2, _nbytes(w_in) + 24 * tm * d * 4 + (8 << 20)),
        name="diff_qkv_t" if transposed else "diff_qkv",
    )(x, mod, gains, w_in)


def _diff_lambda(lam_ref, lam_init):
    lp = lam_ref[...]
    e1 = jnp.exp(jnp.sum(lp[0:1, :] * lp[1:2, :], axis=-1, keepdims=True))
    e2 = jnp.exp(jnp.sum(lp[2:3, :] * lp[3:4, :], axis=-1, keepdims=True))
    return e1 - e2 + lam_init


def _stack_q_halves(q, qs_scr, rows):
    lane = lax.broadcasted_iota(jnp.int32, q.shape, 1)
    first = (lane & (LANES - 1)) < (LANES // 2)
    zero = jnp.zeros_like(q)
    qs_scr[0:rows, :] = jnp.where(first, q, zero)
    qs_scr[rows:2 * rows, :] = jnp.where(first, zero, q)


def _flash_init(m_scr, l_scr, acc_scr):
    m_scr[...] = jnp.full(m_scr.shape, NEG_INF, F32)
    l_scr[...] = jnp.zeros(l_scr.shape, F32)
    acc_scr[...] = jnp.zeros(acc_scr.shape, F32)


def _flash_step(qs_scr, scores, v_blk, m_scr, l_scr, acc_scr, mask):
    sl = lambda h: slice(LANES * h, LANES * (h + 1))
    s_all = [scores(h, qs_scr[:, sl(h)]) for h in range(DIFF_HEADS)]
    stats = []
    for h, s in enumerate(s_all):
        if mask is not None:
            s = jnp.where(mask, s, NEG_INF)
        m_old = m_scr[h]
        m_new = jnp.maximum(m_old, jnp.max(s, axis=1, keepdims=True))
        alpha = jnp.exp(m_old - m_new)
        p = jnp.exp(s - m_new)
        l_scr[h] = alpha * l_scr[h] + jnp.sum(p, axis=1, keepdims=True)
        m_scr[h] = m_new
        stats.append((alpha, p.astype(BF16)))
    for h, (alpha, p) in enumerate(stats):
        acc_scr[:, sl(h)] = alpha * acc_scr[:, sl(h)] + _dot(p, v_blk(h))


def _flash_finish(l_scr, acc_scr, lam, subln, lam_init, rows):
    outs = []
    for h in range(DIFF_HEADS):
        sl = slice(LANES * h, LANES * (h + 1))
        l = l_scr[h]
        o = acc_scr[0:rows, sl] / l[0:rows] - lam * (acc_scr[rows:2 * rows, sl] / l[rows:2 * rows])
        outs.append((_rms_rows(o) * subln * (1.0 - lam_init)).astype(BF16))
    return jnp.concatenate(outs, axis=1)


def _chunk_mask(rows, cols, row_pos0, col_pos0):
    r = lax.broadcasted_iota(jnp.int32, (2 * rows, cols), 0)
    c = lax.broadcasted_iota(jnp.int32, (2 * rows, cols), 1)
    r = jnp.where(r >= rows, r - rows, r)
    return ((c + col_pos0) // CHUNK) <= ((r + row_pos0) // CHUNK)


def _flash_t_step(qz_scr, k_ref, vt_ref, m_scr, l_scr, acc_scr, mask, bounded=False):
    heads = [(h, c) for h in range(DIFF_HEADS) for c in range(2)]
    rows = lambda h: slice(LANES * h, LANES * (h + 1))

    def scores(i):
        h, c = heads[i]
        return _dot(k_ref[:, rows(h)], qz_scr[c, rows(h), :])

    def softmax(i, s):
        r = 2 * heads[i][0] + heads[i][1]
        if mask is not None:
            s = jnp.where(mask, s, NEG_INF)
        m_old = m_scr[r:r + 1, :]
        block_ref = 0.0 if bounded else jnp.max(s, axis=0, keepdims=True)
        m_new = jnp.maximum(m_old, block_ref)
        alpha = jnp.exp2(m_old - m_new)
        p = jnp.exp2(s - m_new)
        l_scr[r:r + 1, :] = alpha * l_scr[r:r + 1, :] + jnp.sum(p, axis=0, keepdims=True)
        m_scr[r:r + 1, :] = m_new
        return alpha, p.astype(BF16)

    def accumulate(i, alpha, p):
        h, c = heads[i]
        acc_scr[c, rows(h), :] = alpha * acc_scr[c, rows(h), :] + _dot(vt_ref[rows(h), :], p)

    n = len(heads)
    ahead, behind = 2, 1
    pending_s = {i: scores(i) for i in range(ahead)}
    pending_p = {}
    for i in range(n):
        if i + ahead < n:
            pending_s[i + ahead] = scores(i + ahead)
        pending_p[i] = softmax(i, pending_s.pop(i))
        if i - behind >= 0:
            accumulate(i - behind, *pending_p.pop(i - behind))
    for i in sorted(pending_p):
        accumulate(i, *pending_p[i])


def _flash_prompt_kernel(qt_ref, kt_ref, safe_ref, q_ref, k_ref, vt_ref, x_ref, mod_ref, g_ref, lam_ref,
                         subln_ref, woutt_ref, y_ref, qz_scr, m_scr, l_scr, acc_scr, *, lam_init):
    p = pl.program_id(1)
    qi = qt_ref[p]
    ki = kt_ref[p]
    bounded = safe_ref[pl.program_id(0) * pl.num_programs(1) + p] != 0
    tk = k_ref.shape[0]
    tq = q_ref.shape[1]

    @pl.when(ki == 0)
    def _():
        m_scr[...] = jnp.full(m_scr.shape, NEG_INF, F32)
        l_scr[...] = jnp.zeros(l_scr.shape, F32)
        acc_scr[...] = jnp.zeros(acc_scr.shape, F32)
        q = q_ref[...]
        feat = lax.broadcasted_iota(jnp.int32, q.shape, 0)
        first = (feat & (LANES - 1)) < (LANES // 2)
        zero = jnp.zeros_like(q)
        qz_scr[0] = jnp.where(first, q, zero)
        qz_scr[1] = jnp.where(first, zero, q)

    @pl.when((ki < qi) & bounded)
    def _():
        _flash_t_step(qz_scr, k_ref, vt_ref, m_scr, l_scr, acc_scr, None, bounded=True)

    @pl.when((ki < qi) & jnp.logical_not(bounded))
    def _():
        _flash_t_step(qz_scr, k_ref, vt_ref, m_scr, l_scr, acc_scr, None)

    def chunk_causal():
        key = lax.broadcasted_iota(jnp.int32, (tk, tq), 0)
        qry = lax.broadcasted_iota(jnp.int32, (tk, tq), 1)
        return (key // CHUNK) <= (qry // CHUNK)

    @pl.when((ki == qi) & bounded)
    def _():
        _flash_t_step(qz_scr, k_ref, vt_ref, m_scr, l_scr, acc_scr, chunk_causal(), bounded=True)

    @pl.when((ki == qi) & jnp.logical_not(bounded))
    def _():
        _flash_t_step(qz_scr, k_ref, vt_ref, m_scr, l_scr, acc_scr, chunk_causal())

    @pl.when(ki == qi)
    def _():
        lam = _diff_lambda(lam_ref, lam_init)
        outs = []
        for h in range(DIFF_HEADS):
            rows = slice(LANES * h, LANES * (h + 1))
            o = (acc_scr[0, rows, :] / l_scr[2 * h:2 * h + 1, :]
                 - lam * (acc_scr[1, rows, :] / l_scr[2 * h + 1:2 * h + 2, :]))
            o = o * lax.rsqrt(jnp.mean(o * o, axis=0, keepdims=True) + EPS)
            outs.append((o * (subln_ref[...] * (1.0 - lam_init))).astype(BF16))
        out_t = _dot(woutt_ref[...], jnp.concatenate(outs, axis=0))
        y_ref[...] = _resid_out(x_ref[...], out_t.T, g_ref[1:2, :], mod_ref[2])


def _flash_prompt(q_t, k16, v_t, qn2, kn2, vmax, x, mod, gains, lam_p, subln, w_out, lam_init, tq):
    b, s, d = x.shape
    tq = min(tq, s)
    nq = s // tq
    assert qn2.shape == kn2.shape == vmax.shape == (b, nq)
    pairs = [(qi, ki) for qi in range(nq) for ki in range(qi + 1)]
    qt = jnp.asarray([pq for pq, _ in pairs], jnp.int32)
    kt = jnp.asarray([pk for _, pk in pairs], jnp.int32)
    limit = SOFTMAX_SAFE_LOG2 - math.log2(s) - jnp.log2(jnp.maximum(jnp.max(vmax, axis=1, keepdims=True), 1.0))
    safe = (limit > 0) & (qn2[:, qt] * kn2[:, kt] <= limit * limit)
    safe = safe.astype(jnp.int32).reshape(-1)
    x_spec = pl.BlockSpec((None, tq, d), lambda bb, p, qt_, kt_, safe_: (bb, qt_[p], 0))
    q_spec = pl.BlockSpec((None, d, tq), lambda bb, p, qt_, kt_, safe_: (bb, 0, qt_[p]))
    k_spec = pl.BlockSpec((None, tq, d), lambda bb, p, qt_, kt_, safe_: (bb, kt_[p], 0))
    v_spec = pl.BlockSpec((None, d, tq), lambda bb, p, qt_, kt_, safe_: (bb, 0, kt_[p]))
    mod_spec = pl.BlockSpec((None, 6, 1, d), lambda bb, p, qt_, kt_, safe_: (bb, 0, 0, 0))
    w_out_t = w_out.T
    vmem = (_nbytes(w_out) + 6 * tq * d * 2 + 4 * tq * d * 4 + 2 * tq * d * 2 + 2 * tq * d * 4
            + 8 * tq * tq * 4 + (8 << 20))
    grid_spec = pltpu.PrefetchScalarGridSpec(
        num_scalar_prefetch=3,
        grid=(b, len(pairs)),
        in_specs=[q_spec, k_spec, v_spec, x_spec, mod_spec, _const_spec(gains.shape),
                  _const_spec(lam_p.shape), _const_spec((LANES, 1)), _const_spec(w_out_t.shape)],
        out_specs=x_spec,
        scratch_shapes=[pltpu.VMEM((2, d, tq), BF16),
                        pltpu.VMEM((2 * DIFF_HEADS, tq), F32),
                        pltpu.VMEM((2 * DIFF_HEADS, tq), F32),
                        pltpu.VMEM((2, d, tq), F32)])
    return pl.pallas_call(
        functools.partial(_flash_prompt_kernel, lam_init=lam_init),
        grid_spec=grid_spec,
        out_shape=jax.ShapeDtypeStruct(x.shape, F32),
        compiler_params=_params(2, vmem),
        name="diff_flash_prompt",
    )(qt, kt, safe, q_t, k16, v_t, x, mod, gains, lam_p, subln.reshape(LANES, 1), w_out_t)


def _flash_sample_kernel(q_ref, ck_ref, cv_ref, kn_ref, vn_ref, lam_ref, subln_ref, o_ref,
                         qs_scr, kpad_scr, vpad_scr, m_scr, l_scr, acc_scr,
                         *, rows, past, lam_init, new_mask_needed):
    kb = pl.program_id(1)

    @pl.when(kb == 0)
    def _():
        _flash_init(m_scr, l_scr, acc_scr)
        _stack_q_halves(q_ref[...], qs_scr, rows)
        kpad_scr[...] = jnp.zeros(kpad_scr.shape, BF16)
        vpad_scr[...] = jnp.zeros(vpad_scr.shape, BF16)
        kpad_scr[0:rows, :] = kn_ref[...]
        vpad_scr[0:rows, :] = vn_ref[...]
        c = lax.broadcasted_iota(jnp.int32, (2 * rows, LANES), 1)
        mask = c < rows
        if new_mask_needed:
            mask = mask & _chunk_mask(rows, LANES, past, past)
        _flash_step(qs_scr,
                    lambda h, qp: _dot_nt(qp, kpad_scr[:, LANES * h:LANES * (h + 1)]),
                    lambda h: vpad_scr[:, LANES * h:LANES * (h + 1)],
                    m_scr, l_scr, acc_scr, mask)

    tk = ck_ref.shape[1]
    _flash_step(qs_scr,
                lambda h, qp: _dot(qp, ck_ref[LANES * h:LANES * (h + 1), :].astype(BF16)),
                lambda h: cv_ref[pl.ds(h, tk, stride=DIFF_HEADS), :].astype(BF16),
                m_scr, l_scr, acc_scr, None)

    @pl.when(kb == pl.num_programs(1) - 1)
    def _():
        lam = _diff_lambda(lam_ref, lam_init)
        o_ref[...] = _flash_finish(l_scr, acc_scr, lam, subln_ref[...], lam_init, rows)


def _flash_sample(q, cache_k_t, cache_v, k_new, v_new, lam_p, subln, lam_init, rows, tk):
    bs, d, past = cache_k_t.shape
    tk = min(tk, past)
    pos = past + np.arange(rows)
    new_mask_needed = not bool(np.all((pos[None, :] // CHUNK) <= (pos[:, None] // CHUNK)))
    row_spec = pl.BlockSpec((rows, d), lambda b, kb: (b, 0))
    k_spec = pl.BlockSpec((None, d, tk), lambda b, kb: (b, 0, kb))
    v_spec = pl.BlockSpec((None, tk * DIFF_HEADS, LANES), lambda b, kb: (b, kb, 0))
    vmem = 4 * tk * d * 4 + 4 * tk * d * 2 + (12 << 20)
    return pl.pallas_call(
        functools.partial(_flash_sample_kernel, rows=rows, past=past, lam_init=lam_init,
                          new_mask_needed=new_mask_needed),
        grid=(bs, past // tk),
        in_specs=[row_spec, k_spec, v_spec, row_spec, row_spec,
                  _const_spec(lam_p.shape), _const_spec((1, LANES))],
        out_specs=row_spec,
        out_shape=jax.ShapeDtypeStruct(q.shape, BF16),
        scratch_shapes=[pltpu.VMEM((2 * rows, d), BF16),
                        pltpu.VMEM((LANES, d), BF16), pltpu.VMEM((LANES, d), BF16),
                        pltpu.VMEM((DIFF_HEADS, 2 * rows, 1), F32),
                        pltpu.VMEM((DIFF_HEADS, 2 * rows, 1), F32),
                        pltpu.VMEM((2 * rows, d), F32)],
        compiler_params=_params(2, vmem),
        name="diff_flash_sample",
    )(q, cache_k_t, cache_v, k_new, v_new, lam_p, subln.reshape(1, LANES))


def _ret_log_gamma(h):
    return float(np.log(np.float32(1.0) - np.float32(2.0) ** np.float32(-5.0 - h)))


def _rotate_pairs(x):
    n = x.shape[-1]
    lane = lax.broadcasted_iota(jnp.int32, x.shape, 1)
    return jnp.where((lane & 1) == 0, -pltpu.roll(x, n - 1, 1), pltpu.roll(x, 1, 1))


def _ret_project(h, win_ref, cos, sin, d, dk):
    cos4 = jnp.concatenate([cos] * (d // dk), axis=1)
    sin4 = jnp.concatenate([sin] * (d // dk), axis=1)
    q = _dot(h, win_ref[:, 0:d])
    q = q * cos4 + _rotate_pairs(q) * sin4
    k = _dot(h, win_ref[:, d:2 * d])
    k = (k * cos4 + _rotate_pairs(k) * sin4) * (dk ** -0.5)
    v = _dot(h, win_ref[:, 2 * d:4 * d])
    return q, k, v


def _ret_decay(lg, rows, same_seq=None):
    t = lax.broadcasted_iota(jnp.int32, (rows, rows), 0)
    s = lax.broadcasted_iota(jnp.int32, (rows, rows), 1)
    ok = t >= s
    if same_seq is not None:
        ok = ok & ((t // same_seq) == (s // same_seq))
    diff = jnp.maximum(t - s, 0).astype(F32)
    return jnp.where(ok, jnp.exp(lg * diff), 0.0)


def _ret_prompt_kernel(x_ref, mod_ref, g_ref, win_ref, cos_ref, sin_ref, wout_ref, y_ref, st_ref,
                       state_scr, o_scr, *, heads):
    i = pl.program_id(1)
    rows, d = x_ref.shape
    dk = d // heads
    dv = 2 * dk

    @pl.when(i == 0)
    def _():
        state_scr[...] = jnp.zeros(state_scr.shape, F32)

    x = x_ref[...]
    h = _mod_in(x, g_ref[0:1, :], mod_ref, 0, 1).astype(BF16)
    sub = min(rows, SUB_ROWS)
    t = lax.broadcasted_iota(jnp.int32, (sub, 1), 0).astype(F32)
    lgs = [_ret_log_gamma(hh) for hh in range(heads)]
    states = [state_scr[hh] for hh in range(heads)]
    pre = {}

    def project(r0):
        hs = h[r0:r0 + sub]
        parts = []
        for c0, c1 in ((0, d), (d, 2 * d), (2 * d, 4 * d), (4 * d, 6 * d)):
            parts.append(_dot(hs, win_ref[:, c0:c1]))
            yield
        pre[r0] = tuple(parts)

    def mix(r0):
        q, k, v, g = pre.pop(r0)
        cos4 = jnp.concatenate([cos_ref[r0:r0 + sub, :]] * heads, axis=1)
        sin4 = jnp.concatenate([sin_ref[r0:r0 + sub, :]] * heads, axis=1)
        q = q * cos4 + _rotate_pairs(q) * sin4
        k = (k * cos4 + _rotate_pairs(k) * sin4) * (dk ** -0.5)
        sg = _silu(g)
        qs = [q[:, hh * dk:(hh + 1) * dk].astype(BF16) for hh in range(heads)]
        ks = [k[:, hh * dk:(hh + 1) * dk] for hh in range(heads)]
        vs = [v[:, hh * dv:(hh + 1) * dv].astype(BF16) for hh in range(heads)]
        yield
        scores = [_dot_nt(qs[hh], ks[hh].astype(BF16)) for hh in range(heads)]
        cross = [_dot(qs[hh], states[hh].astype(BF16)) for hh in range(heads)]
        kv = [_dot_tn((ks[hh] * jnp.exp(lgs[hh] * (sub - 1.0 - t))).astype(BF16), vs[hh])
              for hh in range(heads)]
        for hh in range(heads):
            states[hh] = math.exp(lgs[hh] * sub) * states[hh] + kv[hh]
        yield
        inner = [_dot((scores[hh] * _ret_decay(lgs[hh], sub)).astype(BF16), vs[hh]) for hh in range(heads)]
        yield
        gated = []
        for hh in range(heads):
            o = inner[hh] + cross[hh] * jnp.exp(lgs[hh] * (t + 1.0))
            gated.append((_rms_rows(o) * sg[:, hh * dv:(hh + 1) * dv]).astype(BF16))
        o_scr[r0:r0 + sub, :] = jnp.concatenate(gated, axis=1)

    starts = list(range(0, rows, sub))
    _interleave(project(starts[0]))
    for n, r0 in enumerate(starts):
        nxt = [project(starts[n + 1])] if n + 1 < len(starts) else []
        _interleave(mix(r0), *nxt)
    for hh in range(heads):
        state_scr[hh] = states[hh]
    y_ref[...] = _resid_out(x, _dot(o_scr[...], wout_ref[...]), g_ref[1:2, :], mod_ref[2])

    @pl.when(i == pl.num_programs(1) - 1)
    def _():
        st_ref[...] = state_scr[...]


def _xpos_tables(pos, dk):
    inv = 1.0 / (10000.0 ** jnp.linspace(0.0, 1.0, dk // 2, dtype=F32))
    ang = pos.astype(F32)[:, None] * jnp.repeat(inv, 2)[None, :]
    return jnp.cos(ang), jnp.sin(ang)


def _ret_prompt(x, mod, gains, w_in, w_out, tm):
    b, s, d = x.shape
    tm = min(tm, s)
    heads = RET_HEADS
    dk = d // heads
    dv = 2 * dk
    cos, sin = _xpos_tables(jnp.arange(s), dk)
    x_spec, mod_spec = _row_specs(x, mod, tm)
    tab_spec = pl.BlockSpec((tm, dk), lambda bb, i: (i, 0))
    st_spec = pl.BlockSpec((None, heads, dk, dv), lambda bb, i: (bb, 0, 0, 0))
    vmem = _nbytes(w_in, w_out) + 3 * heads * dk * dv * 4 + 40 * tm * d * 4 + (8 << 20)
    return pl.pallas_call(
        functools.partial(_ret_prompt_kernel, heads=heads),
        grid=(b, s // tm),
        in_specs=[x_spec, mod_spec, _const_spec(gains.shape), _const_spec(w_in.shape),
                  tab_spec, tab_spec, _const_spec(w_out.shape)],
        out_specs=[x_spec, st_spec],
        out_shape=[jax.ShapeDtypeStruct(x.shape, F32),
                   jax.ShapeDtypeStruct((b, heads, dk, dv), F32)],
        scratch_shapes=[pltpu.VMEM((heads, dk, dv), F32), pltpu.VMEM((tm, heads * dv), BF16)],
        compiler_params=_params(2, vmem),
        name="ret_prompt",
    )(x, mod, gains, w_in, cos, sin, w_out)


def _ret_sample_kernel(x_ref, mod_ref, g_ref, win_ref, cos_ref, sin_ref, st_in_ref, wout_ref,
                       y_ref, st_out_ref, q_scr, k_scr, v_scr, o_scr, *, heads, rows):
    b = pl.program_id(0)
    total, d = x_ref.shape
    dk = d // heads
    dv = 2 * dk

    @pl.when(b == 0)
    def _():
        h = _mod_in(x_ref[...], g_ref[0:1, :], mod_ref, 0, 1).astype(BF16)
        q, k, v = _ret_project(h, win_ref, cos_ref[...], sin_ref[...], d, dk)
        q_scr[...] = q.astype(BF16)
        k_scr[...] = k
        v_scr[...] = v.astype(BF16)
        for hh in range(heads):
            scores = (_dot_nt(q[:, hh * dk:(hh + 1) * dk].astype(BF16),
                              k[:, hh * dk:(hh + 1) * dk].astype(BF16))
                      * _ret_decay(_ret_log_gamma(hh), total, same_seq=rows))
            o_scr[:, hh * dv:(hh + 1) * dv] = _dot(scores.astype(BF16),
                                                   v[:, hh * dv:(hh + 1) * dv].astype(BF16))

    r0 = pl.multiple_of(b * rows, rows)
    t = lax.broadcasted_iota(jnp.int32, (rows, 1), 0).astype(F32)
    ta = lax.broadcasted_iota(jnp.int32, (total, 1), 0)
    mine = (ta >= r0) & (ta < r0 + rows)
    t_all = (ta - r0).astype(F32)
    lgs = [_ret_log_gamma(hh) for hh in range(heads)]
    states = [st_in_ref[hh] for hh in range(heads)]
    cross = [_dot(q_scr[pl.ds(r0, rows), hh * dk:(hh + 1) * dk], states[hh].astype(BF16)) for hh in range(heads)]
    kv = [_dot_tn(jnp.where(mine, k_scr[:, hh * dk:(hh + 1) * dk] * jnp.exp(lgs[hh] * (rows - 1.0 - t_all)),
                            0.0).astype(BF16), v_scr[:, hh * dv:(hh + 1) * dv]) for hh in range(heads)]
    for hh in range(heads):
        o_scr[pl.ds(r0, rows), hh * dv:(hh + 1) * dv] += cross[hh] * jnp.exp(lgs[hh] * (t + 1.0))
        st_out_ref[hh] = math.exp(lgs[hh] * rows) * states[hh] + kv[hh]

    @pl.when(b == pl.num_programs(0) - 1)
    def _():
        h = _mod_in(x_ref[...], g_ref[0:1, :], mod_ref, 0, 1).astype(BF16)
        gated = []
        for hh in range(heads):
            sl = slice(hh * dv, (hh + 1) * dv)
            sg = _silu(_dot(h, win_ref[:, 4 * d + hh * dv:4 * d + (hh + 1) * dv]))
            gated.append((_rms_rows(o_scr[:, sl]) * sg).astype(BF16))
        oc = jnp.concatenate(gated, axis=1)
        y_ref[...] = _resid_out(x_ref[...], _dot(oc, wout_ref[...]), g_ref[1:2, :], mod_ref[2])


def _ret_sample(x, mod, gains, w_in, w_out, state, rows, past):
    total, d = x.shape
    bs, heads, dk, dv = state.shape
    cos, sin = _xpos_tables(past + jnp.arange(rows), dk)
    cos = jnp.tile(cos, (bs, 1))
    sin = jnp.tile(sin, (bs, 1))
    st_spec = pl.BlockSpec((None, heads, dk, dv), lambda b: (b, 0, 0, 0))
    vmem = _nbytes(w_in, w_out, x, x, mod) + 4 * heads * dk * dv * 4 + 60 * total * d * 4 + (8 << 20)
    return pl.pallas_call(
        functools.partial(_ret_sample_kernel, heads=heads, rows=rows),
        grid=(bs,),
        in_specs=[_const_spec(x.shape), _const_spec(mod.shape), _const_spec(gains.shape),
                  _const_spec(w_in.shape), _const_spec(cos.shape), _const_spec(sin.shape),
                  st_spec, _const_spec(w_out.shape)],
        out_specs=[pl.BlockSpec(x.shape, lambda b: (0, 0)), st_spec],
        out_shape=[jax.ShapeDtypeStruct(x.shape, F32), jax.ShapeDtypeStruct(state.shape, F32)],
        scratch_shapes=[pltpu.VMEM((total, d), BF16), pltpu.VMEM((total, d), F32),
                        pltpu.VMEM((total, 2 * d), BF16), pltpu.VMEM((total, 2 * d), F32)],
        compiler_params=_params(1, vmem),
        name="ret_sample",
    )(x, mod, gains, w_in, cos, sin, state, w_out)


def _hgrn_lower(lb_ref, layer):
    lb = lb_ref[...]
    e = jnp.exp(lb - jnp.max(lb, axis=0, keepdims=True))
    p = e / jnp.sum(e, axis=0, keepdims=True)
    if layer == 0:
        return jnp.zeros_like(p[0:1, :])
    return jnp.sum(p[1:layer + 1, :], axis=0, keepdims=True)


def _block_cumsum(x, block):
    rows = x.shape[0]
    t = lax.broadcasted_iota(jnp.int32, (rows, rows), 0)
    s = lax.broadcasted_iota(jnp.int32, (rows, rows), 1)
    tri = jnp.where((s <= t) & ((t // block) == (s // block)), 1.0, 0.0).astype(BF16)
    hi = x.astype(BF16)
    lo = (x - hi.astype(F32)).astype(BF16)
    return _dot(tri, hi) + _dot(tri, lo)


def _block_last(x, block):
    rows = x.shape[0]
    parts = [jnp.broadcast_to(x[c * block + block - 1:c * block + block, :], (block, x.shape[1]))
             for c in range(rows // block)]
    return parts[0] if len(parts) == 1 else jnp.concatenate(parts, axis=0)


def _hgrn_matmuls(h, win_ref, d):
    return tuple(_dot(h, win_ref[:, n * d:(n + 1) * d]) for n in range(4))


def _hgrn_gates(pre, lower, block):
    q, f, v, g = pre
    q = _silu(q)
    sg = _silu(g)
    sig = jax.nn.sigmoid(f)
    forget = lower + (1.0 - lower) * sig
    k = (1.0 - lower) * (1.0 - sig)
    b = _block_cumsum(jnp.log(forget), block) * math.log2(math.e)
    b_last = _block_last(b, block)
    return q * jnp.exp2(b), k * jnp.exp2(-b), k * jnp.exp2(b_last - b), v, sg, jnp.exp2(b_last)


def _hgrn_project(h, win_ref, lower, d, block):
    return _hgrn_gates(_hgrn_matmuls(h, win_ref, d), lower, block)


def _hgrn_prompt_kernel(x_ref, mod_ref, g_ref, win_ref, lb_ref, ng_ref, wout_ref, y_ref, st_ref,
                        state_scr, o_scr, *, layer):
    i = pl.program_id(1)
    rows, d = x_ref.shape
    heads = d // HG_WIDTH
    w = HG_WIDTH

    @pl.when(i == 0)
    def _():
        state_scr[...] = jnp.zeros(state_scr.shape, F32)

    x = x_ref[...]
    h = _mod_in(x, g_ref[0:1, :], mod_ref, 0, 1).astype(BF16)
    lower = _hgrn_lower(lb_ref, layer)
    t = lax.broadcasted_iota(jnp.int32, (CHUNK, CHUNK), 0)
    s = lax.broadcasted_iota(jnp.int32, (CHUNK, CHUNK), 1)
    causal = t >= s
    ch = lambda hh: slice(hh * w, (hh + 1) * w)
    sub = min(rows, SUB_ROWS)
    n_chunks = sub // CHUNK
    cells = [(c, hh) for c in range(n_chunks) for hh in range(heads)]
    rc = lambda c: slice(c * CHUNK, (c + 1) * CHUNK)
    states = [state_scr[hh] for hh in range(heads)]
    starts = list(range(0, rows, sub))
    pre = {}

    def project(r0):
        parts = []
        for n in range(4):
            parts.append(_dot(h[r0:r0 + sub], win_ref[:, n * d:(n + 1) * d]))
            yield
        pre[r0] = tuple(parts)

    def mix(r0):
        q_dec, k_inv, k_end, v, sg, e_last = _hgrn_gates(pre.pop(r0), lower, CHUNK)
        yield
        qd16, ki16, ke16, v16 = (a.astype(BF16) for a in (q_dec, k_inv, k_end, v))
        kv = {(c, hh): _dot_tn(v16[rc(c), ch(hh)], ke16[rc(c), ch(hh)]) for c, hh in cells}
        scores = {(c, hh): jnp.where(causal, _dot_nt(qd16[rc(c), ch(hh)], ki16[rc(c), ch(hh)]),
                                     0.0).astype(BF16) for c, hh in cells}
        yield
        state_in = {}
        for hh in range(heads):
            for c in range(n_chunks):
                state_in[c, hh] = states[hh].astype(BF16)
                states[hh] = e_last[c * CHUNK:c * CHUNK + 1, ch(hh)] * states[hh] + kv[c, hh]
        o = [jnp.concatenate([_dot(scores[c, hh], v16[rc(c), ch(hh)])
                              + _dot_nt(qd16[rc(c), ch(hh)], state_in[c, hh]) for c in range(n_chunks)],
                             axis=0) for hh in range(heads)]
        yield
        o_scr[r0:r0 + sub, :] = jnp.concatenate(
            [(_rms_rows(o[hh]) * ng_ref[:, ch(hh)] * sg[:, ch(hh)]).astype(BF16) for hh in range(heads)], axis=1)

    _interleave(project(starts[0]))
    for n, r0 in enumerate(starts):
        nxt = [project(starts[n + 1])] if n + 1 < len(starts) else []
        _interleave(mix(r0), *nxt)
    for hh in range(heads):
        state_scr[hh] = states[hh]
    y_ref[...] = _resid_out(x, _dot(o_scr[...], wout_ref[...]), g_ref[1:2, :], mod_ref[2])

    @pl.when(i == pl.num_programs(1) - 1)
    def _():
        for hh in range(heads):
            st_ref[hh] = state_scr[hh].T


def _hgrn_prompt(x, mod, gains, w_in, norm_g, w_out, lower_bounds, layer, tm):
    b, s, d = x.shape
    tm = min(tm, s)
    heads = d // HG_WIDTH
    x_spec, mod_spec = _row_specs(x, mod, tm)
    st_spec = pl.BlockSpec((None, heads, HG_WIDTH, HG_WIDTH), lambda bb, i: (bb, 0, 0, 0))
    vmem = _nbytes(w_in, w_out) + 40 * tm * d * 4 + (8 << 20)
    return pl.pallas_call(
        functools.partial(_hgrn_prompt_kernel, layer=layer),
        grid=(b, s // tm),
        in_specs=[x_spec, mod_spec, _const_spec(gains.shape), _const_spec(w_in.shape),
                  _const_spec(lower_bounds.shape), _const_spec((1, d)), _const_spec(w_out.shape)],
        out_specs=[x_spec, st_spec],
        out_shape=[jax.ShapeDtypeStruct(x.shape, F32),
                   jax.ShapeDtypeStruct((b, heads, HG_WIDTH, HG_WIDTH), F32)],
        scratch_shapes=[pltpu.VMEM((heads, HG_WIDTH, HG_WIDTH), F32), pltpu.VMEM((tm, d), BF16)],
        compiler_params=_params(2, vmem),
        name="hgrn_prompt",
    )(x, mod, gains, w_in, lower_bounds, norm_g.reshape(1, d), w_out)


def _hgrn_sample_kernel(x_ref, mod_ref, g_ref, win_ref, lb_ref, ng_ref, st_in_ref, wout_ref,
                        y_ref, st_out_ref, qd_scr, ke_scr, v_scr, sg_scr, el_scr, o_scr,
                        *, layer, rows):
    b = pl.program_id(0)
    total, d = x_ref.shape
    heads = d // HG_WIDTH
    w = HG_WIDTH

    @pl.when(b == 0)
    def _():
        h = _mod_in(x_ref[...], g_ref[0:1, :], mod_ref, 0, 1).astype(BF16)
        q_dec, k_inv, k_end, v, sg, e_last = _hgrn_project(h, win_ref, _hgrn_lower(lb_ref, layer), d, rows)
        qd_scr[...] = q_dec
        ke_scr[...] = k_end
        v_scr[...] = v
        sg_scr[...] = sg
        el_scr[...] = e_last
        t = lax.broadcasted_iota(jnp.int32, (total, total), 0)
        s = lax.broadcasted_iota(jnp.int32, (total, total), 1)
        ok = (t >= s) & ((t // rows) == (s // rows))
        for hh in range(heads):
            ch = slice(hh * w, (hh + 1) * w)
            scores = jnp.where(ok, _dot_nt(q_dec[:, ch].astype(BF16), k_inv[:, ch].astype(BF16)), 0.0)
            o_scr[:, ch] = _dot(scores.astype(BF16), v[:, ch].astype(BF16))

    r0 = pl.multiple_of(b * rows, rows)
    ta = lax.broadcasted_iota(jnp.int32, (total, 1), 0)
    mine = (ta >= r0) & (ta < r0 + rows)
    ch = lambda hh: slice(hh * w, (hh + 1) * w)
    states = [st_in_ref[hh] for hh in range(heads)]
    cross = [_dot(qd_scr[pl.ds(r0, rows), ch(hh)].astype(BF16), states[hh].astype(BF16)) for hh in range(heads)]
    kv_t = [_dot_tn(v_scr[:, ch(hh)].astype(BF16), jnp.where(mine, ke_scr[:, ch(hh)], 0.0).astype(BF16))
            for hh in range(heads)]
    for hh in range(heads):
        o_scr[pl.ds(r0, rows), ch(hh)] += cross[hh]
        st_out_ref[hh] = (el_scr[pl.ds(r0, 1), ch(hh)] * states[hh].T + kv_t[hh]).T

    @pl.when(b == pl.num_programs(0) - 1)
    def _():
        outs = []
        for hh in range(heads):
            ch = slice(hh * w, (hh + 1) * w)
            outs.append((_rms_rows(o_scr[:, ch]) * ng_ref[:, ch] * sg_scr[:, ch]).astype(BF16))
        oc = jnp.concatenate(outs, axis=1)
        y_ref[...] = _resid_out(x_ref[...], _dot(oc, wout_ref[...]), g_ref[1:2, :], mod_ref[2])


def _hgrn_sample(x, mod, gains, w_in, norm_g, w_out, lower_bounds, layer, state, rows):
    total, d = x.shape
    bs, heads, dk, dv = state.shape
    st_spec = pl.BlockSpec((None, heads, dk, dv), lambda b: (b, 0, 0, 0))
    scr = pltpu.VMEM((total, d), F32)
    vmem = _nbytes(w_in, w_out, x, x, mod) + 60 * total * d * 4 + (8 << 20)
    return pl.pallas_call(
        functools.partial(_hgrn_sample_kernel, layer=layer, rows=rows),
        grid=(bs,),
        in_specs=[_const_spec(x.shape), _const_spec(mod.shape), _const_spec(gains.shape),
                  _const_spec(w_in.shape), _const_spec(lower_bounds.shape), _const_spec((1, d)),
                  st_spec, _const_spec(w_out.shape)],
        out_specs=[pl.BlockSpec(x.shape, lambda b: (0, 0)), st_spec],
        out_shape=[jax.ShapeDtypeStruct(x.shape, F32), jax.ShapeDtypeStruct(state.shape, F32)],
        scratch_shapes=[scr, scr, scr, scr, scr, scr],
        compiler_params=_params(1, vmem),
        name="hgrn_sample",
    )(x, mod, gains, w_in, lower_bounds, norm_g.reshape(1, d), state, w_out)


def kernel(x_prompt, x_sample, cache_k_diff, cache_v_diff, state_retention, state_hgrn, c_prompt, c_sample, w_ada, b_ada, norm_gains, gmlp_w_in, gmlp_ln_g, gmlp_ln_b, gmlp_w_s, gmlp_b_s, gmlp_w_out, diff_w_in, diff_lambda, diff_subln, diff_w_out, ret_w_in, ret_w_out, hgrn_w_in, hgrn_norm, hgrn_w_out, hgrn_lower_bounds, ffn_w_in, ffn_w_out):
    bp, s, d = x_prompt.shape
    bs, ls, _ = x_sample.shape
    ms = bs * ls
    depth = w_ada.shape[0]
    past = cache_k_diff.shape[2]
    n_mix = 4

    m_all = _ada(jnp.concatenate([c_prompt, c_sample], axis=0), w_ada, b_ada)
    ffn_in16, ffn_out16 = ffn_w_in.astype(BF16), ffn_w_out.astype(BF16)
    yp = x_prompt
    ys = x_sample.reshape(1, ms, d)
    outs = {name: [] for name in ("gv", "kp", "vp", "ks", "vs", "rp", "rs", "hp", "hs")}
    for i in range(depth):
        kind, j = i % n_mix, i // n_mix
        m = m_all[i].reshape(bp + bs, 6, d)
        mod_p = m[:bp].reshape(bp, 6, 1, d)
        mod_s = jnp.repeat(m[bp:], ls, axis=0).transpose(1, 0, 2).reshape(1, 6, ms, d)
        gains = norm_gains[i]
        if kind == 0:
            w_in, w_out = gmlp_w_in[j].astype(BF16), gmlp_w_out[j].astype(BF16)
            args = (gmlp_ln_g[j], gmlp_ln_b[j], gmlp_w_s[j], gmlp_b_s[j], w_out)
            yp = _gmlp(yp, mod_p, gains, w_in, *args, tm=512, sub=SUB_ROWS, t_chunk=GMLP_CHUNK, emit_v=False)
            ys, v_rows = _gmlp(ys, mod_s, gains, w_in, *args, tm=ms, sub=ms, t_chunk=ls, emit_v=True)
            outs["gv"].append(v_rows.reshape(bs, ls, -1))
        elif kind == 1:
            lam_init = 0.8 - 0.6 * math.exp(-0.3 * i)
            w_in, w_out = diff_w_in[j].astype(BF16), diff_w_out[j].astype(BF16)
            hk, hv = 2 * DIFF_HEADS, DIFF_HEADS
            scale = (d // hk) ** -0.5
            q_t, k, v, k16, v_t, qn2, kn2, vmax = _qkv(yp, mod_p, gains, w_in, tm=512,
                                                       scale=scale * math.log2(math.e), transposed=True)
            outs["kp"].append(k.reshape(bp, s, hk, d // hk))
            outs["vp"].append(v.reshape(bp, s, hv, d // hv))
            yp = _flash_prompt(q_t, k16, v_t, qn2[:, :, 0, 0], kn2[:, :, 0, 0], vmax[:, :, 0, 0], yp, mod_p,
                               gains, diff_lambda[j], diff_subln[j], w_out, lam_init, tq=512)
            q, k, v, k16, v16 = _qkv(ys, mod_s, gains, w_in, tm=512, scale=scale, transposed=False)
            outs["ks"].append(k.reshape(bs, ls, hk, d // hk))
            outs["vs"].append(v.reshape(bs, ls, hv, d // hv))
            cache_k_t = cache_k_diff[j].transpose(0, 2, 3, 1).reshape(bs, d, past)
            cache_v = cache_v_diff[j].reshape(bs, past * hv, d // hv)
            oc = _flash_sample(q[0], cache_k_t, cache_v, k16[0], v16[0], diff_lambda[j], diff_subln[j],
                               lam_init, rows=ls, tk=1024)
            ys = _outproj(oc[None], ys, mod_s, gains, w_out, tm=512)
        elif kind == 2:
            w_in, w_out = ret_w_in[j].astype(BF16), ret_w_out[j].astype(BF16)
            yp, st = _ret_prompt(yp, mod_p, gains, w_in, w_out, tm=512)
            outs["rp"].append(st)
            y2, st = _ret_sample(ys[0], mod_s[0, :3], gains, w_in, w_out, state_retention[j], rows=ls, past=past)
            ys = y2[None]
            outs["rs"].append(st)
        else:
            w_in, w_out = hgrn_w_in[j].astype(BF16), hgrn_w_out[j].astype(BF16)
            yp, st = _hgrn_prompt(yp, mod_p, gains, w_in, hgrn_norm[j], w_out, hgrn_lower_bounds, i, tm=512)
            outs["hp"].append(st)
            y2, st = _hgrn_sample(ys[0], mod_s[0, :3], gains, w_in, hgrn_norm[j], w_out, hgrn_lower_bounds, i,
                                  state_hgrn[j], rows=ls)
            ys = y2[None]
            outs["hs"].append(st)
        yp = _ffn(yp, mod_p, gains, ffn_in16, ffn_out16, i, tm=1024, sub=SUB_ROWS)
        ys = _ffn(ys, mod_s, gains, ffn_in16, ffn_out16, i, tm=512, sub=SUB_ROWS)

    return (yp, ys.reshape(bs, ls, d), jnp.stack(outs["gv"]), jnp.stack(outs["kp"]), jnp.stack(outs["vp"]),
            jnp.stack(outs["ks"]), jnp.stack(outs["vs"]), jnp.stack(outs["rp"]), jnp.stack(outs["rs"]),
            jnp.stack(outs["hp"]), jnp.stack(outs["hs"]))
```

```python
import functools
import math

import numpy as np
import jax
import jax.numpy as jnp
from jax import lax
from jax.experimental import pallas as pl
from jax.experimental.pallas import tpu as pltpu

F32 = jnp.float32
BF16 = jnp.bfloat16
EPS = 1e-6
NEG_INF = -1e30

CHUNK = 64
GMLP_CHUNK = 128
DIFF_HEADS = 8
RET_HEADS = 4
HG_WIDTH = 128
LANES = 128
MXU_WIDTH = 256
SUB_ROWS = 256
VMEM_CAP = 60 << 20
SOFTMAX_SAFE_LOG2 = 110.0

_NT = (((1,), (1,)), ((), ()))
_TN = (((0,), (0,)), ((), ()))


def _dot(a, b):
    return jnp.dot(a, b, preferred_element_type=F32)


def _dot_nt(a, b):
    return lax.dot_general(a, b, _NT, preferred_element_type=F32)


def _dot_tn(a, b):
    return lax.dot_general(a, b, _TN, preferred_element_type=F32)


def _silu(x):
    return x * jax.nn.sigmoid(x)


def _gelu(x):
    return 0.5 * x * (1.0 + lax.erf(x * (2.0 ** -0.5)))


def _rms_rows(x):
    return x * lax.rsqrt(jnp.mean(x * x, axis=-1, keepdims=True) + EPS)


def _mod_in(x, g, mod_ref, k_shift, k_scale):
    return _rms_rows(x) * g * (1.0 + mod_ref[k_scale]) + mod_ref[k_shift]


def _resid_out(x, o, g, gate):
    return x + gate * (_rms_rows(o) * g)


def _params(n_grid, vmem_bytes):
    return pltpu.CompilerParams(
        dimension_semantics=("arbitrary",) * n_grid,
        vmem_limit_bytes=int(min(max(vmem_bytes, 32 << 20), VMEM_CAP)))


def _const_spec(shape):
    nd = len(shape)
    return pl.BlockSpec(shape, lambda *_: (0,) * nd, pipeline_mode=pl.Buffered(1))


def _row_specs(x, mod, tm):
    _, _, d = x.shape
    r = mod.shape[2]
    x_spec = pl.BlockSpec((None, tm, d), lambda b, i, *_: (b, i, 0))
    if r == 1:
        mod_spec = pl.BlockSpec((None, 6, 1, d), lambda b, i, *_: (b, 0, 0, 0))
    else:
        mod_spec = pl.BlockSpec((None, 6, tm, d), lambda b, i, *_: (b, 0, i, 0))
    return x_spec, mod_spec


def _interleave(*stages):
    live = list(stages)
    while live:
        for g in list(live):
            try:
                next(g)
            except StopIteration:
                live.remove(g)


def _nbytes(*arrays):
    return sum(int(np.prod(a.shape)) * jnp.dtype(a.dtype).itemsize for a in arrays)


def _ada_kernel(c_ref, w_ref, b_ref, o_ref):
    a = _silu(c_ref[...]).astype(BF16)
    o_ref[...] = _dot(a, w_ref[...].astype(BF16)) + b_ref[...]


def _ada(c_all, w_ada, b_ada):
    depth, d, n = w_ada.shape
    rows = c_all.shape[0]
    tn = n // 4
    return pl.pallas_call(
        _ada_kernel,
        grid=(depth, n // tn),
        in_specs=[pl.BlockSpec((rows, d), lambda l, j: (0, 0)),
                  pl.BlockSpec((None, d, tn), lambda l, j: (l, 0, j)),
                  pl.BlockSpec((None, 1, tn), lambda l, j: (l, 0, j))],
        out_specs=pl.BlockSpec((None, rows, tn), lambda l, j: (l, 0, j)),
        out_shape=jax.ShapeDtypeStruct((depth, rows, n), F32),
        compiler_params=_params(2, 3 * d * tn * 4 + (8 << 20)),
        name="ada_mod",
    )(c_all, w_ada, b_ada.reshape(depth, 1, n))


def _ffn_kernel(x_ref, mod_ref, g_ref, win_ref, wout_ref, y_ref, *, hidden, chunks, sub):
    rows = x_ref.shape[0]
    per_row = mod_ref.shape[1] != 1
    for r0 in range(0, rows, sub):
        rs = slice(r0, r0 + sub)
        x = x_ref[rs, :]
        shift, scale, gate_out = (mod_ref[k, rs, :] if per_row else mod_ref[k] for k in (3, 4, 5))
        h = (_rms_rows(x) * g_ref[2:3, :] * (1.0 + scale) + shift).astype(BF16)
        acc = None
        for c0, cw in chunks:
            gate = _dot(h, win_ref[:, c0:c0 + cw])
            up = _dot(h, win_ref[:, hidden + c0:hidden + c0 + cw])
            act = (_silu(gate) * up).astype(BF16)
            part = _dot(act, wout_ref[c0:c0 + cw, :])
            acc = part if acc is None else acc + part
        y_ref[rs, :] = _resid_out(x, acc, g_ref[3:4, :], gate_out)


def _split_chunks(total, width):
    out, c0 = [], 0
    while c0 < total:
        out.append((c0, min(width, total - c0)))
        c0 += width
    return tuple(out)


def _layer_spec(shape, layer):
    nd = len(shape) - 1
    return pl.BlockSpec((None,) + tuple(shape[1:]), lambda *_: (layer,) + (0,) * nd,
                        pipeline_mode=pl.Buffered(1))


def _ffn(x, mod, gains, w_in, w_out, layer, tm, sub):
    b, s, d = x.shape
    hidden = w_out.shape[1]
    tm = min(tm, s)
    sub = min(sub, tm)
    x_spec, mod_spec = _row_specs(x, mod, tm)
    vmem = _nbytes(w_in[0], w_out[0]) + 6 * tm * d * 4 + 4 * tm * 1024 * 4 + (8 << 20)
    return pl.pallas_call(
        functools.partial(_ffn_kernel, hidden=hidden, chunks=_split_chunks(hidden, 4 * MXU_WIDTH), sub=sub),
        grid=(b, s // tm),
        in_specs=[x_spec, mod_spec, _const_spec(gains.shape), _layer_spec(w_in.shape, layer),
                  _layer_spec(w_out.shape, layer)],
        out_specs=x_spec,
        out_shape=jax.ShapeDtypeStruct(x.shape, F32),
        compiler_params=_params(2, vmem),
        name="ffn",
    )(x, mod, gains, w_in, w_out)


def _gmlp_kernel(x_ref, mod_ref, g_ref, win_ref, lng_ref, lnb_ref, wbd_ref, brow_ref, wout_ref,
                 *out_and_scratch, half, groups, pair, emit_v):
    if emit_v:
        y_ref, vn_ref, v_scr = out_and_scratch
    else:
        y_ref, v_scr = out_and_scratch
    gw = half // groups
    cw = pair * gw
    nblk = half // cw
    rows = x_ref.shape[0]
    sub = wbd_ref.shape[1]
    x = x_ref[...]
    h_all = _mod_in(x, g_ref[0:1, :], mod_ref, 0, 1).astype(BF16)
    for r0 in range(0, rows, sub):
        rs = slice(r0, r0 + sub)
        h = h_all[rs]
        s1 = None
        s2 = None
        for j in range(nblk):
            v = _gelu(_dot(h, win_ref[:, half + j * cw:half + (j + 1) * cw]))
            v_scr[rs, j * cw:(j + 1) * cw] = v
            a1 = jnp.sum(v, axis=-1, keepdims=True)
            a2 = jnp.sum(v * v, axis=-1, keepdims=True)
            s1 = a1 if s1 is None else s1 + a1
            s2 = a2 if s2 is None else s2 + a2
        mu = s1 * (1.0 / half)
        rstd = lax.rsqrt(s2 * (1.0 / half) - mu * mu + EPS)
        acc = None
        for j in range(nblk):
            cols = slice(j * cw, (j + 1) * cw)
            vn = (v_scr[rs, cols] - mu) * rstd * lng_ref[:, cols] + lnb_ref[:, cols]
            if emit_v:
                vn_ref[rs, cols] = vn
            vnb = vn.astype(BF16)
            u = _gelu(_dot(h, win_ref[:, cols]))
            mixed = []
            for gg in range(pair):
                g = j * pair + gg
                mixed.append(_dot(wbd_ref[g], vnb[:, gg * gw:(gg + 1) * gw]) + brow_ref[:, g:g + 1])
            out = (u * jnp.concatenate(mixed, axis=1)).astype(BF16)
            part = _dot(out, wout_ref[cols, :])
            acc = part if acc is None else acc + part
        gate = mod_ref[2] if mod_ref.shape[1] == 1 else mod_ref[2, rs, :]
        y_ref[rs, :] = _resid_out(x[rs], acc, g_ref[1:2, :], gate)


def _gmlp(x, mod, gains, w_in, ln_g, ln_b, w_s, b_s, w_out, tm, sub, t_chunk, emit_v):
    b, s, d = x.shape
    half = w_out.shape[0]
    groups = w_s.shape[0]
    tm = min(tm, s)
    sub = min(sub, tm)
    n_rep = sub // t_chunk
    pos = np.arange(sub)
    expand = jnp.asarray(pos[:, None] % t_chunk == np.arange(t_chunk)[None, :], w_s.dtype)
    same = jnp.asarray(pos[:, None] // t_chunk == pos[None, :] // t_chunk)
    wt = jnp.tril(w_s[:, :t_chunk, :t_chunk])
    w_bd = jnp.einsum("rt,gts,cs->grc", expand, wt, expand, precision=lax.Precision.HIGHEST)
    w_bd = jnp.where(same[None], w_bd, 0.0).astype(BF16)
    b_rows = jnp.tile(b_s[:, :t_chunk].T, (n_rep, 1))
    x_spec, mod_spec = _row_specs(x, mod, tm)
    out_shape = [jax.ShapeDtypeStruct(x.shape, F32)]
    out_specs = [x_spec]
    if emit_v:
        out_shape.append(jax.ShapeDtypeStruct((b, s, half), F32))
        out_specs.append(pl.BlockSpec((None, tm, half), lambda bb, i: (bb, i, 0)))
    vmem = (_nbytes(w_in, w_out, w_bd) + tm * half * 4 * (5 if emit_v else 1)
            + 6 * tm * d * 4 + (12 << 20))
    res = pl.pallas_call(
        functools.partial(_gmlp_kernel, half=half, groups=groups, pair=2, emit_v=emit_v),
        grid=(b, s // tm),
        in_specs=[x_spec, mod_spec, _const_spec(gains.shape), _const_spec(w_in.shape),
                  _const_spec((1, half)), _const_spec((1, half)), _const_spec(w_bd.shape),
                  _const_spec(b_rows.shape), _const_spec(w_out.shape)],
        out_specs=out_specs,
        out_shape=out_shape,
        scratch_shapes=[pltpu.VMEM((tm, half), F32)],
        compiler_params=_params(2, vmem),
        name="gmlp_v" if emit_v else "gmlp",
    )(x, mod, gains, w_in, ln_g.reshape(1, half), ln_b.reshape(1, half), w_bd, b_rows, w_out)
    return res if emit_v else res[0]


def _outproj_kernel(o_ref, x_ref, mod_ref, g_ref, w_ref, y_ref):
    y_ref[...] = _resid_out(x_ref[...], _dot(o_ref[...], w_ref[...]), g_ref[1:2, :], mod_ref[2])


def _outproj(o, x, mod, gains, w_out, tm):
    b, s, d = x.shape
    tm = min(tm, s)
    k = o.shape[-1]
    x_spec, mod_spec = _row_specs(x, mod, tm)
    return pl.pallas_call(
        _outproj_kernel,
        grid=(b, s // tm),
        in_specs=[pl.BlockSpec((None, tm, k), lambda bb, i: (bb, i, 0)), x_spec, mod_spec,
                  _const_spec(gains.shape), _const_spec(w_out.shape)],
        out_specs=x_spec,
        out_shape=jax.ShapeDtypeStruct(x.shape, F32),
        compiler_params=_params(2, _nbytes(w_out) + 8 * tm * d * 4 + 2 * tm * k * 2 + (8 << 20)),
        name="outproj",
    )(o, x, mod, gains, w_out)


def _max_head_norm2(x_t, heads):
    d, rows = x_t.shape
    n2 = jnp.sum((x_t * x_t).reshape(heads, d // heads, rows), axis=1)
    return jnp.full((8, LANES), jnp.max(n2), F32)


def _qkv_kernel(x_ref, mod_ref, g_ref, w_ref, q_ref, k_ref, v_ref, k16_ref, v16_ref, *norm_refs,
                scale, transposed):
    d = x_ref.shape[-1]
    h = _mod_in(x_ref[...], g_ref[0:1, :], mod_ref, 0, 1).astype(BF16)
    q = _dot(h, w_ref[:, :d]) * scale
    k = _dot(h, w_ref[:, d:2 * d])
    k_ref[...] = k
    k16_ref[...] = k.astype(BF16)
    v = _dot(h, w_ref[:, 2 * d:])
    v_ref[...] = v
    if transposed:
        qn_ref, kn_ref, vmax_ref = norm_refs
        q_t = q.T
        q_ref[...] = q_t.astype(BF16)
        v16_ref[...] = v.T.astype(BF16)
        qn_ref[...] = _max_head_norm2(q_t, 2 * DIFF_HEADS)
        kn_ref[...] = _max_head_norm2(k.T, 2 * DIFF_HEADS)
        vmax_ref[...] = jnp.full((8, LANES), jnp.max(jnp.abs(v)), F32)
    else:
        q_ref[...] = q.astype(BF16)
        v16_ref[...] = v.astype(BF16)


def _qkv(x, mod, gains, w_in, tm, scale, transposed):
    b, s, d = x.shape
    tm = min(tm, s)
    x_spec, mod_spec = _row_specs(x, mod, tm)
    sds = jax.ShapeDtypeStruct
    out_specs = [x_spec, x_spec, x_spec, x_spec, x_spec]
    out_shape = [sds(x.shape, BF16), sds(x.shape, F32), sds(x.shape, F32), sds(x.shape, BF16),
                 sds(x.shape, BF16)]
    if transposed:
        t_spec = pl.BlockSpec((None, d, tm), lambda bb, i: (bb, 0, i))
        n_spec = pl.BlockSpec((None, None, 8, LANES), lambda bb, i: (bb, i, 0, 0))
        out_specs[0] = out_specs[4] = t_spec
        out_shape[0] = out_shape[4] = sds((b, d, s), BF16)
        out_specs += [n_spec, n_spec, n_spec]
        out_shape += [sds((b, s // tm, 8, LANES), F32)] * 3
    return pl.pallas_call(
        functools.partial(_qkv_kernel, scale=scale, transposed=transposed),
        grid=(b, s // tm),
        in_specs=[x_spec, mod_spec, _const_spec(gains.shape), _const_spec(w_in.shape)],
        out_specs=out_specs,
        out_shape=out_shape,
        compiler_params=_params(2, _nbytes(w_in) + 24 * tm * d * 4 + (8 << 20)),
        name="diff_qkv_t" if transposed else "diff_qkv",
    )(x, mod, gains, w_in)


def _diff_lambda(lam_ref, lam_init):
    lp = lam_ref[...]
    e1 = jnp.exp(jnp.sum(lp[0:1, :] * lp[1:2, :], axis=-1, keepdims=True))
    e2 = jnp.exp(jnp.sum(lp[2:3, :] * lp[3:4, :], axis=-1, keepdims=True))
    return e1 - e2 + lam_init


def _stack_q_halves(q, qs_scr, rows):
    lane = lax.broadcasted_iota(jnp.int32, q.shape, 1)
    first = (lane & (LANES - 1)) < (LANES // 2)
    zero = jnp.zeros_like(q)
    qs_scr[0:rows, :] = jnp.where(first, q, zero)
    qs_scr[rows:2 * rows, :] = jnp.where(first, zero, q)


def _flash_init(m_scr, l_scr, acc_scr):
    m_scr[...] = jnp.full(m_scr.shape, NEG_INF, F32)
    l_scr[...] = jnp.zeros(l_scr.shape, F32)
    acc_scr[...] = jnp.zeros(acc_scr.shape, F32)


def _flash_step(qs_scr, scores, v_blk, m_scr, l_scr, acc_scr, mask):
    sl = lambda h: slice(LANES * h, LANES * (h + 1))
    s_all = [scores(h, qs_scr[:, sl(h)]) for h in range(DIFF_HEADS)]
    stats = []
    for h, s in enumerate(s_all):
        if mask is not None:
            s = jnp.where(mask, s, NEG_INF)
        m_old = m_scr[h]
        m_new = jnp.maximum(m_old, jnp.max(s, axis=1, keepdims=True))
        alpha = jnp.exp(m_old - m_new)
        p = jnp.exp(s - m_new)
        l_scr[h] = alpha * l_scr[h] + jnp.sum(p, axis=1, keepdims=True)
        m_scr[h] = m_new
        stats.append((alpha, p.astype(BF16)))
    for h, (alpha, p) in enumerate(stats):
        acc_scr[:, sl(h)] = alpha * acc_scr[:, sl(h)] + _dot(p, v_blk(h))


def _flash_finish(l_scr, acc_scr, lam, subln, lam_init, rows):
    outs = []
    for h in range(DIFF_HEADS):
        sl = slice(LANES * h, LANES * (h + 1))
        l = l_scr[h]
        o = acc_scr[0:rows, sl] / l[0:rows] - lam * (acc_scr[rows:2 * rows, sl] / l[rows:2 * rows])
        outs.append((_rms_rows(o) * subln * (1.0 - lam_init)).astype(BF16))
    return jnp.concatenate(outs, axis=1)


def _chunk_mask(rows, cols, row_pos0, col_pos0):
    r = lax.broadcasted_iota(jnp.int32, (2 * rows, cols), 0)
    c = lax.broadcasted_iota(jnp.int32, (2 * rows, cols), 1)
    r = jnp.where(r >= rows, r - rows, r)
    return ((c + col_pos0) // CHUNK) <= ((r + row_pos0) // CHUNK)


def _flash_t_step(qz_scr, k_ref, vt_ref, m_scr, l_scr, acc_scr, mask, bounded=False):
    heads = [(h, c) for h in range(DIFF_HEADS) for c in range(2)]
    rows = lambda h: slice(LANES * h, LANES * (h + 1))

    def scores(i):
        h, c = heads[i]
        return _dot(k_ref[:, rows(h)], qz_scr[c, rows(h), :])

    def softmax(i, s):
        r = 2 * heads[i][0] + heads[i][1]
        if mask is not None:
            s = jnp.where(mask, s, NEG_INF)
        m_old = m_scr[r:r + 1, :]
        block_ref = 0.0 if bounded else jnp.max(s, axis=0, keepdims=True)
        m_new = jnp.maximum(m_old, block_ref)
        alpha = jnp.exp2(m_old - m_new)
        p = jnp.exp2(s - m_new)
        l_scr[r:r + 1, :] = alpha * l_scr[r:r + 1, :] + jnp.sum(p, axis=0, keepdims=True)
        m_scr[r:r + 1, :] = m_new
        return alpha, p.astype(BF16)

    def accumulate(i, alpha, p):
        h, c = heads[i]
        acc_scr[c, rows(h), :] = alpha * acc_scr[c, rows(h), :] + _dot(vt_ref[rows(h), :], p)

    n = len(heads)
    ahead, behind = 2, 1
    pending_s = {i: scores(i) for i in range(ahead)}
    pending_p = {}
    for i in range(n):
        if i + ahead < n:
            pending_s[i + ahead] = scores(i + ahead)
        pending_p[i] = softmax(i, pending_s.pop(i))
        if i - behind >= 0:
            accumulate(i - behind, *pending_p.pop(i - behind))
    for i in sorted(pending_p):
        accumulate(i, *pending_p[i])


def _flash_prompt_kernel(qt_ref, kt_ref, safe_ref, q_ref, k_ref, vt_ref, x_ref, mod_ref, g_ref, lam_ref,
                         subln_ref, woutt_ref, y_ref, qz_scr, m_scr, l_scr, acc_scr, *, lam_init):
    p = pl.program_id(1)
    qi = qt_ref[p]
    ki = kt_ref[p]
    bounded = safe_ref[pl.program_id(0) * pl.num_programs(1) + p] != 0
    tk = k_ref.shape[0]
    tq = q_ref.shape[1]

    @pl.when(ki == 0)
    def _():
        m_scr[...] = jnp.full(m_scr.shape, NEG_INF, F32)
        l_scr[...] = jnp.zeros(l_scr.shape, F32)
        acc_scr[...] = jnp.zeros(acc_scr.shape, F32)
        q = q_ref[...]
        feat = lax.broadcasted_iota(jnp.int32, q.shape, 0)
        first = (feat & (LANES - 1)) < (LANES // 2)
        zero = jnp.zeros_like(q)
        qz_scr[0] = jnp.where(first, q, zero)
        qz_scr[1] = jnp.where(first, zero, q)

    @pl.when((ki < qi) & bounded)
    def _():
        _flash_t_step(qz_scr, k_ref, vt_ref, m_scr, l_scr, acc_scr, None, bounded=True)

    @pl.when((ki < qi) & jnp.logical_not(bounded))
    def _():
        _flash_t_step(qz_scr, k_ref, vt_ref, m_scr, l_scr, acc_scr, None)

    def chunk_causal():
        key = lax.broadcasted_iota(jnp.int32, (tk, tq), 0)
        qry = lax.broadcasted_iota(jnp.int32, (tk, tq), 1)
        return (key // CHUNK) <= (qry // CHUNK)

    @pl.when((ki == qi) & bounded)
    def _():
        _flash_t_step(qz_scr, k_ref, vt_ref, m_scr, l_scr, acc_scr, chunk_causal(), bounded=True)

    @pl.when((ki == qi) & jnp.logical_not(bounded))
    def _():
        _flash_t_step(qz_scr, k_ref, vt_ref, m_scr, l_scr, acc_scr, chunk_causal())

    @pl.when(ki == qi)
    def _():
        lam = _diff_lambda(lam_ref, lam_init)
        outs = []
        for h in range(DIFF_HEADS):
            rows = slice(LANES * h, LANES * (h + 1))
            o = (acc_scr[0, rows, :] / l_scr[2 * h:2 * h + 1, :]
                 - lam * (acc_scr[1, rows, :] / l_scr[2 * h + 1:2 * h + 2, :]))
            o = o * lax.rsqrt(jnp.mean(o * o, axis=0, keepdims=True) + EPS)
            outs.append((o * (subln_ref[...] * (1.0 - lam_init))).astype(BF16))
        out_t = _dot(woutt_ref[...], jnp.concatenate(outs, axis=0))
        y_ref[...] = _resid_out(x_ref[...], out_t.T, g_ref[1:2, :], mod_ref[2])


def _flash_prompt(q_t, k16, v_t, qn2, kn2, vmax, x, mod, gains, lam_p, subln, w_out, lam_init, tq):
    b, s, d = x.shape
    tq = min(tq, s)
    nq = s // tq
    assert qn2.shape == kn2.shape == vmax.shape == (b, nq)
    pairs = [(qi, ki) for qi in range(nq) for ki in range(qi + 1)]
    qt = jnp.asarray([pq for pq, _ in pairs], jnp.int32)
    kt = jnp.asarray([pk for _, pk in pairs], jnp.int32)
    limit = SOFTMAX_SAFE_LOG2 - math.log2(s) - jnp.log2(jnp.maximum(jnp.max(vmax, axis=1, keepdims=True), 1.0))
    safe = (limit > 0) & (qn2[:, qt] * kn2[:, kt] <= limit * limit)
    safe = safe.astype(jnp.int32).reshape(-1)
    x_spec = pl.BlockSpec((None, tq, d), lambda bb, p, qt_, kt_, safe_: (bb, qt_[p], 0))
    q_spec = pl.BlockSpec((None, d, tq), lambda bb, p, qt_, kt_, safe_: (bb, 0, qt_[p]))
    k_spec = pl.BlockSpec((None, tq, d), lambda bb, p, qt_, kt_, safe_: (bb, kt_[p], 0))
    v_spec = pl.BlockSpec((None, d, tq), lambda bb, p, qt_, kt_, safe_: (bb, 0, kt_[p]))
    mod_spec = pl.BlockSpec((None, 6, 1, d), lambda bb, p, qt_, kt_, safe_: (bb, 0, 0, 0))
    w_out_t = w_out.T
    vmem = (_nbytes(w_out) + 6 * tq * d * 2 + 4 * tq * d * 4 + 2 * tq * d * 2 + 2 * tq * d * 4
            + 8 * tq * tq * 4 + (8 << 20))
    grid_spec = pltpu.PrefetchScalarGridSpec(
        num_scalar_prefetch=3,
        grid=(b, len(pairs)),
        in_specs=[q_spec, k_spec, v_spec, x_spec, mod_spec, _const_spec(gains.shape),
                  _const_spec(lam_p.shape), _const_spec((LANES, 1)), _const_spec(w_out_t.shape)],
        out_specs=x_spec,
        scratch_shapes=[pltpu.VMEM((2, d, tq), BF16),
                        pltpu.VMEM((2 * DIFF_HEADS, tq), F32),
                        pltpu.VMEM((2 * DIFF_HEADS, tq), F32),
                        pltpu.VMEM((2, d, tq), F32)])
    return pl.pallas_call(
        functools.partial(_flash_prompt_kernel, lam_init=lam_init),
        grid_spec=grid_spec,
        out_shape=jax.ShapeDtypeStruct(x.shape, F32),
        compiler_params=_params(2, vmem),
        name="diff_flash_prompt",
    )(qt, kt, safe, q_t, k16, v_t, x, mod, gains, lam_p, subln.reshape(LANES, 1), w_out_t)


def _flash_sample_kernel(q_ref, ck_ref, cv_ref, kn_ref, vn_ref, lam_ref, subln_ref, o_ref,
                         qs_scr, kpad_scr, vpad_scr, m_scr, l_scr, acc_scr,
                         *, rows, past, lam_init, new_mask_needed):
    kb = pl.program_id(1)

    @pl.when(kb == 0)
    def _():
        _flash_init(m_scr, l_scr, acc_scr)
        _stack_q_halves(q_ref[...], qs_scr, rows)
        kpad_scr[...] = jnp.zeros(kpad_scr.shape, BF16)
        vpad_scr[...] = jnp.zeros(vpad_scr.shape, BF16)
        kpad_scr[0:rows, :] = kn_ref[...]
        vpad_scr[0:rows, :] = vn_ref[...]
        c = lax.broadcasted_iota(jnp.int32, (2 * rows, LANES), 1)
        mask = c < rows
        if new_mask_needed:
            mask = mask & _chunk_mask(rows, LANES, past, past)
        _flash_step(qs_scr,
                    lambda h, qp: _dot_nt(qp, kpad_scr[:, LANES * h:LANES * (h + 1)]),
                    lambda h: vpad_scr[:, LANES * h:LANES * (h + 1)],
                    m_scr, l_scr, acc_scr, mask)

    tk = ck_ref.shape[1]
    _flash_step(qs_scr,
                lambda h, qp: _dot(qp, ck_ref[LANES * h:LANES * (h + 1), :].astype(BF16)),
                lambda h: cv_ref[pl.ds(h, tk, stride=DIFF_HEADS), :].astype(BF16),
                m_scr, l_scr, acc_scr, None)

    @pl.when(kb == pl.num_programs(1) - 1)
    def _():
        lam = _diff_lambda(lam_ref, lam_init)
        o_ref[...] = _flash_finish(l_scr, acc_scr, lam, subln_ref[...], lam_init, rows)


def _flash_sample(q, cache_k_t, cache_v, k_new, v_new, lam_p, subln, lam_init, rows, tk):
    bs, d, past = cache_k_t.shape
    tk = min(tk, past)
    pos = past + np.arange(rows)
    new_mask_needed = not bool(np.all((pos[None, :] // CHUNK) <= (pos[:, None] // CHUNK)))
    row_spec = pl.BlockSpec((rows, d), lambda b, kb: (b, 0))
    k_spec = pl.BlockSpec((None, d, tk), lambda b, kb: (b, 0, kb))
    v_spec = pl.BlockSpec((None, tk * DIFF_HEADS, LANES), lambda b, kb: (b, kb, 0))
    vmem = 4 * tk * d * 4 + 4 * tk * d * 2 + (12 << 20)
    return pl.pallas_call(
        functools.partial(_flash_sample_kernel, rows=rows, past=past, lam_init=lam_init,
                          new_mask_needed=new_mask_needed),
        grid=(bs, past // tk),
        in_specs=[row_spec, k_spec, v_spec, row_spec, row_spec,
                  _const_spec(lam_p.shape), _const_spec((1, LANES))],
        out_specs=row_spec,
        out_shape=jax.ShapeDtypeStruct(q.shape, BF16),
        scratch_shapes=[pltpu.VMEM((2 * rows, d), BF16),
                        pltpu.VMEM((LANES, d), BF16), pltpu.VMEM((LANES, d), BF16),
                        pltpu.VMEM((DIFF_HEADS, 2 * rows, 1), F32),
                        pltpu.VMEM((DIFF_HEADS, 2 * rows, 1), F32),
                        pltpu.VMEM((2 * rows, d), F32)],
        compiler_params=_params(2, vmem),
        name="diff_flash_sample",
    )(q, cache_k_t, cache_v, k_new, v_new, lam_p, subln.reshape(1, LANES))


def _ret_log_gamma(h):
    return float(np.log(np.float32(1.0) - np.float32(2.0) ** np.float32(-5.0 - h)))


def _rotate_pairs(x):
    n = x.shape[-1]
    lane = lax.broadcasted_iota(jnp.int32, x.shape, 1)
    return jnp.where((lane & 1) == 0, -pltpu.roll(x, n - 1, 1), pltpu.roll(x, 1, 1))


def _ret_project(h, win_ref, cos, sin, d, dk):
    cos4 = jnp.concatenate([cos] * (d // dk), axis=1)
    sin4 = jnp.concatenate([sin] * (d // dk), axis=1)
    q = _dot(h, win_ref[:, 0:d])
    q = q * cos4 + _rotate_pairs(q) * sin4
    k = _dot(h, win_ref[:, d:2 * d])
    k = (k * cos4 + _rotate_pairs(k) * sin4) * (dk ** -0.5)
    v = _dot(h, win_ref[:, 2 * d:4 * d])
    return q, k, v


def _ret_decay(lg, rows, same_seq=None):
    t = lax.broadcasted_iota(jnp.int32, (rows, rows), 0)
    s = lax.broadcasted_iota(jnp.int32, (rows, rows), 1)
    ok = t >= s
    if same_seq is not None:
        ok = ok & ((t // same_seq) == (s // same_seq))
    diff = jnp.maximum(t - s, 0).astype(F32)
    return jnp.where(ok, jnp.exp(lg * diff), 0.0)


def _ret_prompt_kernel(x_ref, mod_ref, g_ref, win_ref, cos_ref, sin_ref, wout_ref, y_ref, st_ref,
                       state_scr, o_scr, *, heads):
    i = pl.program_id(1)
    rows, d = x_ref.shape
    dk = d // heads
    dv = 2 * dk

    @pl.when(i == 0)
    def _():
        state_scr[...] = jnp.zeros(state_scr.shape, F32)

    x = x_ref[...]
    h = _mod_in(x, g_ref[0:1, :], mod_ref, 0, 1).astype(BF16)
    sub = min(rows, SUB_ROWS)
    t = lax.broadcasted_iota(jnp.int32, (sub, 1), 0).astype(F32)
    lgs = [_ret_log_gamma(hh) for hh in range(heads)]
    states = [state_scr[hh] for hh in range(heads)]
    pre = {}

    def project(r0):
        hs = h[r0:r0 + sub]
        parts = []
        for c0, c1 in ((0, d), (d, 2 * d), (2 * d, 4 * d), (4 * d, 6 * d)):
            parts.append(_dot(hs, win_ref[:, c0:c1]))
            yield
        pre[r0] = tuple(parts)

    def mix(r0):
        q, k, v, g = pre.pop(r0)
        cos4 = jnp.concatenate([cos_ref[r0:r0 + sub, :]] * heads, axis=1)
        sin4 = jnp.concatenate([sin_ref[r0:r0 + sub, :]] * heads, axis=1)
        q = q * cos4 + _rotate_pairs(q) * sin4
        k = (k * cos4 + _rotate_pairs(k) * sin4) * (dk ** -0.5)
        sg = _silu(g)
        qs = [q[:, hh * dk:(hh + 1) * dk].astype(BF16) for hh in range(heads)]
        ks = [k[:, hh * dk:(hh + 1) * dk] for hh in range(heads)]
        vs = [v[:, hh * dv:(hh + 1) * dv].astype(BF16) for hh in range(heads)]
        yield
        scores = [_dot_nt(qs[hh], ks[hh].astype(BF16)) for hh in range(heads)]
        cross = [_dot(qs[hh], states[hh].astype(BF16)) for hh in range(heads)]
        kv = [_dot_tn((ks[hh] * jnp.exp(lgs[hh] * (sub - 1.0 - t))).astype(BF16), vs[hh])
              for hh in range(heads)]
        for hh in range(heads):
            states[hh] = math.exp(lgs[hh] * sub) * states[hh] + kv[hh]
        yield
        inner = [_dot((scores[hh] * _ret_decay(lgs[hh], sub)).astype(BF16), vs[hh]) for hh in range(heads)]
        yield
        gated = []
        for hh in range(heads):
            o = inner[hh] + cross[hh] * jnp.exp(lgs[hh] * (t + 1.0))
            gated.append((_rms_rows(o) * sg[:, hh * dv:(hh + 1) * dv]).astype(BF16))
        o_scr[r0:r0 + sub, :] = jnp.concatenate(gated, axis=1)

    starts = list(range(0, rows, sub))
    _interleave(project(starts[0]))
    for n, r0 in enumerate(starts):
        nxt = [project(starts[n + 1])] if n + 1 < len(starts) else []
        _interleave(mix(r0), *nxt)
    for hh in range(heads):
        state_scr[hh] = states[hh]
    y_ref[...] = _resid_out(x, _dot(o_scr[...], wout_ref[...]), g_ref[1:2, :], mod_ref[2])

    @pl.when(i == pl.num_programs(1) - 1)
    def _():
        st_ref[...] = state_scr[...]


def _xpos_tables(pos, dk):
    inv = 1.0 / (10000.0 ** jnp.linspace(0.0, 1.0, dk // 2, dtype=F32))
    ang = pos.astype(F32)[:, None] * jnp.repeat(inv, 2)[None, :]
    return jnp.cos(ang), jnp.sin(ang)


def _ret_prompt(x, mod, gains, w_in, w_out, tm):
    b, s, d = x.shape
    tm = min(tm, s)
    heads = RET_HEADS
    dk = d // heads
    dv = 2 * dk
    cos, sin = _xpos_tables(jnp.arange(s), dk)
    x_spec, mod_spec = _row_specs(x, mod, tm)
    tab_spec = pl.BlockSpec((tm, dk), lambda bb, i: (i, 0))
    st_spec = pl.BlockSpec((None, heads, dk, dv), lambda bb, i: (bb, 0, 0, 0))
    vmem = _nbytes(w_in, w_out) + 3 * heads * dk * dv * 4 + 40 * tm * d * 4 + (8 << 20)
    return pl.pallas_call(
        functools.partial(_ret_prompt_kernel, heads=heads),
        grid=(b, s // tm),
        in_specs=[x_spec, mod_spec, _const_spec(gains.shape), _const_spec(w_in.shape),
                  tab_spec, tab_spec, _const_spec(w_out.shape)],
        out_specs=[x_spec, st_spec],
        out_shape=[jax.ShapeDtypeStruct(x.shape, F32),
                   jax.ShapeDtypeStruct((b, heads, dk, dv), F32)],
        scratch_shapes=[pltpu.VMEM((heads, dk, dv), F32), pltpu.VMEM((tm, heads * dv), BF16)],
        compiler_params=_params(2, vmem),
        name="ret_prompt",
    )(x, mod, gains, w_in, cos, sin, w_out)


def _ret_sample_kernel(x_ref, mod_ref, g_ref, win_ref, cos_ref, sin_ref, st_in_ref, wout_ref,
                       y_ref, st_out_ref, q_scr, k_scr, v_scr, o_scr, *, heads, rows):
    b = pl.program_id(0)
    total, d = x_ref.shape
    dk = d // heads
    dv = 2 * dk

    @pl.when(b == 0)
    def _():
        h = _mod_in(x_ref[...], g_ref[0:1, :], mod_ref, 0, 1).astype(BF16)
        q, k, v = _ret_project(h, win_ref, cos_ref[...], sin_ref[...], d, dk)
        q_scr[...] = q.astype(BF16)
        k_scr[...] = k
        v_scr[...] = v.astype(BF16)
        for hh in range(heads):
            scores = (_dot_nt(q[:, hh * dk:(hh + 1) * dk].astype(BF16),
                              k[:, hh * dk:(hh + 1) * dk].astype(BF16))
                      * _ret_decay(_ret_log_gamma(hh), total, same_seq=rows))
            o_scr[:, hh * dv:(hh + 1) * dv] = _dot(scores.astype(BF16),
                                                   v[:, hh * dv:(hh + 1) * dv].astype(BF16))

    r0 = pl.multiple_of(b * rows, rows)
    t = lax.broadcasted_iota(jnp.int32, (rows, 1), 0).astype(F32)
    ta = lax.broadcasted_iota(jnp.int32, (total, 1), 0)
    mine = (ta >= r0) & (ta < r0 + rows)
    t_all = (ta - r0).astype(F32)
    lgs = [_ret_log_gamma(hh) for hh in range(heads)]
    states = [st_in_ref[hh] for hh in range(heads)]
    cross = [_dot(q_scr[pl.ds(r0, rows), hh * dk:(hh + 1) * dk], states[hh].astype(BF16)) for hh in range(heads)]
    kv = [_dot_tn(jnp.where(mine, k_scr[:, hh * dk:(hh + 1) * dk] * jnp.exp(lgs[hh] * (rows - 1.0 - t_all)),
                            0.0).astype(BF16), v_scr[:, hh * dv:(hh + 1) * dv]) for hh in range(heads)]
    for hh in range(heads):
        o_scr[pl.ds(r0, rows), hh * dv:(hh + 1) * dv] += cross[hh] * jnp.exp(lgs[hh] * (t + 1.0))
        st_out_ref[hh] = math.exp(lgs[hh] * rows) * states[hh] + kv[hh]

    @pl.when(b == pl.num_programs(0) - 1)
    def _():
        h = _mod_in(x_ref[...], g_ref[0:1, :], mod_ref, 0, 1).astype(BF16)
        gated = []
        for hh in range(heads):
            sl = slice(hh * dv, (hh + 1) * dv)
            sg = _silu(_dot(h, win_ref[:, 4 * d + hh * dv:4 * d + (hh + 1) * dv]))
            gated.append((_rms_rows(o_scr[:, sl]) * sg).astype(BF16))
        oc = jnp.concatenate(gated, axis=1)
        y_ref[...] = _resid_out(x_ref[...], _dot(oc, wout_ref[...]), g_ref[1:2, :], mod_ref[2])


def _ret_sample(x, mod, gains, w_in, w_out, state, rows, past):
    total, d = x.shape
    bs, heads, dk, dv = state.shape
    cos, sin = _xpos_tables(past + jnp.arange(rows), dk)
    cos = jnp.tile(cos, (bs, 1))
    sin = jnp.tile(sin, (bs, 1))
    st_spec = pl.BlockSpec((None, heads, dk, dv), lambda b: (b, 0, 0, 0))
    vmem = _nbytes(w_in, w_out, x, x, mod) + 4 * heads * dk * dv * 4 + 60 * total * d * 4 + (8 << 20)
    return pl.pallas_call(
        functools.partial(_ret_sample_kernel, heads=heads, rows=rows),
        grid=(bs,),
        in_specs=[_const_spec(x.shape), _const_spec(mod.shape), _const_spec(gains.shape),
                  _const_spec(w_in.shape), _const_spec(cos.shape), _const_spec(sin.shape),
                  st_spec, _const_spec(w_out.shape)],
        out_specs=[pl.BlockSpec(x.shape, lambda b: (0, 0)), st_spec],
        out_shape=[jax.ShapeDtypeStruct(x.shape, F32), jax.ShapeDtypeStruct(state.shape, F32)],
        scratch_shapes=[pltpu.VMEM((total, d), BF16), pltpu.VMEM((total, d), F32),
                        pltpu.VMEM((total, 2 * d), BF16), pltpu.VMEM((total, 2 * d), F32)],
        compiler_params=_params(1, vmem),
        name="ret_sample",
    )(x, mod, gains, w_in, cos, sin, state, w_out)


def _hgrn_lower(lb_ref, layer):
    lb = lb_ref[...]
    e = jnp.exp(lb - jnp.max(lb, axis=0, keepdims=True))
    p = e / jnp.sum(e, axis=0, keepdims=True)
    if layer == 0:
        return jnp.zeros_like(p[0:1, :])
    return jnp.sum(p[1:layer + 1, :], axis=0, keepdims=True)


def _block_cumsum(x, block):
    rows = x.shape[0]
    t = lax.broadcasted_iota(jnp.int32, (rows, rows), 0)
    s = lax.broadcasted_iota(jnp.int32, (rows, rows), 1)
    tri = jnp.where((s <= t) & ((t // block) == (s // block)), 1.0, 0.0).astype(BF16)
    hi = x.astype(BF16)
    lo = (x - hi.astype(F32)).astype(BF16)
    return _dot(tri, hi) + _dot(tri, lo)


def _block_last(x, block):
    rows = x.shape[0]
    parts = [jnp.broadcast_to(x[c * block + block - 1:c * block + block, :], (block, x.shape[1]))
             for c in range(rows // block)]
    return parts[0] if len(parts) == 1 else jnp.concatenate(parts, axis=0)


def _hgrn_matmuls(h, win_ref, d):
    return tuple(_dot(h, win_ref[:, n * d:(n + 1) * d]) for n in range(4))


def _hgrn_gates(pre, lower, block):
    q, f, v, g = pre
    q = _silu(q)
    sg = _silu(g)
    sig = jax.nn.sigmoid(f)
    forget = lower + (1.0 - lower) * sig
    k = (1.0 - lower) * (1.0 - sig)
    b = _block_cumsum(jnp.log(forget), block) * math.log2(math.e)
    b_last = _block_last(b, block)
    return q * jnp.exp2(b), k * jnp.exp2(-b), k * jnp.exp2(b_last - b), v, sg, jnp.exp2(b_last)


def _hgrn_project(h, win_ref, lower, d, block):
    return _hgrn_gates(_hgrn_matmuls(h, win_ref, d), lower, block)


def _hgrn_prompt_kernel(x_ref, mod_ref, g_ref, win_ref, lb_ref, ng_ref, wout_ref, y_ref, st_ref,
                        state_scr, o_scr, *, layer):
    i = pl.program_id(1)
    rows, d = x_ref.shape
    heads = d // HG_WIDTH
    w = HG_WIDTH

    @pl.when(i == 0)
    def _():
        state_scr[...] = jnp.zeros(state_scr.shape, F32)

    x = x_ref[...]
    h = _mod_in(x, g_ref[0:1, :], mod_ref, 0, 1).astype(BF16)
    lower = _hgrn_lower(lb_ref, layer)
    t = lax.broadcasted_iota(jnp.int32, (CHUNK, CHUNK), 0)
    s = lax.broadcasted_iota(jnp.int32, (CHUNK, CHUNK), 1)
    causal = t >= s
    ch = lambda hh: slice(hh * w, (hh + 1) * w)
    sub = min(rows, SUB_ROWS)
    n_chunks = sub // CHUNK
    cells = [(c, hh) for c in range(n_chunks) for hh in range(heads)]
    rc = lambda c: slice(c * CHUNK, (c + 1) * CHUNK)
    states = [state_scr[hh] for hh in range(heads)]
    starts = list(range(0, rows, sub))
    pre = {}

    def project(r0):
        parts = []
        for n in range(4):
            parts.append(_dot(h[r0:r0 + sub], win_ref[:, n * d:(n + 1) * d]))
            yield
        pre[r0] = tuple(parts)

    def mix(r0):
        q_dec, k_inv, k_end, v, sg, e_last = _hgrn_gates(pre.pop(r0), lower, CHUNK)
        yield
        qd16, ki16, ke16, v16 = (a.astype(BF16) for a in (q_dec, k_inv, k_end, v))
        kv = {(c, hh): _dot_tn(v16[rc(c), ch(hh)], ke16[rc(c), ch(hh)]) for c, hh in cells}
        scores = {(c, hh): jnp.where(causal, _dot_nt(qd16[rc(c), ch(hh)], ki16[rc(c), ch(hh)]),
                                     0.0).astype(BF16) for c, hh in cells}
        yield
        state_in = {}
        for hh in range(heads):
            for c in range(n_chunks):
                state_in[c, hh] = states[hh].astype(BF16)
                states[hh] = e_last[c * CHUNK:c * CHUNK + 1, ch(hh)] * states[hh] + kv[c, hh]
        o = [jnp.concatenate([_dot(scores[c, hh], v16[rc(c), ch(hh)])
                              + _dot_nt(qd16[rc(c), ch(hh)], state_in[c, hh]) for c in range(n_chunks)],
                             axis=0) for hh in range(heads)]
        yield
        o_scr[r0:r0 + sub, :] = jnp.concatenate(
            [(_rms_rows(o[hh]) * ng_ref[:, ch(hh)] * sg[:, ch(hh)]).astype(BF16) for hh in range(heads)], axis=1)

    _interleave(project(starts[0]))
    for n, r0 in enumerate(starts):
        nxt = [project(starts[n + 1])] if n + 1 < len(starts) else []
        _interleave(mix(r0), *nxt)
    for hh in range(heads):
        state_scr[hh] = states[hh]
    y_ref[...] = _resid_out(x, _dot(o_scr[...], wout_ref[...]), g_ref[1:2, :], mod_ref[2])

    @pl.when(i == pl.num_programs(1) - 1)
    def _():
        for hh in range(heads):
            st_ref[hh] = state_scr[hh].T


def _hgrn_prompt(x, mod, gains, w_in, norm_g, w_out, lower_bounds, layer, tm):
    b, s, d = x.shape
    tm = min(tm, s)
    heads = d // HG_WIDTH
    x_spec, mod_spec = _row_specs(x, mod, tm)
    st_spec = pl.BlockSpec((None, heads, HG_WIDTH, HG_WIDTH), lambda bb, i: (bb, 0, 0, 0))
    vmem = _nbytes(w_in, w_out) + 40 * tm * d * 4 + (8 << 20)
    return pl.pallas_call(
        functools.partial(_hgrn_prompt_kernel, layer=layer),
        grid=(b, s // tm),
        in_specs=[x_spec, mod_spec, _const_spec(gains.shape), _const_spec(w_in.shape),
                  _const_spec(lower_bounds.shape), _const_spec((1, d)), _const_spec(w_out.shape)],
        out_specs=[x_spec, st_spec],
        out_shape=[jax.ShapeDtypeStruct(x.shape, F32),
                   jax.ShapeDtypeStruct((b, heads, HG_WIDTH, HG_WIDTH), F32)],
        scratch_shapes=[pltpu.VMEM((heads, HG_WIDTH, HG_WIDTH), F32), pltpu.VMEM((tm, d), BF16)],
        compiler_params=_params(2, vmem),
        name="hgrn_prompt",
    )(x, mod, gains, w_in, lower_bounds, norm_g.reshape(1, d), w_out)


def _hgrn_sample_kernel(x_ref, mod_ref, g_ref, win_ref, lb_ref, ng_ref, st_in_ref, wout_ref,
                        y_ref, st_out_ref, qd_scr, ke_scr, v_scr, sg_scr, el_scr, o_scr,
                        *, layer, rows):
    b = pl.program_id(0)
    total, d = x_ref.shape
    heads = d // HG_WIDTH
    w = HG_WIDTH

    @pl.when(b == 0)
    def _():
        h = _mod_in(x_ref[...], g_ref[0:1, :], mod_ref, 0, 1).astype(BF16)
        q_dec, k_inv, k_end, v, sg, e_last = _hgrn_project(h, win_ref, _hgrn_lower(lb_ref, layer), d, rows)
        qd_scr[...] = q_dec
        ke_scr[...] = k_end
        v_scr[...] = v
        sg_scr[...] = sg
        el_scr[...] = e_last
        t = lax.broadcasted_iota(jnp.int32, (total, total), 0)
        s = lax.broadcasted_iota(jnp.int32, (total, total), 1)
        ok = (t >= s) & ((t // rows) == (s // rows))
        for hh in range(heads):
            ch = slice(hh * w, (hh + 1) * w)
            scores = jnp.where(ok, _dot_nt(q_dec[:, ch].astype(BF16), k_inv[:, ch].astype(BF16)), 0.0)
            o_scr[:, ch] = _dot(scores.astype(BF16), v[:, ch].astype(BF16))

    r0 = pl.multiple_of(b * rows, rows)
    ta = lax.broadcasted_iota(jnp.int32, (total, 1), 0)
    mine = (ta >= r0) & (ta < r0 + rows)
    ch = lambda hh: slice(hh * w, (hh + 1) * w)
    states = [st_in_ref[hh] for hh in range(heads)]
    cross = [_dot(qd_scr[pl.ds(r0, rows), ch(hh)].astype(BF16), states[hh].astype(BF16)) for hh in range(heads)]
    kv_t = [_dot_tn(v_scr[:, ch(hh)].astype(BF16), jnp.where(mine, ke_scr[:, ch(hh)], 0.0).astype(BF16))
            for hh in range(heads)]
    for hh in range(heads):
        o_scr[pl.ds(r0, rows), ch(hh)] += cross[hh]
        st_out_ref[hh] = (el_scr[pl.ds(r0, 1), ch(hh)] * states[hh].T + kv_t[hh]).T

    @pl.when(b == pl.num_programs(0) - 1)
    def _():
        outs = []
        for hh in range(heads):
            ch = slice(hh * w, (hh + 1) * w)
            outs.append((_rms_rows(o_scr[:, ch]) * ng_ref[:, ch] * sg_scr[:, ch]).astype(BF16))
        oc = jnp.concatenate(outs, axis=1)
        y_ref[...] = _resid_out(x_ref[...], _dot(oc, wout_ref[...]), g_ref[1:2, :], mod_ref[2])


def _hgrn_sample(x, mod, gains, w_in, norm_g, w_out, lower_bounds, layer, state, rows):
    total, d = x.shape
    bs, heads, dk, dv = state.shape
    st_spec = pl.BlockSpec((None, heads, dk, dv), lambda b: (b, 0, 0, 0))
    scr = pltpu.VMEM((total, d), F32)
    vmem = _nbytes(w_in, w_out, x, x, mod) + 60 * total * d * 4 + (8 << 20)
    return pl.pallas_call(
        functools.partial(_hgrn_sample_kernel, layer=layer, rows=rows),
        grid=(bs,),
        in_specs=[_const_spec(x.shape), _const_spec(mod.shape), _const_spec(gains.shape),
                  _const_spec(w_in.shape), _const_spec(lower_bounds.shape), _const_spec((1, d)),
                  st_spec, _const_spec(w_out.shape)],
        out_specs=[pl.BlockSpec(x.shape, lambda b: (0, 0)), st_spec],
        out_shape=[jax.ShapeDtypeStruct(x.shape, F32), jax.ShapeDtypeStruct(state.shape, F32)],
        scratch_shapes=[scr, scr, scr, scr, scr, scr],
        compiler_params=_params(1, vmem),
        name="hgrn_sample",
    )(x, mod, gains, w_in, lower_bounds, norm_g.reshape(1, d), state, w_out)


def kernel(x_prompt, x_sample, cache_k_diff, cache_v_diff, state_retention, state_hgrn, c_prompt, c_sample, w_ada, b_ada, norm_gains, gmlp_w_in, gmlp_ln_g, gmlp_ln_b, gmlp_w_s, gmlp_b_s, gmlp_w_out, diff_w_in, diff_lambda, diff_subln, diff_w_out, ret_w_in, ret_w_out, hgrn_w_in, hgrn_norm, hgrn_w_out, hgrn_lower_bounds, ffn_w_in, ffn_w_out):
    bp, s, d = x_prompt.shape
    bs, ls, _ = x_sample.shape
    ms = bs * ls
    depth = w_ada.shape[0]
    past = cache_k_diff.shape[2]
    n_mix = 4

    m_all = _ada(jnp.concatenate([c_prompt, c_sample], axis=0), w_ada, b_ada)
    ffn_in16, ffn_out16 = ffn_w_in.astype(BF16), ffn_w_out.astype(BF16)
    yp = x_prompt
    ys = x_sample.reshape(1, ms, d)
    outs = {name: [] for name in ("gv", "kp", "vp", "ks", "vs", "rp", "rs", "hp", "hs")}
    for i in range(depth):
        kind, j = i % n_mix, i // n_mix
        m = m_all[i].reshape(bp + bs, 6, d)
        mod_p = m[:bp].reshape(bp, 6, 1, d)
        mod_s = jnp.repeat(m[bp:], ls, axis=0).transpose(1, 0, 2).reshape(1, 6, ms, d)
        gains = norm_gains[i]
        if kind == 0:
            w_in, w_out = gmlp_w_in[j].astype(BF16), gmlp_w_out[j].astype(BF16)
            args = (gmlp_ln_g[j], gmlp_ln_b[j], gmlp_w_s[j], gmlp_b_s[j], w_out)
            yp = _gmlp(yp, mod_p, gains, w_in, *args, tm=512, sub=SUB_ROWS, t_chunk=GMLP_CHUNK, emit_v=False)
            ys, v_rows = _gmlp(ys, mod_s, gains, w_in, *args, tm=ms, sub=ms, t_chunk=ls, emit_v=True)
            outs["gv"].append(v_rows.reshape(bs, ls, -1))
        elif kind == 1:
            lam_init = 0.8 - 0.6 * math.exp(-0.3 * i)
            w_in, w_out = diff_w_in[j].astype(BF16), diff_w_out[j].astype(BF16)
            hk, hv = 2 * DIFF_HEADS, DIFF_HEADS
            scale = (d // hk) ** -0.5
            q_t, k, v, k16, v_t, qn2, kn2, vmax = _qkv(yp, mod_p, gains, w_in, tm=512,
                                                       scale=scale * math.log2(math.e), transposed=True)
            outs["kp"].append(k.reshape(bp, s, hk, d // hk))
            outs["vp"].append(v.reshape(bp, s, hv, d // hv))
            yp = _flash_prompt(q_t, k16, v_t, qn2[:, :, 0, 0], kn2[:, :, 0, 0], vmax[:, :, 0, 0], yp, mod_p,
                               gains, diff_lambda[j], diff_subln[j], w_out, lam_init, tq=512)
            q, k, v, k16, v16 = _qkv(ys, mod_s, gains, w_in, tm=512, scale=scale, transposed=False)
            outs["ks"].append(k.reshape(bs, ls, hk, d // hk))
            outs["vs"].append(v.reshape(bs, ls, hv, d // hv))
            cache_k_t = cache_k_diff[j].transpose(0, 2, 3, 1).reshape(bs, d, past)
            cache_v = cache_v_diff[j].reshape(bs, past * hv, d // hv)
            oc = _flash_sample(q[0], cache_k_t, cache_v, k16[0], v16[0], diff_lambda[j], diff_subln[j],
                               lam_init, rows=ls, tk=2048)
            ys = _outproj(oc[None], ys, mod_s, gains, w_out, tm=512)
        elif kind == 2:
            w_in, w_out = ret_w_in[j].astype(BF16), ret_w_out[j].astype(BF16)
            yp, st = _ret_prompt(yp, mod_p, gains, w_in, w_out, tm=512)
            outs["rp"].append(st)
            y2, st = _ret_sample(ys[0], mod_s[0, :3], gains, w_in, w_out, state_retention[j], rows=ls, past=past)
            ys = y2[None]
            outs["rs"].append(st)
        else:
            w_in, w_out = hgrn_w_in[j].astype(BF16), hgrn_w_out[j].astype(BF16)
            yp, st = _hgrn_prompt(yp, mod_p, gains, w_in, hgrn_norm[j], w_out, hgrn_lower_bounds, i, tm=512)
            outs["hp"].append(st)
            y2, st = _hgrn_sample(ys[0], mod_s[0, :3], gains, w_in, hgrn_norm[j], w_out, hgrn_lower_bounds, i,
                                  state_hgrn[j], rows=ls)
            ys = y2[None]
            outs["hs"].append(st)
        yp = _ffn(yp, mod_p, gains, ffn_in16, ffn_out16, i, tm=1024, sub=SUB_ROWS)
        ys = _ffn(ys, mod_s, gains, ffn_in16, ffn_out16, i, tm=512, sub=SUB_ROWS)

    return (yp, ys.reshape(bs, ls, d), jnp.stack(outs["gv"]), jnp.stack(outs["kp"]), jnp.stack(outs["vp"]),
            jnp.stack(outs["ks"]), jnp.stack(outs["vs"]), jnp.stack(outs["rp"]), jnp.stack(outs["rs"]),
            jnp.stack(outs["hp"]), jnp.stack(outs["hs"]))
```
